```python
import math
import jax
import jax.numpy as jnp
from jax import lax
import numpy as np

D_MODEL = 1024
BATCH = 8
SEQ = 2048
DEPTH = 4
DEC_BATCH = 128
DEC_SEQ = 1
PAST_LEN = 2048
PAGE_SIZE = 128

HEAD_DIM = 64
N_HEADS = D_MODEL // HEAD_DIM
MOBA_HEADS = N_HEADS // 4
NSA_HEADS = N_HEADS - MOBA_HEADS
NSA_KV = 2
NSA_GROUP = NSA_HEADS // NSA_KV
DSA_HEADS = N_HEADS
DSA_KV = 4
DSA_GROUP = DSA_HEADS // DSA_KV
IDX_HEADS = 8
IDX_DIM = 64
MOBA_BLOCK = 256
MOBA_TOPK = 3
CMP_LEN = 32
CMP_STRIDE = 16
CMP_HIDDEN = 128
SEL_BLOCK = 64
N_SEL = 8
WINDOW = 512
DSA_TOPK = 256
N_BUCKETS = 32
MAX_DISTANCE = 128
D_FF = 2816
Q_BLOCK = 128
MOBA_Q_BLOCK = 32
N_EVEN = (DEPTH + 1) // 2
N_ODD = DEPTH // 2
MOBA_W = MOBA_HEADS * HEAD_DIM
NSA_W = NSA_HEADS * HEAD_DIM
NSA_KW = NSA_KV * HEAD_DIM
DSA_W = DSA_HEADS * HEAD_DIM
DSA_KW = DSA_KV * HEAD_DIM
EVEN_SIZES = (MOBA_W, MOBA_W, MOBA_W, NSA_W, NSA_KW, NSA_KW, NSA_KW, NSA_KW, NSA_KW, NSA_KW, NSA_HEADS * 3)
ODD_SIZES = (DSA_W, DSA_KW, DSA_KW, IDX_HEADS * IDX_DIM, IDX_DIM, IDX_HEADS)
EVEN_IN = sum(EVEN_SIZES)
ODD_IN = sum(ODD_SIZES)
RMS_EPS = 1e-6
NEG_INF = -1e30
TINY = 1e-30
SEL_FORCE = 1e4

kernel_name = 'hybrid_moba_nsa_dsa_macaron_step'


def rmsnorm(x, g):
    xf = x.astype(jnp.float32)
    y = xf * lax.rsqrt(jnp.mean(xf * xf, axis=-1, keepdims=True) + RMS_EPS)
    return (y * g.astype(jnp.float32)).astype(x.dtype)


def t5_bucket(dist):
    n = jnp.maximum(dist, 0)
    exact = N_BUCKETS // 2
    nf = jnp.maximum(n, 1).astype(jnp.float32)
    large = exact + (jnp.log(nf / exact) / math.log(MAX_DISTANCE / exact) * (N_BUCKETS - exact)).astype(jnp.int32)
    return jnp.where(n < exact, n, jnp.minimum(large, N_BUCKETS - 1))


def masked_softmax(logits, mask):
    logits = jnp.where(mask, logits.astype(jnp.float32), NEG_INF)
    m = jnp.max(logits, axis=-1, keepdims=True)
    e = jnp.where(mask, jnp.exp(logits - m), 0.0)
    return e / jnp.maximum(jnp.sum(e, axis=-1, keepdims=True), TINY)


def split_cols(z, sizes):
    return jnp.split(z, np.cumsum(sizes)[:-1].tolist(), axis=-1)


def map_query_blocks(fn, xs, n_tok, block):
    n = n_tok // block
    xs_b = tuple(jnp.swapaxes(a.reshape((a.shape[0], n, block) + a.shape[2:]), 0, 1) for a in xs)
    pos = jnp.arange(n_tok, dtype=jnp.int32).reshape(n, block)
    out = lax.map(lambda args: fn(*args), xs_b + (pos,))
    out = jnp.swapaxes(out, 0, 1)
    return out.reshape((out.shape[0], n_tok) + out.shape[3:])


def gather_past(cache, page_table, layer):
    rows = cache[page_table, :, layer]
    return rows.reshape((rows.shape[0], rows.shape[1] * rows.shape[2]) + rows.shape[3:])


def adaln(c, w, b):
    return (jax.nn.silu(c) @ w + b).reshape(c.shape[0], 3, 3, D_MODEL)


def modulated_norm(x, g, mod, s):
    return rmsnorm(x, g) * (1.0 + mod[:, s, 1][:, None, :]) + mod[:, s, 0][:, None, :]


def swiglu(h, w_in, w_out):
    a, g = jnp.split(h @ w_in, 2, axis=-1)
    return (jax.nn.silu(g) * a) @ w_out


def ffn_half(x, mod, s, g, w_in, w_out):
    return x + 0.5 * mod[:, s, 2][:, None, :] * swiglu(modulated_norm(x, g, mod, s), w_in, w_out)


def moba_blocks(k, v):
    B, L, H, dh = k.shape
    nb = -(-L // MOBA_BLOCK)
    pad = ((0, 0), (0, nb * MOBA_BLOCK - L), (0, 0), (0, 0))
    kb = jnp.pad(k, pad).reshape(B, nb, MOBA_BLOCK, H, dh)
    vb = jnp.pad(v, pad).reshape(B, nb, MOBA_BLOCK, H, dh)
    k_mean = jnp.mean(kb.astype(jnp.float32), axis=2)
    return kb.transpose(0, 3, 1, 2, 4), vb.transpose(0, 3, 1, 2, 4), k_mean


def moba_core(q, q_pos, kt, vt, k_mean, tab):
    B, Tq, H, dh = q.shape
    nb = kt.shape[2]
    own = q_pos // MOBA_BLOCK
    gate = jnp.einsum('bthd,bnhd->bthn', q, k_mean).astype(jnp.float32)
    past = jnp.arange(nb)[None, :] < own[:, None]
    gate = jnp.where(past[None, :, None, :], gate, NEG_INF)
    kk = min(MOBA_TOPK, nb)
    _, sel = lax.top_k(gate, kk)
    own_b = jnp.broadcast_to(own[None, :, None, None], (B, Tq, H, 1)).astype(sel.dtype)
    blocks = jnp.concatenate([sel, own_b], axis=-1)
    slot_ok = jnp.concatenate([jnp.arange(kk)[None, :] < jnp.minimum(own, MOBA_TOPK)[:, None],
                               jnp.ones((Tq, 1), dtype=bool)], axis=-1)
    ns = kk + 1
    b_ix = jnp.arange(B)[:, None, None, None]
    h_ix = jnp.arange(H)[None, None, :, None]
    kg = kt[b_ix, h_ix, blocks]
    vg = vt[b_ix, h_ix, blocks]
    key_pos = blocks[..., None] * MOBA_BLOCK + jnp.arange(MOBA_BLOCK)
    dist = q_pos[None, :, None, None, None] - key_pos
    mask = slot_ok[None, :, None, :, None] & (dist >= 0)
    bias = tab[h_ix[..., None], t5_bucket(dist)]
    logits = jnp.einsum('bthd,bthnsd->bthns', q, kg).astype(jnp.float32) * dh ** -0.5 + bias
    p = masked_softmax(logits.reshape(B, Tq, H, ns * MOBA_BLOCK), mask.reshape(B, Tq, H, ns * MOBA_BLOCK))
    return jnp.einsum('bthk,bthkd->bthd', p, vg.reshape(B, Tq, H, ns * MOBA_BLOCK, dh))


def nsa_compress(raw, pe, w1, w2):
    B, L, G, dh = raw.shape
    nc = (L - CMP_LEN) // CMP_STRIDE + 1
    idx = jnp.arange(nc)[:, None] * CMP_STRIDE + jnp.arange(CMP_LEN)[None, :]
    blk = raw[:, idx] + pe[None, None, :, None, :]
    flat = blk.transpose(0, 1, 3, 2, 4).reshape(B, nc, G, CMP_LEN * dh)
    return jax.nn.gelu(flat @ w1) @ w2


def nsa_summaries(k_raw, v_raw, pe, w1, w2, g):
    ck = rmsnorm(nsa_compress(k_raw, pe[0], w1[0], w2[0]), g)
    cv = nsa_compress(v_raw, pe[1], w1[1], w2[1])
    return ck, cv


def sel_blocks(x):
    B, L, G, dh = x.shape
    ns = -(-L // SEL_BLOCK)
    xp = jnp.pad(x, ((0, 0), (0, ns * SEL_BLOCK - L), (0, 0), (0, 0)))
    return xp.reshape(B, ns, SEL_BLOCK, G, dh).transpose(0, 3, 1, 2, 4)


def nsa_core(q, gates, q_pos, ck, cv, skt, svt, kw, vw, w_pos, tab):
    B, Tq, G, HG, dh = q.shape
    scale = dh ** -0.5
    nc = ck.shape[1]
    c_start = jnp.arange(nc) * CMP_STRIDE
    c_end = c_start + CMP_LEN - 1
    c_dist = q_pos[:, None] - c_end[None, :]
    c_bias = jnp.moveaxis(tab[:, :, t5_bucket(c_dist)], 2, 0)
    lc = jnp.einsum('btghd,bcgd->btghc', q, ck).astype(jnp.float32) * scale + c_bias[None]
    pc = masked_softmax(lc, (c_dist >= 0)[None, :, None, None, :])
    o_cmp = jnp.einsum('btghc,bcgd->btghd', pc, cv)
    ns = skt.shape[2]
    s_start = jnp.arange(ns) * SEL_BLOCK
    overlap = ((c_start[:, None] < s_start[None, :] + SEL_BLOCK)
               & (c_start[:, None] + CMP_LEN > s_start[None, :])).astype(jnp.float32)
    imp = jnp.einsum('btgc,cn->btgn', jnp.sum(pc, axis=3), overlap)
    cur = q_pos // SEL_BLOCK
    j = jnp.arange(ns)[None, :]
    forced = (j == 0) | (j == cur[:, None]) | (j == cur[:, None] - 1)
    admissible = j <= cur[:, None]
    score = jnp.where(forced[None, :, None, :], SEL_FORCE,
                      jnp.where(admissible[None, :, None, :], imp, NEG_INF))
    ks = min(N_SEL, ns)
    _, sel = lax.top_k(score, ks)
    b_ix = jnp.arange(B)[:, None, None, None]
    g_ix = jnp.arange(G)[None, None, :, None]
    kg = skt[b_ix, g_ix, sel]
    vg = svt[b_ix, g_ix, sel]
    key_pos = sel[..., None] * SEL_BLOCK + jnp.arange(SEL_BLOCK)
    s_dist = q_pos[None, :, None, None, None] - key_pos
    s_bias = tab[jnp.arange(G)[:, None, None, None], jnp.arange(HG)[:, None, None],
                 t5_bucket(s_dist)[:, :, :, None]]
    ls = jnp.einsum('btghd,btgksd->btghks', q, kg).astype(jnp.float32) * scale + s_bias
    ps = masked_softmax(ls.reshape(B, Tq, G, HG, ks * SEL_BLOCK),
                        (s_dist >= 0).reshape(B, Tq, G, 1, ks * SEL_BLOCK))
    o_slc = jnp.einsum('btghk,btgkd->btghd', ps, vg.reshape(B, Tq, G, ks * SEL_BLOCK, dh))
    w_dist = q_pos[:, None] - w_pos[None, :]
    w_mask = (w_dist >= 0) & (w_dist < WINDOW) & (w_pos[None, :] >= 0)
    w_bias = jnp.moveaxis(tab[:, :, t5_bucket(w_dist)], 2, 0)
    lw = jnp.einsum('btghd,bsgd->btghs', q, kw).astype(jnp.float32) * scale + w_bias[None]
    pw = masked_softmax(lw, w_mask[None, :, None, None, :])
    o_win = jnp.einsum('btghs,bsgd->btghd', pw, vw)
    return gates[..., 0:1] * o_cmp + gates[..., 1:2] * o_slc + gates[..., 2:3] * o_win


def even_project(h, w_in, moba_g, nsa_g):
    B, T, _ = h.shape
    mq, mk, mv, nq, ck, cv, sk, sv, wk, wv, gl = split_cols(h @ w_in, EVEN_SIZES)
    heads = lambda a, n: a.reshape(B, T, n, HEAD_DIM)
    mq = rmsnorm(heads(mq, MOBA_HEADS), moba_g[0])
    mk = rmsnorm(heads(mk, MOBA_HEADS), moba_g[1])
    nq = rmsnorm(nq.reshape(B, T, NSA_KV, NSA_GROUP, HEAD_DIM), nsa_g[0])
    sk = rmsnorm(heads(sk, NSA_KV), nsa_g[2])
    wk = rmsnorm(heads(wk, NSA_KV), nsa_g[3])
    gates = jax.nn.sigmoid(gl.reshape(B, T, NSA_KV, NSA_GROUP, 3).astype(jnp.float32))
    return (mq, mk, heads(mv, MOBA_HEADS), nq, heads(ck, NSA_KV), heads(cv, NSA_KV),
            sk, heads(sv, NSA_KV), wk, heads(wv, NSA_KV), gates)


def even_merge(o_m, o_n, w_out, h):
    B, T = h.shape[:2]
    o = jnp.concatenate([o_m.reshape(B, T, MOBA_W), o_n.reshape(B, T, NSA_W)], axis=-1)
    return o.astype(h.dtype) @ w_out


def even_mixer_prompt(h, w_in, w_out, moba_g, nsa_g, pe, w1, w2, tab_m, tab_n):
    mq, mk, mv, nq, ck_raw, cv_raw, sk, sv, wk, wv, gates = even_project(h, w_in, moba_g, nsa_g)
    B, T = h.shape[:2]
    kt, vt, km = moba_blocks(mk, mv)
    o_m = map_query_blocks(lambda qb, pos: moba_core(qb, pos, kt, vt, km, tab_m), (mq,), T, MOBA_Q_BLOCK)
    ck, cv = nsa_summaries(ck_raw, cv_raw, pe, w1, w2, nsa_g[1])
    skt, svt = sel_blocks(sk), sel_blocks(sv)
    wpad = ((0, 0), (WINDOW, 0), (0, 0), (0, 0))
    wk_pad, wv_pad = jnp.pad(wk, wpad), jnp.pad(wv, wpad)

    def nsa_block(qb, gb, pos):
        n = qb.shape[1]
        p0 = pos[0]
        kw = lax.dynamic_slice_in_dim(wk_pad, p0, WINDOW + n, axis=1)
        vw = lax.dynamic_slice_in_dim(wv_pad, p0, WINDOW + n, axis=1)
        w_pos = p0 - WINDOW + jnp.arange(WINDOW + n)
        return nsa_core(qb, gb, pos, ck, cv, skt, svt, kw, vw, w_pos, tab_n)

    o_n = map_query_blocks(nsa_block, (nq, gates), T, Q_BLOCK)
    wb = min(WINDOW, T)
    return even_merge(o_m, o_n, w_out, h), (mk, mv, ck_raw, cv_raw, sk, sv, wk[:, T - wb:], wv[:, T - wb:])


def even_mixer_sample(h, past, win_k_buf, win_v_buf, w_in, w_out, moba_g, nsa_g, pe, w1, w2, tab_m, tab_n):
    mq, mk, mv, nq, ck_raw, cv_raw, sk, sv, wk, wv, gates = even_project(h, w_in, moba_g, nsa_g)
    T = h.shape[1]
    q_pos = PAST_LEN + jnp.arange(T)
    pk_m, pv_m, pk_c, pv_c, pk_s, pv_s = past
    cat = lambda a, b: jnp.concatenate([a, b], axis=1)
    kt, vt, km = moba_blocks(cat(pk_m, mk), cat(pv_m, mv))
    o_m = moba_core(mq, q_pos, kt, vt, km, tab_m)
    ck, cv = nsa_summaries(cat(pk_c, ck_raw), cat(pv_c, cv_raw), pe, w1, w2, nsa_g[1])
    skt, svt = sel_blocks(cat(pk_s, sk)), sel_blocks(cat(pv_s, sv))
    kw, vw = cat(win_k_buf, wk), cat(win_v_buf, wv)
    wb = win_k_buf.shape[1]
    w_pos = PAST_LEN - wb + jnp.arange(wb + T)
    o_n = nsa_core(nq, gates, q_pos, ck, cv, skt, svt, kw, vw, w_pos, tab_n)
    keep = min(WINDOW, PAST_LEN + T)
    return even_merge(o_m, o_n, w_out, h), (mk, mv, ck_raw, cv_raw, sk, sv, kw[:, -keep:], vw[:, -keep:])


def odd_project(h, w_in, qk_g):
    B, T, _ = h.shape
    q, k, v, qi, ki, wi = split_cols(h @ w_in, ODD_SIZES)
    q = rmsnorm(q.reshape(B, T, DSA_KV, DSA_GROUP, HEAD_DIM), qk_g[0])
    k = rmsnorm(k.reshape(B, T, DSA_KV, HEAD_DIM), qk_g[1])
    return q, k, v.reshape(B, T, DSA_KV, HEAD_DIM), qi.reshape(B, T, IDX_HEADS, IDX_DIM), ki, wi


def dsa_core(q, qi, wi, q_pos, k, v, ki, kk, tab):
    B, Tq, G, HG, dh = q.shape
    L = k.shape[1]
    s = jnp.einsum('bthe,ble->bthl', qi, ki).astype(jnp.float32) * IDX_DIM ** -0.5
    score = jnp.einsum('bth,bthl->btl', wi.astype(jnp.float32), jax.nn.relu(s)) * IDX_HEADS ** -0.5
    score = jnp.where(jnp.arange(L)[None, None, :] <= q_pos[None, :, None], score, NEG_INF)
    _, sel = lax.top_k(score, kk)
    b_ix = jnp.arange(B)[:, None, None]
    kg = k[b_ix, sel]
    vg = v[b_ix, sel]
    dist = q_pos[None, :, None] - sel
    bias = tab[jnp.arange(G)[:, None, None], jnp.arange(HG)[:, None],
               t5_bucket(dist)[:, :, None, None, :]]
    logits = jnp.einsum('btghd,btkgd->btghk', q, kg).astype(jnp.float32) * dh ** -0.5 + bias
    p = masked_softmax(logits, (dist >= 0)[:, :, None, None, :])
    return jnp.einsum('btghk,btkgd->btghd', p, vg)


def odd_mixer_prompt(h, w_in, w_out, qk_g, tab):
    q, k, v, qi, ki, wi = odd_project(h, w_in, qk_g)
    B, T = h.shape[:2]
    kk = min(DSA_TOPK, T // 4)
    o = map_query_blocks(lambda qb, qib, wib, pos: dsa_core(qb, qib, wib, pos, k, v, ki, kk, tab),
                         (q, qi, wi), T, Q_BLOCK)
    return o.reshape(B, T, DSA_W).astype(h.dtype) @ w_out, (k, v, ki)


def odd_mixer_sample(h, past, w_in, w_out, qk_g, tab):
    q, k, v, qi, ki, wi = odd_project(h, w_in, qk_g)
    B, T = h.shape[:2]
    pk, pv, pki = past
    cat = lambda a, b: jnp.concatenate([a, b], axis=1)
    kk = min(DSA_TOPK, (PAST_LEN + T) // 4)
    o = dsa_core(q, qi, wi, PAST_LEN + jnp.arange(T), cat(pk, k), cat(pv, v), cat(pki, ki), kk, tab)
    return o.reshape(B, T, DSA_W).astype(h.dtype) @ w_out, (k, v, ki)


def setup_inputs(seed: int = 0) -> dict:
    key = jax.random.key(seed)
    ks = iter(jax.random.split(key, 40))
    nrm = lambda shape, s=1.0: jax.random.normal(next(ks), shape, jnp.float32) * s
    n_pages = PAST_LEN // PAGE_SIZE
    n_used = DEC_BATCH * n_pages
    n_pool = n_used + n_used // 4
    win_buf = min(WINDOW, PAST_LEN)
    page_table = jax.random.permutation(next(ks), n_pool)[:n_used].reshape(DEC_BATCH, n_pages).astype(jnp.int32)
    return {
        'x_prompt': nrm((BATCH, SEQ, D_MODEL)),
        'x_sample': nrm((DEC_BATCH, DEC_SEQ, D_MODEL)),
        'cache_moba_k': nrm((n_pool, PAGE_SIZE, N_EVEN, MOBA_HEADS, HEAD_DIM)),
        'cache_moba_v': nrm((n_pool, PAGE_SIZE, N_EVEN, MOBA_HEADS, HEAD_DIM)),
        'cache_nsa_cmp_k': nrm((n_pool, PAGE_SIZE, N_EVEN, NSA_KV, HEAD_DIM)),
        'cache_nsa_cmp_v': nrm((n_pool, PAGE_SIZE, N_EVEN, NSA_KV, HEAD_DIM)),
        'cache_nsa_slc_k': nrm((n_pool, PAGE_SIZE, N_EVEN, NSA_KV, HEAD_DIM)),
        'cache_nsa_slc_v': nrm((n_pool, PAGE_SIZE, N_EVEN, NSA_KV, HEAD_DIM)),
        'state_nsa_win_k': nrm((DEC_BATCH, win_buf, N_EVEN, NSA_KV, HEAD_DIM)),
        'state_nsa_win_v': nrm((DEC_BATCH, win_buf, N_EVEN, NSA_KV, HEAD_DIM)),
        'cache_dsa_k': nrm((n_pool, PAGE_SIZE, N_ODD, DSA_KV, HEAD_DIM)),
        'cache_dsa_v': nrm((n_pool, PAGE_SIZE, N_ODD, DSA_KV, HEAD_DIM)),
        'cache_dsa_idx_k': nrm((n_pool, PAGE_SIZE, N_ODD, IDX_DIM)),
        'page_table': page_table,
        'c_prompt': nrm((BATCH, D_MODEL)),
        'c_sample': nrm((DEC_BATCH, D_MODEL)),
        'bias_table': nrm((N_BUCKETS, N_HEADS), 0.2),
        'norm_gain': 1.0 + nrm((DEPTH, 3, D_MODEL), 0.1),
        'ada_w': nrm((DEPTH, D_MODEL, 9 * D_MODEL), D_MODEL ** -0.5),
        'ada_b': nrm((DEPTH, 9 * D_MODEL), 0.02),
        'ffn_w_in': nrm((DEPTH, 2, D_MODEL, 2 * D_FF), D_MODEL ** -0.5),
        'ffn_w_out': nrm((DEPTH, 2, D_FF, D_MODEL), D_FF ** -0.5),
        'even_w_in': nrm((N_EVEN, D_MODEL, EVEN_IN), D_MODEL ** -0.5),
        'even_w_out': nrm((N_EVEN, MOBA_W + NSA_W, D_MODEL), (MOBA_W + NSA_W) ** -0.5),
        'moba_qk_gain': 1.0 + nrm((N_EVEN, 2, HEAD_DIM), 0.1),
        'nsa_qk_gain': 1.0 + nrm((N_EVEN, 4, HEAD_DIM), 0.1),
        'nsa_cmp_pe': nrm((N_EVEN, 2, CMP_LEN, HEAD_DIM), 0.1),
        'nsa_cmp_w1': nrm((N_EVEN, 2, CMP_LEN * HEAD_DIM, CMP_HIDDEN), (CMP_LEN * HEAD_DIM) ** -0.5),
        'nsa_cmp_w2': nrm((N_EVEN, 2, CMP_HIDDEN, HEAD_DIM), CMP_HIDDEN ** -0.5),
        'odd_w_in': nrm((N_ODD, D_MODEL, ODD_IN), D_MODEL ** -0.5),
        'odd_w_out': nrm((N_ODD, DSA_W, D_MODEL), DSA_W ** -0.5),
        'dsa_qk_gain': 1.0 + nrm((N_ODD, 2, HEAD_DIM), 0.1),
    }


def reference(x_prompt, x_sample, cache_moba_k, cache_moba_v, cache_nsa_cmp_k, cache_nsa_cmp_v,
              cache_nsa_slc_k, cache_nsa_slc_v, state_nsa_win_k, state_nsa_win_v, cache_dsa_k, cache_dsa_v,
              cache_dsa_idx_k, page_table, c_prompt, c_sample, bias_table, norm_gain, ada_w, ada_b,
              ffn_w_in, ffn_w_out, even_w_in, even_w_out, moba_qk_gain, nsa_qk_gain, nsa_cmp_pe,
              nsa_cmp_w1, nsa_cmp_w2, odd_w_in, odd_w_out, dsa_qk_gain):
    tab_moba = bias_table[:, :MOBA_HEADS].T
    tab_nsa = bias_table[:, MOBA_HEADS:].T.reshape(NSA_KV, NSA_GROUP, N_BUCKETS)
    tab_dsa = bias_table.T.reshape(DSA_KV, DSA_GROUP, N_BUCKETS)
    even_caches = (cache_moba_k, cache_moba_v, cache_nsa_cmp_k, cache_nsa_cmp_v, cache_nsa_slc_k, cache_nsa_slc_v)
    odd_caches = (cache_dsa_k, cache_dsa_v, cache_dsa_idx_k)
    rows_ep = [[] for _ in range(8)]
    rows_es = [[] for _ in range(8)]
    rows_op = [[] for _ in range(3)]
    rows_os = [[] for _ in range(3)]
    yp, ys = x_prompt, x_sample
    for li in range(DEPTH):
        mp = adaln(c_prompt, ada_w[li], ada_b[li])
        ms = adaln(c_sample, ada_w[li], ada_b[li])
        yp = ffn_half(yp, mp, 0, norm_gain[li, 0], ffn_w_in[li, 0], ffn_w_out[li, 0])
        ys = ffn_half(ys, ms, 0, norm_gain[li, 0], ffn_w_in[li, 0], ffn_w_out[li, 0])
        hp = modulated_norm(yp, norm_gain[li, 1], mp, 1)
        hs = modulated_norm(ys, norm_gain[li, 1], ms, 1)
        if li % 2 == 0:
            e = li // 2
            prm = (even_w_in[e], even_w_out[e], moba_qk_gain[e], nsa_qk_gain[e], nsa_cmp_pe[e],
                   nsa_cmp_w1[e], nsa_cmp_w2[e], tab_moba, tab_nsa)
            op, new_p = even_mixer_prompt(hp, *prm)
            past = [gather_past(c, page_table, e) for c in even_caches]
            os_, new_s = even_mixer_sample(hs, past, state_nsa_win_k[:, :, e], state_nsa_win_v[:, :, e], *prm)
            for acc, r in zip(rows_ep, new_p):
                acc.append(r)
            for acc, r in zip(rows_es, new_s):
                acc.append(r)
        else:
            o = li // 2
            prm = (odd_w_in[o], odd_w_out[o], dsa_qk_gain[o], tab_dsa)
            op, new_p = odd_mixer_prompt(hp, *prm)
            past = [gather_past(c, page_table, o) for c in odd_caches]
            os_, new_s = odd_mixer_sample(hs, past, *prm)
            for acc, r in zip(rows_op, new_p):
                acc.append(r)
            for acc, r in zip(rows_os, new_s):
                acc.append(r)
        yp = yp + mp[:, 1, 2][:, None, :] * op
        ys = ys + ms[:, 1, 2][:, None, :] * os_
        yp = ffn_half(yp, mp, 2, norm_gain[li, 2], ffn_w_in[li, 1], ffn_w_out[li, 1])
        ys = ffn_half(ys, ms, 2, norm_gain[li, 2], ffn_w_in[li, 1], ffn_w_out[li, 1])
    moba_k_p, moba_v_p, cmp_k_p, cmp_v_p, slc_k_p, slc_v_p, win_k_p, win_v_p = [jnp.stack(r, axis=2) for r in rows_ep]
    moba_k_s, moba_v_s, cmp_k_s, cmp_v_s, slc_k_s, slc_v_s, win_k_s, win_v_s = [jnp.stack(r, axis=2) for r in rows_es]
    dsa_k_p, dsa_v_p, dsa_idx_k_p = [jnp.stack(r, axis=2) for r in rows_op]
    dsa_k_s, dsa_v_s, dsa_idx_k_s = [jnp.stack(r, axis=2) for r in rows_os]
    return (yp, ys, moba_k_p, moba_k_s, moba_v_p, moba_v_s, cmp_k_p, cmp_k_s, cmp_v_p, cmp_v_s,
            slc_k_p, slc_k_s, slc_v_p, slc_v_s, win_k_p, win_k_s, win_v_p, win_v_s,
            dsa_k_p, dsa_k_s, dsa_v_p, dsa_v_s, dsa_idx_k_p, dsa_idx_k_s)
```

```python
import functools
import math

import jax
import jax.numpy as jnp
import numpy as np
from jax import lax
from jax.experimental import pallas as pl
from jax.experimental.pallas import tpu as pltpu

D_MODEL = 1024
DEPTH = 4
PAST_LEN = 2048
PAGE_SIZE = 128
HEAD_DIM = 64
N_HEADS = 16
MOBA_HEADS = 4
NSA_HEADS = 12
NSA_KV = 2
NSA_GROUP = 6
DSA_HEADS = 16
DSA_KV = 4
DSA_GROUP = 4
IDX_HEADS = 8
IDX_DIM = 64
MOBA_BLOCK = 256
MOBA_TOPK = 3
CMP_LEN = 32
CMP_STRIDE = 16
CMP_HIDDEN = 128
SEL_BLOCK = 64
N_SEL = 8
WINDOW = 512
DSA_TOPK = 256
N_BUCKETS = 32
MAX_DISTANCE = 128
D_FF = 2816
N_EVEN = 2
N_ODD = 2
MOBA_W = MOBA_HEADS * HEAD_DIM
NSA_W = NSA_HEADS * HEAD_DIM
NSA_KW = NSA_KV * HEAD_DIM
DSA_W = DSA_HEADS * HEAD_DIM
DSA_KW = DSA_KV * HEAD_DIM
RMS_EPS = 1e-6
NEG_INF = -1e30
TINY = 1e-30
SEL_FORCE = 1e4

LANES = 128
VMEM_LIMIT = 56 * 1024 * 1024
BF16 = jnp.bfloat16
F32 = jnp.float32


def _cparams(*sem):
    return pltpu.CompilerParams(dimension_semantics=sem, vmem_limit_bytes=VMEM_LIMIT)


def _t5_bucket(dist):
    n = jnp.maximum(dist, 0)
    exact = N_BUCKETS // 2
    nf = jnp.maximum(n, 1).astype(F32)
    large = exact + (jnp.log(nf / exact) / math.log(MAX_DISTANCE / exact) * (N_BUCKETS - exact)).astype(jnp.int32)
    return jnp.where(n < exact, n, jnp.minimum(large, N_BUCKETS - 1))


def _adaln_kernel(c_ref, w_ref, b_ref, o_ref):
    c = c_ref[...]
    s = (c * jax.nn.sigmoid(c)).astype(BF16)
    o_ref[0] = jnp.dot(s, w_ref[0].astype(BF16), preferred_element_type=F32) + b_ref[0]


def adaln_all(c_all, ada_w, ada_b):
    r = c_all.shape[0]
    n_out = ada_w.shape[2]
    tn = 1024
    return pl.pallas_call(
        _adaln_kernel,
        grid=(DEPTH, n_out // tn),
        in_specs=[
            pl.BlockSpec((r, D_MODEL), lambda l, j: (0, 0)),
            pl.BlockSpec((1, D_MODEL, tn), lambda l, j: (l, 0, j)),
            pl.BlockSpec((1, 1, tn), lambda l, j: (l, 0, j)),
        ],
        out_specs=pl.BlockSpec((1, r, tn), lambda l, j: (l, 0, j)),
        out_shape=jax.ShapeDtypeStruct((DEPTH, r, n_out), F32),
        compiler_params=_cparams("parallel", "parallel"),
    )(c_all, ada_w, ada_b.reshape(DEPTH, 1, n_out))


def _modnorm(x, g, scale, shift):
    y = x * lax.rsqrt(jnp.mean(x * x, axis=-1, keepdims=True) + RMS_EPS)
    return (y * g) * (1.0 + scale) + shift


def _mod_spec(tm_rows, t_mod):
    if t_mod == 1:
        return pl.BlockSpec((1, 1, D_MODEL), lambda b, i, *_: (b, 0, 0))
    return pl.BlockSpec((1, tm_rows, D_MODEL), lambda b, i, *_: (b, i, 0))


def _ffn_kernel(x_ref, sh_ref, sc_ref, gt_ref, g_ref, wa_ref, wg_ref, wo_ref, o_ref, xn_ref, acc_ref):
    j = pl.program_id(2)

    @pl.when(j == 0)
    def _():
        xn_ref[...] = _modnorm(x_ref[0], g_ref[...], sc_ref[0], sh_ref[0]).astype(BF16)
        acc_ref[...] = jnp.zeros_like(acc_ref)

    xn = xn_ref[...]
    a = jnp.dot(xn, wa_ref[...], preferred_element_type=F32)
    g = jnp.dot(xn, wg_ref[...], preferred_element_type=F32)
    h = ((g * jax.nn.sigmoid(g)) * a).astype(BF16)
    acc_ref[...] += jnp.dot(h, wo_ref[...], preferred_element_type=F32)

    @pl.when(j == pl.num_programs(2) - 1)
    def _():
        o_ref[0] = x_ref[0] + (0.5 * gt_ref[0]) * acc_ref[...]


def ffn_half(x, shift, scale, gate, g, w_in, w_out):
    b, t, _ = x.shape
    t_mod = shift.shape[1]
    tm = min(t, 512)
    fc = 1408
    nf = D_FF // fc
    mspec = _mod_spec(tm, t_mod)
    return pl.pallas_call(
        _ffn_kernel,
        grid=(b, t // tm, nf),
        in_specs=[
            pl.BlockSpec((1, tm, D_MODEL), lambda bb, i, j: (bb, i, 0)),
            mspec, mspec, mspec,
            pl.BlockSpec((1, D_MODEL), lambda bb, i, j: (0, 0)),
            pl.BlockSpec((D_MODEL, fc), lambda bb, i, j: (0, j)),
            pl.BlockSpec((D_MODEL, fc), lambda bb, i, j: (0, j + nf)),
            pl.BlockSpec((fc, D_MODEL), lambda bb, i, j: (j, 0)),
        ],
        out_specs=pl.BlockSpec((1, tm, D_MODEL), lambda bb, i, j: (bb, i, 0)),
        out_shape=jax.ShapeDtypeStruct(x.shape, F32),
        scratch_shapes=[pltpu.VMEM((tm, D_MODEL), BF16), pltpu.VMEM((tm, D_MODEL), F32)],
        compiler_params=_cparams("parallel", "parallel", "arbitrary"),
    )(x, shift, scale, gate, g.reshape(1, D_MODEL), w_in, w_in, w_out)


def _head_rms(z, seg_ref, gain):
    ss = jnp.dot(z * z, seg_ref[...], preferred_element_type=F32, precision=lax.Precision.HIGHEST)
    return z * lax.rsqrt(ss * (1.0 / HEAD_DIM) + RMS_EPS) * gain


def _proj_kernel(plan, x_ref, sh_ref, sc_ref, g_ref, w_ref, gain_ref, seg_ref, *o_refs):
    h = _modnorm(x_ref[0], g_ref[...], sc_ref[0], sh_ref[0])
    z = jnp.dot(h.astype(BF16), w_ref[...], preferred_element_type=F32)
    for o_ref, (start, width, mode) in zip(o_refs, plan):
        for c in range(width // LANES):
            lo = start + c * LANES
            zc = z[:, lo:lo + LANES]
            if mode == "norm":
                zc = _head_rms(zc, seg_ref, gain_ref[:, lo:lo + LANES])
            elif mode == "sigmoid":
                zc = jax.nn.sigmoid(zc)
            o_ref[0, :, c * LANES:(c + 1) * LANES] = zc


def mixer_project(x, shift, scale, g, w, gain_cols, plan):
    b, t, _ = x.shape
    t_mod = shift.shape[1]
    tm = min(t, 512)
    wp = w.shape[1]
    mspec = _mod_spec(tm, t_mod)
    seg = jnp.kron(jnp.eye(LANES // HEAD_DIM, dtype=F32), jnp.ones((HEAD_DIM, HEAD_DIM), F32))
    out_shapes = [jax.ShapeDtypeStruct((b, t, width), F32) for _, width, _ in plan]
    out_specs = [pl.BlockSpec((1, tm, width), lambda bb, i: (bb, i, 0)) for _, width, _ in plan]
    return pl.pallas_call(
        functools.partial(_proj_kernel, plan),
        grid=(b, t // tm),
        in_specs=[
            pl.BlockSpec((1, tm, D_MODEL), lambda bb, i: (bb, i, 0)),
            mspec, mspec,
            pl.BlockSpec((1, D_MODEL), lambda bb, i: (0, 0)),
            pl.BlockSpec((D_MODEL, wp), lambda bb, i: (0, 0)),
            pl.BlockSpec((1, wp), lambda bb, i: (0, 0)),
            pl.BlockSpec((LANES, LANES), lambda bb, i: (0, 0)),
        ],
        out_specs=out_specs,
        out_shape=out_shapes,
        compiler_params=_cparams("parallel", "parallel"),
    )(x, shift, scale, g.reshape(1, D_MODEL), w, gain_cols, seg)


def _outproj_kernel(y_ref, o_ref, gt_ref, w_ref, out_ref):
    z = jnp.dot(o_ref[0].astype(BF16), w_ref[...], preferred_element_type=F32)
    out_ref[0] = y_ref[0] + gt_ref[0] * z


def mixer_merge(y, o, gate, w_out):
    b, t, _ = y.shape
    tm = min(t, 512)
    return pl.pallas_call(
        _outproj_kernel,
        grid=(b, t // tm),
        in_specs=[
            pl.BlockSpec((1, tm, D_MODEL), lambda bb, i: (bb, i, 0)),
            pl.BlockSpec((1, tm, D_MODEL), lambda bb, i: (bb, i, 0)),
            _mod_spec(tm, gate.shape[1]),
            pl.BlockSpec((D_MODEL, D_MODEL), lambda bb, i: (0, 0)),
        ],
        out_specs=pl.BlockSpec((1, tm, D_MODEL), lambda bb, i: (bb, i, 0)),
        out_shape=jax.ShapeDtypeStruct(y.shape, F32),
        compiler_params=_cparams("parallel", "parallel"),
    )(y, o, gate, w_out)


TQ = 128
NT_DIMS = (((1,), (1,)), ((), ()))
M_INIT = -3e38


def _bias_tiles(bvec, t):
    r = jnp.arange(t)[:, None]
    c = jnp.arange(t)[None, :]
    d = r - c
    diag = jnp.where(d >= 0, bvec[:, jnp.maximum(d, 0)], NEG_INF)
    adj = bvec[:, t + d]
    far = jnp.broadcast_to(bvec[:, 2 * t][:, None, None], adj.shape)
    return jnp.stack([diag, adj, far], axis=1)


def _flash(qb, k_tile, v_tile, add_tile, j_lo, j_hi, rows):
    scale = HEAD_DIM ** -0.5

    def body(j, carry):
        m, l, acc = carry
        s = lax.dot_general(qb, k_tile(j), NT_DIMS, preferred_element_type=F32) * scale + add_tile(j)
        m_new = jnp.maximum(m, jnp.max(s, axis=1, keepdims=True))
        alpha = jnp.exp(m - m_new)
        p = jnp.exp(s - m_new)
        l = alpha * l + jnp.sum(p, axis=1, keepdims=True)
        acc = alpha * acc + jnp.dot(p.astype(BF16), v_tile(j), preferred_element_type=F32)
        return m_new, l, acc

    init = (jnp.full((rows, 1), M_INIT, F32), jnp.zeros((rows, 1), F32), jnp.zeros((rows, HEAD_DIM), F32))
    _, l, acc = lax.fori_loop(j_lo, j_hi, body, init)
    return acc / jnp.maximum(l, TINY)


def _top_k_rows(score, allowed, k):
    lane = lax.broadcasted_iota(jnp.int32, score.shape, 1)
    n = score.shape[1]
    remaining = allowed
    picked = jnp.zeros(score.shape, jnp.bool_)
    for _ in range(k):
        cur = jnp.where(remaining, score, -jnp.inf)
        best = jnp.max(cur, axis=1, keepdims=True)
        cand = jnp.logical_and(remaining, cur == best)
        first = jnp.min(jnp.where(cand, lane, n), axis=1, keepdims=True)
        hit = lane == first
        picked = jnp.logical_or(picked, hit)
        remaining = jnp.logical_and(remaining, jnp.logical_not(hit))
    return picked


def _moba_kernel(q_ref, k_ref, v_ref, kb_ref, bias_ref, o_ref, km_ref):
    i = pl.program_id(2)
    nb = km_ref.shape[0]
    t = MOBA_BLOCK

    @pl.when(i == 0)
    def _():
        km_ref[...] = jnp.mean(k_ref[0, 0].reshape(nb, t, HEAD_DIM), axis=1)

    q = q_ref[0, 0]
    gate = lax.dot_general(q, km_ref[...], NT_DIMS, preferred_element_type=F32, precision=lax.Precision.HIGHEST)
    blk = lax.broadcasted_iota(jnp.int32, gate.shape, 1)
    chosen = jnp.logical_or(_top_k_rows(gate, blk < i, MOBA_TOPK), blk == i)
    blk_add = jnp.where(chosen, 0.0, NEG_INF)
    qb = q.astype(BF16)

    def k_tile(j):
        return kb_ref[0, 0, pl.ds(pl.multiple_of(j * t, t), t), :]

    def v_tile(j):
        return v_ref[0, 0, pl.ds(pl.multiple_of(j * t, t), t), :]

    def add_tile(j):
        col = jnp.sum(jnp.where(blk == j, blk_add, 0.0), axis=1, keepdims=True)
        return bias_ref[0, jnp.minimum(i - j, 2)] + col

    o_ref[0, 0] = _flash(qb, k_tile, v_tile, add_tile, 0, i + 1, t)


def moba_prompt(q, k, kb, vb, bias):
    b, h, t, _ = q.shape
    tb = MOBA_BLOCK
    nb = t // tb
    full = lambda bb, hh, i: (bb, hh, 0, 0)
    return pl.pallas_call(
        _moba_kernel,
        grid=(b, h, nb),
        in_specs=[
            pl.BlockSpec((1, 1, tb, HEAD_DIM), lambda bb, hh, i: (bb, hh, i, 0)),
            pl.BlockSpec((1, 1, t, HEAD_DIM), full),
            pl.BlockSpec((1, 1, t, HEAD_DIM), full),
            pl.BlockSpec((1, 1, t, HEAD_DIM), full),
            pl.BlockSpec((1, 3, tb, tb), lambda bb, hh, i: (hh, 0, 0, 0)),
        ],
        out_specs=pl.BlockSpec((1, 1, tb, HEAD_DIM), lambda bb, hh, i: (bb, hh, i, 0)),
        out_shape=jax.ShapeDtypeStruct(q.shape, F32),
        scratch_shapes=[pltpu.VMEM((nb, HEAD_DIM), F32)],
        compiler_params=_cparams("parallel", "parallel", "arbitrary"),
    )(q, k, vb, kb, bias)


N_CHUNK = PAST_LEN // CMP_STRIDE
CHUNK_W = CMP_STRIDE * NSA_KW
N_CMP = (PAST_LEN - CMP_LEN) // CMP_STRIDE + 1


def _compress_weights(pe, w1, w2):
    eye = jnp.eye(NSA_KV, dtype=F32)
    pe_x = jnp.broadcast_to(pe.reshape(2, 2, CMP_STRIDE, 1, HEAD_DIM),
                            (2, 2, CMP_STRIDE, NSA_KV, HEAD_DIM)).reshape(2, 2, CHUNK_W)
    w1_h = w1.reshape(2, 2, CMP_STRIDE, HEAD_DIM, CMP_HIDDEN)
    w1_x = jnp.einsum("thrdj,gq->trgdhqj", w1_h, eye).reshape(2, CHUNK_W, 2 * NSA_KV * CMP_HIDDEN)
    w2_x = jnp.einsum("tjd,gq->tgjqd", w2, eye).reshape(2, NSA_KV * CMP_HIDDEN, NSA_KW)
    return pe_x, w1_x.astype(BF16), w2_x.astype(BF16)


def _compress_one(r, pe_ref, w1_ref, w2_ref, t):
    half_w = NSA_KV * CMP_HIDDEN
    ha = jnp.dot((r + pe_ref[t, 0:1]).astype(BF16), w1_ref[t, :, :half_w], preferred_element_type=F32)
    hb = jnp.dot((r + pe_ref[t, 1:2]).astype(BF16), w1_ref[t, :, half_w:], preferred_element_type=F32)
    hid = jax.nn.gelu(ha + pltpu.roll(hb, N_CHUNK - 1, 0))
    return jnp.dot(hid.astype(BF16), w2_ref[t], preferred_element_type=F32)


def _compress_kernel(rk_ref, rv_ref, pe_ref, w1_ref, w2_ref, gain_ref, seg_ref, ck_ref, cv_ref):
    ck_ref[0] = _head_rms(_compress_one(rk_ref[0], pe_ref, w1_ref, w2_ref, 0), seg_ref, gain_ref[...])
    cv_ref[0] = _compress_one(rv_ref[0], pe_ref, w1_ref, w2_ref, 1)


def nsa_compress_prompt(ck_raw, cv_raw, pe_x, w1_x, w2_x, gain):
    b = ck_raw.shape[0]
    seg = jnp.kron(jnp.eye(LANES // HEAD_DIM, dtype=F32), jnp.ones((HEAD_DIM, HEAD_DIM), F32))
    rows = pl.BlockSpec((1, N_CHUNK, CHUNK_W), lambda bb: (bb, 0, 0))
    out = pl.BlockSpec((1, N_CHUNK, NSA_KW), lambda bb: (bb, 0, 0))
    const = lambda *shape: pl.BlockSpec(shape, lambda bb: (0,) * len(shape))
    return pl.pallas_call(
        _compress_kernel,
        grid=(b,),
        in_specs=[rows, rows, const(2, 2, CHUNK_W), const(2, CHUNK_W, 4 * CMP_HIDDEN),
                  const(2, 2 * CMP_HIDDEN, NSA_KW), const(1, NSA_KW), const(LANES, LANES)],
        out_specs=[out, out],
        out_shape=[jax.ShapeDtypeStruct((b, N_CHUNK, NSA_KW), F32)] * 2,
        compiler_params=_cparams("parallel"),
    )(ck_raw.reshape(b, N_CHUNK, CHUNK_W), cv_raw.reshape(b, N_CHUNK, CHUNK_W), pe_x, w1_x, w2_x,
      jnp.tile(gain, NSA_KV).reshape(1, NSA_KW), seg)


N_SELB = PAST_LEN // SEL_BLOCK


def _nsa_kernel(q_ref, gt_ref, ck_ref, cv_ref, sk_ref, sv_ref, wk_ref, wv_ref, bias_ref, cb_ref, ov_ref, ex_ref,
                o_ref, madd_ref):
    i = pl.program_id(2)
    scale = HEAD_DIM ** -0.5
    row = lax.broadcasted_iota(jnp.int32, (TQ, LANES), 0)
    lane = lax.broadcasted_iota(jnp.int32, (TQ, LANES), 1)
    qpos = i * TQ + row

    cmask = jnp.logical_and(qpos - (lane * CMP_STRIDE + CMP_LEN - 1) >= 0, lane < N_CMP)
    ck = ck_ref[0, 0]
    cvb = cv_ref[0, 0].astype(BF16)
    psum = jnp.zeros((TQ, LANES), F32)
    o_cmp = []
    for h in range(NSA_GROUP):
        lc = lax.dot_general(q_ref[0, 0, h], ck, NT_DIMS, preferred_element_type=F32,
                             precision=lax.Precision.HIGHEST) * scale + cb_ref[h]
        lc = jnp.where(cmask, lc, NEG_INF)
        e = jnp.where(cmask, jnp.exp(lc - jnp.max(lc, axis=1, keepdims=True)), 0.0)
        pc = e / jnp.maximum(jnp.sum(e, axis=1, keepdims=True), TINY)
        psum = psum + pc
        o_cmp.append(jnp.dot(pc.astype(BF16), cvb, preferred_element_type=F32))

    imp = jnp.dot(psum, ov_ref[...], preferred_element_type=F32, precision=lax.Precision.HIGHEST)
    cur = qpos // SEL_BLOCK
    forced = jnp.logical_or(lane == 0, jnp.logical_or(lane == cur, lane == cur - 1))
    picked = _top_k_rows(jnp.where(forced, SEL_FORCE, imp), lane <= cur, N_SEL)
    pick_b = jnp.where(picked, 1.0, 0.0).astype(BF16)

    def fill(j, _):
        hit = jnp.dot(pick_b, ex_ref[:, pl.ds(pl.multiple_of(j * TQ, TQ), TQ)], preferred_element_type=F32)
        madd_ref[j] = jnp.where(hit > 0.5, 0.0, NEG_INF)
        return 0

    lax.fori_loop(0, i + 1, fill, 0)

    def tile_of(ref):
        return lambda j: ref[0, 0, pl.ds(pl.multiple_of(j * TQ, TQ), TQ), :]

    n_win = WINDOW // TQ
    edge = jnp.where(row >= lane, NEG_INF, 0.0)
    for h in range(NSA_GROUP):
        qb = q_ref[0, 0, h].astype(BF16)

        def add_slc(j):
            return bias_ref[h, jnp.minimum(i - j, 2)] + madd_ref[j]

        def add_win(j):
            return bias_ref[h, jnp.minimum(i - j, 2)] + jnp.where(i - j == n_win, edge, 0.0)

        o_slc = _flash(qb, tile_of(sk_ref), tile_of(sv_ref), add_slc, 0, i + 1, TQ)
        o_win = _flash(qb, tile_of(wk_ref), tile_of(wv_ref), add_win, jnp.maximum(i - n_win, 0), i + 1, TQ)
        g3 = gt_ref[0, 0][:, 3 * h:3 * h + 3]
        o_ref[0, 0, h] = g3[:, 0:1] * o_cmp[h] + g3[:, 1:2] * o_slc + g3[:, 2:3] * o_win


def nsa_prompt(q, gates, ck, cv, sk, sv, wk, wv, bias, cbias):
    b, g, hg, t, _ = q.shape
    nq = t // TQ
    c = jnp.arange(LANES)[:, None]
    n = jnp.arange(LANES)[None, :]
    overlap = jnp.logical_and(c * CMP_STRIDE < n * SEL_BLOCK + SEL_BLOCK,
                              c * CMP_STRIDE + CMP_LEN > n * SEL_BLOCK).astype(F32)
    expand = (jnp.arange(t)[None, :] // SEL_BLOCK == jnp.arange(LANES)[:, None]).astype(BF16)
    kv = pl.BlockSpec((1, 1, t, HEAD_DIM), lambda bb, gg, i: (bb, gg, 0, 0))
    cmp_spec = pl.BlockSpec((1, 1, LANES, HEAD_DIM), lambda bb, gg, i: (bb, gg, 0, 0))
    return pl.pallas_call(
        _nsa_kernel,
        grid=(b, g, nq),
        in_specs=[
            pl.BlockSpec((1, 1, hg, TQ, HEAD_DIM), lambda bb, gg, i: (bb, gg, 0, i, 0)),
            pl.BlockSpec((1, 1, TQ, 3 * hg), lambda bb, gg, i: (bb, gg, i, 0)),
            cmp_spec, cmp_spec, kv, kv, kv, kv,
            pl.BlockSpec((hg, 3, TQ, TQ), lambda bb, gg, i: (gg, 0, 0, 0)),
            pl.BlockSpec((hg, TQ, LANES), lambda bb, gg, i: (gg, i, 0)),
            pl.BlockSpec((LANES, LANES), lambda bb, gg, i: (0, 0)),
            pl.BlockSpec((LANES, t), lambda bb, gg, i: (0, 0)),
        ],
        out_specs=pl.BlockSpec((1, 1, hg, TQ, HEAD_DIM), lambda bb, gg, i: (bb, gg, 0, i, 0)),
        out_shape=jax.ShapeDtypeStruct(q.shape, F32),
        scratch_shapes=[pltpu.VMEM((nq, TQ, TQ), F32)],
        compiler_params=_cparams("parallel", "parallel", "arbitrary"),
    )(q, gates, ck, cv, sk, sv, wk, wv, bias, cbias, overlap, expand)


INT_MIN = -2 ** 31


def _count(mask):
    return jnp.sum(jnp.where(mask, 1.0, 0.0), axis=1, keepdims=True)


def _top_k_mask(score, valid, pos, k):
    bits = pltpu.bitcast(score, jnp.int32)
    key = jnp.where(bits < 0, bits ^ jnp.int32(0x7FFFFFFF), bits)
    key = jnp.where(valid, key, jnp.int32(INT_MIN))
    kf = float(k)

    thr = jnp.full((score.shape[0], 1), INT_MIN, jnp.int32)
    cand = thr ^ jnp.int32(INT_MIN)
    thr = jnp.where(_count(key >= cand) >= kf, cand, thr)

    def step(b, thr):
        cand = thr | lax.shift_left(jnp.int32(1), 30 - b)
        return jnp.where(_count(key >= cand) >= kf, cand, thr)

    thr = lax.fori_loop(0, 31, step, thr)
    above = key > thr
    tie = key == thr
    need = kf - _count(above)

    n_bits = max(int(score.shape[1] - 1).bit_length(), 1)

    def tie_step(b, cut):
        cand = cut | lax.shift_left(jnp.int32(1), n_bits - 1 - b)
        return jnp.where(_count(jnp.logical_and(tie, pos < cand)) < need, cand, cut)

    cut = lax.fori_loop(0, n_bits, tie_step, jnp.zeros((score.shape[0], 1), jnp.int32))
    keep = jnp.logical_or(above, jnp.logical_and(tie, pos <= cut))
    return jnp.logical_and(keep, valid)


def _dsa_index_kernel(qi_ref, wi_ref, ki_ref, o_ref):
    i = pl.program_id(1)
    t = ki_ref.shape[1]
    ki = ki_ref[0]
    wi = wi_ref[0]
    score = jnp.zeros((TQ, t), F32)
    for h in range(IDX_HEADS):
        s = lax.dot_general(qi_ref[0, h], ki, NT_DIMS, preferred_element_type=F32,
                            precision=lax.Precision.HIGHEST) * IDX_DIM ** -0.5
        score = score + wi[:, h:h + 1] * jnp.maximum(s, 0.0)
    score = score * IDX_HEADS ** -0.5
    kpos = lax.broadcasted_iota(jnp.int32, (TQ, t), 1)
    qpos = i * TQ + lax.broadcasted_iota(jnp.int32, (TQ, t), 0)
    keep = _top_k_mask(score, kpos <= qpos, kpos, min(DSA_TOPK, t // 4))
    add = jnp.where(keep, 0.0, NEG_INF).astype(BF16)
    for j in range(t // TQ):
        o_ref[0, 0, j] = add[:, j * TQ:(j + 1) * TQ]


def dsa_index_prompt(qi, wi, ki):
    b, ih, t, _ = qi.shape
    nq = t // TQ
    return pl.pallas_call(
        _dsa_index_kernel,
        grid=(b, nq),
        in_specs=[
            pl.BlockSpec((1, ih, TQ, IDX_DIM), lambda bb, i: (bb, 0, i, 0)),
            pl.BlockSpec((1, TQ, LANES), lambda bb, i: (bb, i, 0)),
            pl.BlockSpec((1, t, IDX_DIM), lambda bb, i: (bb, 0, 0)),
        ],
        out_specs=pl.BlockSpec((1, 1, nq, TQ, TQ), lambda bb, i: (bb, i, 0, 0, 0)),
        out_shape=jax.ShapeDtypeStruct((b, nq, nq, TQ, TQ), BF16),
        compiler_params=_cparams("parallel", "parallel"),
    )(qi, wi, ki)


def _dsa_attn_kernel(q_ref, k_ref, v_ref, bias_ref, m_ref, o_ref):
    i = pl.program_id(2)

    def tile_of(ref):
        return lambda j: ref[0, 0, pl.ds(pl.multiple_of(j * TQ, TQ), TQ), :]

    for h in range(DSA_GROUP):

        def add(j):
            return bias_ref[h, jnp.minimum(i - j, 2)] + m_ref[0, 0, j].astype(F32)

        o_ref[0, 0, h] = _flash(q_ref[0, 0, h], tile_of(k_ref), tile_of(v_ref), add, 0, i + 1, TQ)


def dsa_attn_prompt(q, k, v, bias, mask):
    b, g, hg, t, _ = q.shape
    nq = t // TQ
    kv = pl.BlockSpec((1, 1, t, HEAD_DIM), lambda bb, gg, i: (bb, gg, 0, 0))
    return pl.pallas_call(
        _dsa_attn_kernel,
        grid=(b, g, nq),
        in_specs=[
            pl.BlockSpec((1, 1, hg, TQ, HEAD_DIM), lambda bb, gg, i: (bb, gg, 0, i, 0)),
            kv, kv,
            pl.BlockSpec((hg, 3, TQ, TQ), lambda bb, gg, i: (gg, 0, 0, 0)),
            pl.BlockSpec((1, 1, nq, TQ, TQ), lambda bb, gg, i: (bb, i, 0, 0, 0)),
        ],
        out_specs=pl.BlockSpec((1, 1, hg, TQ, HEAD_DIM), lambda bb, gg, i: (bb, gg, 0, i, 0)),
        out_shape=jax.ShapeDtypeStruct(q.shape, F32),
        compiler_params=_cparams("parallel", "parallel", "arbitrary"),
    )(q, k, v, bias, mask)


def _x_rmsnorm(x, g):
    xf = x.astype(F32)
    y = xf * lax.rsqrt(jnp.mean(xf * xf, axis=-1, keepdims=True) + RMS_EPS)
    return (y * g.astype(F32)).astype(x.dtype)


def _x_masked_softmax(logits, mask):
    logits = jnp.where(mask, logits.astype(F32), NEG_INF)
    m = jnp.max(logits, axis=-1, keepdims=True)
    e = jnp.where(mask, jnp.exp(logits - m), 0.0)
    return e / jnp.maximum(jnp.sum(e, axis=-1, keepdims=True), TINY)


def _x_gather_past(cache, page_table, layer):
    rows = cache[page_table, :, layer]
    return rows.reshape((rows.shape[0], rows.shape[1] * rows.shape[2]) + rows.shape[3:])


def _x_moba_blocks(k, v):
    B, L, H, dh = k.shape
    nb = -(-L // MOBA_BLOCK)
    pad = ((0, 0), (0, nb * MOBA_BLOCK - L), (0, 0), (0, 0))
    kb = jnp.pad(k, pad).reshape(B, nb, MOBA_BLOCK, H, dh)
    vb = jnp.pad(v, pad).reshape(B, nb, MOBA_BLOCK, H, dh)
    k_mean = jnp.mean(kb.astype(F32), axis=2)
    return kb.transpose(0, 3, 1, 2, 4), vb.transpose(0, 3, 1, 2, 4), k_mean


def _x_moba_core(q, q_pos, kt, vt, k_mean, tab):
    B, Tq, H, dh = q.shape
    nb = kt.shape[2]
    own = q_pos // MOBA_BLOCK
    gate = jnp.einsum('bthd,bnhd->bthn', q, k_mean).astype(F32)
    past = jnp.arange(nb)[None, :] < own[:, None]
    gate = jnp.where(past[None, :, None, :], gate, NEG_INF)
    kk = min(MOBA_TOPK, nb)
    _, sel = lax.top_k(gate, kk)
    own_b = jnp.broadcast_to(own[None, :, None, None], (B, Tq, H, 1)).astype(sel.dtype)
    blocks = jnp.concatenate([sel, own_b], axis=-1)
    slot_ok = jnp.concatenate([jnp.arange(kk)[None, :] < jnp.minimum(own, MOBA_TOPK)[:, None],
                               jnp.ones((Tq, 1), dtype=bool)], axis=-1)
    ns = kk + 1
    b_ix = jnp.arange(B)[:, None, None, None]
    h_ix = jnp.arange(H)[None, None, :, None]
    kg = kt[b_ix, h_ix, blocks]
    vg = vt[b_ix, h_ix, blocks]
    key_pos = blocks[..., None] * MOBA_BLOCK + jnp.arange(MOBA_BLOCK)
    dist = q_pos[None, :, None, None, None] - key_pos
    mask = slot_ok[None, :, None, :, None] & (dist >= 0)
    bias = tab[h_ix[..., None], _t5_bucket(dist)]
    logits = jnp.einsum('bthd,bthnsd->bthns', q, kg).astype(F32) * dh ** -0.5 + bias
    p = _x_masked_softmax(logits.reshape(B, Tq, H, ns * MOBA_BLOCK), mask.reshape(B, Tq, H, ns * MOBA_BLOCK))
    return jnp.einsum('bthk,bthkd->bthd', p, vg.reshape(B, Tq, H, ns * MOBA_BLOCK, dh))


def _x_nsa_compress(raw, pe, w1, w2):
    B, L, G, dh = raw.shape
    nc = (L - CMP_LEN) // CMP_STRIDE + 1
    idx = jnp.arange(nc)[:, None] * CMP_STRIDE + jnp.arange(CMP_LEN)[None, :]
    blk = raw[:, idx] + pe[None, None, :, None, :]
    flat = blk.transpose(0, 1, 3, 2, 4).reshape(B, nc, G, CMP_LEN * dh)
    return jax.nn.gelu(flat @ w1) @ w2


def _x_sel_blocks(x):
    B, L, G, dh = x.shape
    ns = -(-L // SEL_BLOCK)
    xp = jnp.pad(x, ((0, 0), (0, ns * SEL_BLOCK - L), (0, 0), (0, 0)))
    return xp.reshape(B, ns, SEL_BLOCK, G, dh).transpose(0, 3, 1, 2, 4)


def _x_nsa_core(q, gates, q_pos, ck, cv, skt, svt, kw, vw, w_pos, tab):
    B, Tq, G, HG, dh = q.shape
    scale = dh ** -0.5
    nc = ck.shape[1]
    c_start = jnp.arange(nc) * CMP_STRIDE
    c_end = c_start + CMP_LEN - 1
    c_dist = q_pos[:, None] - c_end[None, :]
    c_bias = jnp.moveaxis(tab[:, :, _t5_bucket(c_dist)], 2, 0)
    lc = jnp.einsum('btghd,bcgd->btghc', q, ck).astype(F32) * scale + c_bias[None]
    pc = _x_masked_softmax(lc, (c_dist >= 0)[None, :, None, None, :])
    o_cmp = jnp.einsum('btghc,bcgd->btghd', pc, cv)
    ns = skt.shape[2]
    s_start = jnp.arange(ns) * SEL_BLOCK
    overlap = ((c_start[:, None] < s_start[None, :] + SEL_BLOCK)
               & (c_start[:, None] + CMP_LEN > s_start[None, :])).astype(F32)
    imp = jnp.einsum('btgc,cn->btgn', jnp.sum(pc, axis=3), overlap)
    cur = q_pos // SEL_BLOCK
    j = jnp.arange(ns)[None, :]
    forced = (j == 0) | (j == cur[:, None]) | (j == cur[:, None] - 1)
    admissible = j <= cur[:, None]
    score = jnp.where(forced[None, :, None, :], SEL_FORCE,
                      jnp.where(admissible[None, :, None, :], imp, NEG_INF))
    ks = min(N_SEL, ns)
    _, sel = lax.top_k(score, ks)
    b_ix = jnp.arange(B)[:, None, None, None]
    g_ix = jnp.arange(G)[None, None, :, None]
    kg = skt[b_ix, g_ix, sel]
    vg = svt[b_ix, g_ix, sel]
    key_pos = sel[..., None] * SEL_BLOCK + jnp.arange(SEL_BLOCK)
    s_dist = q_pos[None, :, None, None, None] - key_pos
    s_bias = tab[jnp.arange(G)[:, None, None, None], jnp.arange(HG)[:, None, None],
                 _t5_bucket(s_dist)[:, :, :, None]]
    ls = jnp.einsum('btghd,btgksd->btghks', q, kg).astype(F32) * scale + s_bias
    ps = _x_masked_softmax(ls.reshape(B, Tq, G, HG, ks * SEL_BLOCK),
                           (s_dist >= 0).reshape(B, Tq, G, 1, ks * SEL_BLOCK))
    o_slc = jnp.einsum('btghk,btgkd->btghd', ps, vg.reshape(B, Tq, G, ks * SEL_BLOCK, dh))
    w_dist = q_pos[:, None] - w_pos[None, :]
    w_mask = (w_dist >= 0) & (w_dist < WINDOW) & (w_pos[None, :] >= 0)
    w_bias = jnp.moveaxis(tab[:, :, _t5_bucket(w_dist)], 2, 0)
    lw = jnp.einsum('btghd,bsgd->btghs', q, kw).astype(F32) * scale + w_bias[None]
    pw = _x_masked_softmax(lw, w_mask[None, :, None, None, :])
    o_win = jnp.einsum('btghs,bsgd->btghd', pw, vw)
    return gates[..., 0:1] * o_cmp + gates[..., 1:2] * o_slc + gates[..., 2:3] * o_win


def _x_dsa_core(q, qi, wi, q_pos, k, v, ki, kk, tab):
    B, Tq, G, HG, dh = q.shape
    L = k.shape[1]
    s = jnp.einsum('bthe,ble->bthl', qi, ki).astype(F32) * IDX_DIM ** -0.5
    score = jnp.einsum('bth,bthl->btl', wi.astype(F32), jax.nn.relu(s)) * IDX_HEADS ** -0.5
    score = jnp.where(jnp.arange(L)[None, None, :] <= q_pos[None, :, None], score, NEG_INF)
    _, sel = lax.top_k(score, kk)
    b_ix = jnp.arange(B)[:, None, None]
    kg = k[b_ix, sel]
    vg = v[b_ix, sel]
    dist = q_pos[None, :, None] - sel
    bias = tab[jnp.arange(G)[:, None, None], jnp.arange(HG)[:, None],
               _t5_bucket(dist)[:, :, None, None, :]]
    logits = jnp.einsum('btghd,btkgd->btghk', q, kg).astype(F32) * dh ** -0.5 + bias
    p = _x_masked_softmax(logits, (dist >= 0)[:, :, None, None, :])
    return jnp.einsum('btghk,btkgd->btghd', p, vg)


def _x_even_sample(proj, past, win_k_buf, win_v_buf, nsa_g, pe, w1, w2, tab_m, tab_n):
    mq, mk, mv, nq, ck_raw, cv_raw, sk, sv, wk, wv, gates = proj
    T = 1
    q_pos = PAST_LEN + jnp.arange(T)
    pk_m, pv_m, pk_c, pv_c, pk_s, pv_s = past
    cat = lambda a, b: jnp.concatenate([a, b], axis=1)
    kt, vt, km = _x_moba_blocks(cat(pk_m, mk), cat(pv_m, mv))
    o_m = _x_moba_core(mq, q_pos, kt, vt, km, tab_m)
    ck = _x_rmsnorm(_x_nsa_compress(cat(pk_c, ck_raw), pe[0], w1[0], w2[0]), nsa_g[1])
    cv = _x_nsa_compress(cat(pv_c, cv_raw), pe[1], w1[1], w2[1])
    skt, svt = _x_sel_blocks(cat(pk_s, sk)), _x_sel_blocks(cat(pv_s, sv))
    kw, vw = cat(win_k_buf, wk), cat(win_v_buf, wv)
    wb = win_k_buf.shape[1]
    w_pos = PAST_LEN - wb + jnp.arange(wb + T)
    o_n = _x_nsa_core(nq, gates, q_pos, ck, cv, skt, svt, kw, vw, w_pos, tab_n)
    return o_m, o_n


EVEN_PLAN = ((0, 256, "norm"), (256, 256, "norm"), (512, 256, "raw"), (768, 768, "norm"),
             (1536, 128, "raw"), (1664, 128, "raw"), (1792, 128, "norm"), (1920, 128, "raw"),
             (2048, 128, "norm"), (2176, 128, "raw"), (2304, 128, "sigmoid"))
EVEN_WP = 2432
ODD_PLAN = ((0, 1024, "norm"), (1024, 256, "norm"), (1280, 256, "raw"), (1536, 512, "raw"),
            (2048, 128, "raw"), (2176, 128, "raw"))
ODD_WP = 2304


def _even_weights(w_in, moba_g, nsa_g):
    w = jnp.pad(w_in, ((0, 0), (0, EVEN_WP - w_in.shape[1]))).astype(BF16)
    z = lambda n: jnp.zeros((n,), F32)
    gain = jnp.concatenate([jnp.tile(moba_g[0], 4), jnp.tile(moba_g[1], 4), z(256), jnp.tile(nsa_g[0], 12), z(256),
                            jnp.tile(nsa_g[2], 2), z(128), jnp.tile(nsa_g[3], 2), z(256)])
    return w, gain.reshape(1, EVEN_WP)


def _odd_weights(w_in, qk_g):
    zc = lambda n: jnp.zeros((D_MODEL, n), F32)
    w = jnp.concatenate([w_in[:, :2112], zc(64), w_in[:, 2112:], zc(ODD_WP - 2176 - IDX_HEADS)], axis=1).astype(BF16)
    gain = jnp.concatenate([jnp.tile(qk_g[0], 16), jnp.tile(qk_g[1], 4), jnp.zeros((ODD_WP - 1280,), F32)])
    return w, gain.reshape(1, ODD_WP)


def _heads(a, n):
    b, t, _ = a.shape
    return a.reshape(b, t, n, HEAD_DIM).transpose(0, 2, 1, 3)


def _unheads(a):
    b, n, t, d = a.shape
    return a.transpose(0, 2, 1, 3).reshape(b, t, n * d)


def _even_prompt(proj, cmp_w, nsa_gain, bias_m, bias_n, cbias):
    mq, mk, mv, nq, ck_raw, cv_raw, sk, sv, wk, wv, gl = proj
    b, t, _ = mq.shape
    o_m = moba_prompt(_heads(mq, MOBA_HEADS), _heads(mk, MOBA_HEADS), _heads(mk.astype(BF16), MOBA_HEADS),
                      _heads(mv.astype(BF16), MOBA_HEADS), bias_m)
    ck, cv = nsa_compress_prompt(ck_raw, cv_raw, *cmp_w, nsa_gain)
    grp = lambda a: _heads(a, NSA_KV)
    q5 = nq.reshape(b, t, NSA_KV, NSA_GROUP, HEAD_DIM).transpose(0, 2, 3, 1, 4)
    gates = gl[..., :3 * NSA_HEADS].reshape(b, t, NSA_KV, 3 * NSA_GROUP).transpose(0, 2, 1, 3)
    o_n = nsa_prompt(q5, gates, grp(ck), grp(cv), grp(sk.astype(BF16)), grp(sv.astype(BF16)),
                     grp(wk.astype(BF16)), grp(wv.astype(BF16)), bias_n, cbias)
    o_n = o_n.transpose(0, 3, 1, 2, 4).reshape(b, t, NSA_W)
    return jnp.concatenate([_unheads(o_m), o_n], axis=-1)


def _odd_prompt(proj, bias_d):
    q, k, v, qi, ki, wi = proj
    b, t, _ = q.shape
    mask = dsa_index_prompt(_heads(qi, IDX_HEADS), wi, ki[..., :IDX_DIM])
    q5 = q.astype(BF16).reshape(b, t, DSA_KV, DSA_GROUP, HEAD_DIM).transpose(0, 2, 3, 1, 4)
    o = dsa_attn_prompt(q5, _heads(k.astype(BF16), DSA_KV), _heads(v.astype(BF16), DSA_KV), bias_d, mask)
    return o.transpose(0, 3, 1, 2, 4).reshape(b, t, DSA_W)


def kernel(x_prompt, x_sample, cache_moba_k, cache_moba_v, cache_nsa_cmp_k, cache_nsa_cmp_v, cache_nsa_slc_k,
           cache_nsa_slc_v, state_nsa_win_k, state_nsa_win_v, cache_dsa_k, cache_dsa_v, cache_dsa_idx_k, page_table,
           c_prompt, c_sample, bias_table, norm_gain, ada_w, ada_b, ffn_w_in, ffn_w_out, even_w_in, even_w_out,
           moba_qk_gain, nsa_qk_gain, nsa_cmp_pe, nsa_cmp_w1, nsa_cmp_w2, odd_w_in, odd_w_out, dsa_qk_gain):
    bp, t, _ = x_prompt.shape
    bs = x_sample.shape[0]

    bvec = bias_table[_t5_bucket(jnp.arange(2 * PAST_LEN))].T
    bias_m = _bias_tiles(bvec[:MOBA_HEADS], MOBA_BLOCK)
    bias_n = _bias_tiles(bvec[MOBA_HEADS:], TQ)
    bias_d = _bias_tiles(bvec, TQ)
    c_dist = jnp.arange(t)[:, None] - (jnp.arange(LANES)[None, :] * CMP_STRIDE + CMP_LEN - 1)
    cbias = bvec[MOBA_HEADS:][:, jnp.maximum(c_dist, 0)]
    tab_moba = bias_table[:, :MOBA_HEADS].T
    tab_nsa = bias_table[:, MOBA_HEADS:].T.reshape(NSA_KV, NSA_GROUP, N_BUCKETS)
    tab_dsa = bias_table.T.reshape(DSA_KV, DSA_GROUP, N_BUCKETS)

    mods = adaln_all(jnp.concatenate([c_prompt, c_sample], axis=0), ada_w, ada_b)
    mods = mods.reshape(DEPTH, bp + bs, 3, 3, D_MODEL)
    ffn_in = ffn_w_in.astype(BF16)
    ffn_out = ffn_w_out.astype(BF16)

    yp = x_prompt
    ys = x_sample.reshape(1, bs, D_MODEL)
    rows_ep = [[] for _ in range(8)]
    rows_es = [[] for _ in range(8)]
    rows_op = [[] for _ in range(3)]
    rows_os = [[] for _ in range(3)]
    for li in range(DEPTH):
        mp = mods[li, :bp][:, None]
        ms = mods[li, bp:][None]
        mod = lambda m, s, k: m[:, :, s, k]
        for s, w_idx in ((0, 0),):
            yp = ffn_half(yp, mod(mp, s, 0), mod(mp, s, 1), mod(mp, s, 2), norm_gain[li, s], ffn_in[li, w_idx],
                          ffn_out[li, w_idx])
            ys = ffn_half(ys, mod(ms, s, 0), mod(ms, s, 1), mod(ms, s, 2), norm_gain[li, s], ffn_in[li, w_idx],
                          ffn_out[li, w_idx])
        if li % 2 == 0:
            e = li // 2
            w_p, gain_cols = _even_weights(even_w_in[e], moba_qk_gain[e], nsa_qk_gain[e])
            w_o = even_w_out[e].astype(BF16)
            cmp_w = _compress_weights(nsa_cmp_pe[e], nsa_cmp_w1[e], nsa_cmp_w2[e])
            proj_p = mixer_project(yp, mod(mp, 1, 0), mod(mp, 1, 1), norm_gain[li, 1], w_p, gain_cols, EVEN_PLAN)
            proj_s = mixer_project(ys, mod(ms, 1, 0), mod(ms, 1, 1), norm_gain[li, 1], w_p, gain_cols, EVEN_PLAN)
            op = _even_prompt(proj_p, cmp_w, nsa_qk_gain[e, 1], bias_m, bias_n, cbias)

            mq, mk, mv, nq, ck_raw, cv_raw, sk, sv, wk, wv, gl = [a.reshape(bs, 1, -1) for a in proj_s]
            hd = lambda a, n: a.reshape(bs, 1, n, HEAD_DIM)
            proj5 = (hd(mq, 4), hd(mk, 4), hd(mv, 4), nq.reshape(bs, 1, NSA_KV, NSA_GROUP, HEAD_DIM),
                     hd(ck_raw, 2), hd(cv_raw, 2), hd(sk, 2), hd(sv, 2), hd(wk, 2), hd(wv, 2),
                     gl[..., :36].reshape(bs, 1, NSA_KV, NSA_GROUP, 3))
            past = [_x_gather_past(c, page_table, e) for c in
                    (cache_moba_k, cache_moba_v, cache_nsa_cmp_k, cache_nsa_cmp_v, cache_nsa_slc_k, cache_nsa_slc_v)]
            o_m, o_n = _x_even_sample(proj5, past, state_nsa_win_k[:, :, e], state_nsa_win_v[:, :, e],
                                      nsa_qk_gain[e], nsa_cmp_pe[e], nsa_cmp_w1[e], nsa_cmp_w2[e], tab_moba, tab_nsa)
            os_ = jnp.concatenate([o_m.reshape(bs, MOBA_W), o_n.reshape(bs, NSA_W)], axis=-1)[None]

            wb = min(WINDOW, t)
            new_p = list(proj_p[1:3]) + list(proj_p[4:8]) + [proj_p[8][:, t - wb:], proj_p[9][:, t - wb:]]
            keep = min(WINDOW, PAST_LEN + 1)
            kw = jnp.concatenate([state_nsa_win_k[:, :, e], proj5[8]], axis=1)[:, -keep:]
            vw = jnp.concatenate([state_nsa_win_v[:, :, e], proj5[9]], axis=1)[:, -keep:]
            new_s = [proj5[1], proj5[2], proj5[4], proj5[5], proj5[6], proj5[7], kw, vw]
            for acc, r in zip(rows_ep, new_p):
                acc.append(r)
            for acc, r in zip(rows_es, new_s):
                acc.append(r)
        else:
            o = li // 2
            w_p, gain_cols = _odd_weights(odd_w_in[o], dsa_qk_gain[o])
            w_o = odd_w_out[o].astype(BF16)
            proj_p = mixer_project(yp, mod(mp, 1, 0), mod(mp, 1, 1), norm_gain[li, 1], w_p, gain_cols, ODD_PLAN)
            proj_s = mixer_project(ys, mod(ms, 1, 0), mod(ms, 1, 1), norm_gain[li, 1], w_p, gain_cols, ODD_PLAN)
            op = _odd_prompt(proj_p, bias_d)

            q, k, v, qi, ki, wi = [a.reshape(bs, 1, -1) for a in proj_s]
            k4, v4, ki1 = k.reshape(bs, 1, DSA_KV, HEAD_DIM), v.reshape(bs, 1, DSA_KV, HEAD_DIM), ki[..., :IDX_DIM]
            pk, pv, pki = [_x_gather_past(c, page_table, o) for c in (cache_dsa_k, cache_dsa_v, cache_dsa_idx_k)]
            cat = lambda a, b: jnp.concatenate([a, b], axis=1)
            kk = min(DSA_TOPK, (PAST_LEN + 1) // 4)
            os_ = _x_dsa_core(q.reshape(bs, 1, DSA_KV, DSA_GROUP, HEAD_DIM), qi.reshape(bs, 1, IDX_HEADS, IDX_DIM),
                              wi[..., :IDX_HEADS], PAST_LEN + jnp.arange(1), cat(pk, k4), cat(pv, v4), cat(pki, ki1),
                              kk, tab_dsa).reshape(1, bs, DSA_W)
            for acc, r in zip(rows_op, (proj_p[1], proj_p[2], proj_p[4][..., :IDX_DIM])):
                acc.append(r)
            for acc, r in zip(rows_os, (k4, v4, ki1)):
                acc.append(r)
        yp = mixer_merge(yp, op, mod(mp, 1, 2), w_o)
        ys = mixer_merge(ys, os_, mod(ms, 1, 2), w_o)
        yp = ffn_half(yp, mod(mp, 2, 0), mod(mp, 2, 1), mod(mp, 2, 2), norm_gain[li, 2], ffn_in[li, 1], ffn_out[li, 1])
        ys = ffn_half(ys, mod(ms, 2, 0), mod(ms, 2, 1), mod(ms, 2, 2), norm_gain[li, 2], ffn_in[li, 1], ffn_out[li, 1])

    def stack_p(rows, n):
        return jnp.stack([r.reshape(bp, r.shape[1], n, HEAD_DIM) for r in rows], axis=2)

    moba_k_p, moba_v_p = stack_p(rows_ep[0], 4), stack_p(rows_ep[1], 4)
    cmp_k_p, cmp_v_p, slc_k_p, slc_v_p, win_k_p, win_v_p = [stack_p(r, 2) for r in rows_ep[2:]]
    moba_k_s, moba_v_s, cmp_k_s, cmp_v_s, slc_k_s, slc_v_s, win_k_s, win_v_s = [jnp.stack(r, axis=2) for r in rows_es]
    dsa_k_p, dsa_v_p = stack_p(rows_op[0], 4), stack_p(rows_op[1], 4)
    dsa_idx_k_p = jnp.stack(rows_op[2], axis=2)
    dsa_k_s, dsa_v_s, dsa_idx_k_s = [jnp.stack(r, axis=2) for r in rows_os]
    return (yp, ys.reshape(bs, 1, D_MODEL), moba_k_p, moba_k_s, moba_v_p, moba_v_s, cmp_k_p, cmp_k_s, cmp_v_p,
            cmp_v_s, slc_k_p, slc_k_s, slc_v_p, slc_v_s, win_k_p, win_k_s, win_v_p, win_v_s,
            dsa_k_p, dsa_k_s, dsa_v_p, dsa_v_s, dsa_idx_k_p, dsa_idx_k_s)
```

```python
import functools
import math

import jax
import jax.numpy as jnp
from jax import lax
from jax.experimental import pallas as pl
from jax.experimental.pallas import tpu as pltpu

D_MODEL = 1024
DEPTH = 4
PAST_LEN = 2048
PAGE_SIZE = 128
N_PAGES = PAST_LEN // PAGE_SIZE
HEAD_DIM = 64
MOBA_HEADS = 4
NSA_HEADS = 12
NSA_KV = 2
NSA_GROUP = 6
DSA_HEADS = 16
DSA_KV = 4
DSA_GROUP = 4
IDX_HEADS = 8
IDX_DIM = 64
MOBA_BLOCK = 256
MOBA_TOPK = 3
CMP_LEN = 32
CMP_STRIDE = 16
CMP_HIDDEN = 128
SEL_BLOCK = 64
N_SEL = 8
WINDOW = 512
DSA_TOPK = 256
N_BUCKETS = 32
MAX_DISTANCE = 128
D_FF = 2816
MOBA_W = MOBA_HEADS * HEAD_DIM
NSA_W = NSA_HEADS * HEAD_DIM
NSA_KW = NSA_KV * HEAD_DIM
DSA_W = DSA_HEADS * HEAD_DIM
DSA_KW = DSA_KV * HEAD_DIM
RMS_EPS = 1e-6
NEG_INF = -1e30
TINY = 1e-30
SEL_FORCE = 1e4
QK_SCALE = HEAD_DIM ** -0.5

LANES = 128
VMEM_LIMIT = 56 * 1024 * 1024
BF16 = jnp.bfloat16
F32 = jnp.float32
HIGHEST = lax.Precision.HIGHEST
NT_DIMS = (((1,), (1,)), ((), ()))
M_INIT = -3e38
INT_MIN = -2 ** 31


def _cparams(*sem):
    return pltpu.CompilerParams(dimension_semantics=sem, vmem_limit_bytes=VMEM_LIMIT)


def _t5_bucket(dist):
    n = jnp.maximum(dist, 0)
    exact = N_BUCKETS // 2
    nf = jnp.maximum(n, 1).astype(F32)
    large = exact + (jnp.log(nf / exact) / math.log(MAX_DISTANCE / exact) * (N_BUCKETS - exact)).astype(jnp.int32)
    return jnp.where(n < exact, n, jnp.minimum(large, N_BUCKETS - 1))


def _rel_bias(dist, table):
    onehot = jax.nn.one_hot(_t5_bucket(dist), N_BUCKETS, dtype=F32)
    return jnp.einsum("...k,kh->h...", onehot, table, precision=HIGHEST)


def _adaln_kernel(c_ref, w_ref, b_ref, o_ref):
    c = c_ref[...]
    s = (c * jax.nn.sigmoid(c)).astype(BF16)
    o_ref[0] = jnp.dot(s, w_ref[0].astype(BF16), preferred_element_type=F32) + b_ref[0]


def adaln_all(c_all, ada_w, ada_b):
    r = c_all.shape[0]
    n_out = ada_w.shape[2]
    tn = 1024
    return pl.pallas_call(
        _adaln_kernel,
        grid=(DEPTH, n_out // tn),
        in_specs=[
            pl.BlockSpec((r, D_MODEL), lambda l, j: (0, 0)),
            pl.BlockSpec((1, D_MODEL, tn), lambda l, j: (l, 0, j)),
            pl.BlockSpec((1, 1, tn), lambda l, j: (l, 0, j)),
        ],
        out_specs=pl.BlockSpec((1, r, tn), lambda l, j: (l, 0, j)),
        out_shape=jax.ShapeDtypeStruct((DEPTH, r, n_out), F32),
        compiler_params=_cparams("parallel", "parallel"),
    )(c_all, ada_w, ada_b.reshape(DEPTH, 1, n_out))


def _modnorm(x, g, scale, shift):
    y = x * lax.rsqrt(jnp.mean(x * x, axis=-1, keepdims=True) + RMS_EPS)
    return (y * g) * (1.0 + scale) + shift


def _mod_spec(tm_rows, t_mod):
    if t_mod == 1:
        return pl.BlockSpec((1, 1, D_MODEL), lambda b, i, *_: (b, 0, 0))
    return pl.BlockSpec((1, tm_rows, D_MODEL), lambda b, i, *_: (b, i, 0))


def _seg_ones():
    return jnp.kron(jnp.eye(LANES // HEAD_DIM, dtype=F32), jnp.ones((HEAD_DIM, HEAD_DIM), F32))


def _head_rms(z, seg_ref, gain):
    ss = jnp.dot(z * z, seg_ref[...], preferred_element_type=F32, precision=HIGHEST)
    return z * lax.rsqrt(ss * (1.0 / HEAD_DIM) + RMS_EPS) * gain


def _ffn_kernel(x_ref, sh_ref, sc_ref, gt_ref, g_ref, wa_ref, wg_ref, wo_ref, o_ref, xn_ref, acc_ref):
    j = pl.program_id(2)

    @pl.when(j == 0)
    def _():
        xn_ref[...] = _modnorm(x_ref[0], g_ref[...], sc_ref[0], sh_ref[0]).astype(BF16)
        acc_ref[...] = jnp.zeros_like(acc_ref)

    xn = xn_ref[...]
    a = jnp.dot(xn, wa_ref[...], preferred_element_type=F32)
    g = jnp.dot(xn, wg_ref[...], preferred_element_type=F32)
    h = ((g * jax.nn.sigmoid(g)) * a).astype(BF16)
    acc_ref[...] += jnp.dot(h, wo_ref[...], preferred_element_type=F32)

    @pl.when(j == pl.num_programs(2) - 1)
    def _():
        o_ref[0] = x_ref[0] + (0.5 * gt_ref[0]) * acc_ref[...]


def ffn_half(x, shift, scale, gate, g, w_in, w_out):
    b, t, _ = x.shape
    tm = min(t, 512)
    fc = D_FF // 2
    nf = D_FF // fc
    mspec = _mod_spec(tm, shift.shape[1])
    return pl.pallas_call(
        _ffn_kernel,
        grid=(b, t // tm, nf),
        in_specs=[
            pl.BlockSpec((1, tm, D_MODEL), lambda bb, i, j: (bb, i, 0)),
            mspec, mspec, mspec,
            pl.BlockSpec((1, D_MODEL), lambda bb, i, j: (0, 0)),
            pl.BlockSpec((D_MODEL, fc), lambda bb, i, j: (0, j)),
            pl.BlockSpec((D_MODEL, fc), lambda bb, i, j: (0, j + nf)),
            pl.BlockSpec((fc, D_MODEL), lambda bb, i, j: (j, 0)),
        ],
        out_specs=pl.BlockSpec((1, tm, D_MODEL), lambda bb, i, j: (bb, i, 0)),
        out_shape=jax.ShapeDtypeStruct(x.shape, F32),
        scratch_shapes=[pltpu.VMEM((tm, D_MODEL), BF16), pltpu.VMEM((tm, D_MODEL), F32)],
        compiler_params=_cparams("parallel", "parallel", "arbitrary"),
    )(x, shift, scale, gate, g.reshape(1, D_MODEL), w_in, w_in, w_out)


def _proj_kernel(modes, outs, x_ref, sh_ref, sc_ref, g_ref, w_ref, gain_ref, seg_ref, *o_refs):
    h = _modnorm(x_ref[0], g_ref[...], sc_ref[0], sh_ref[0])
    z = jnp.dot(h.astype(BF16), w_ref[...], preferred_element_type=F32)
    chunks = []
    for c, mode in enumerate(modes):
        zc = z[:, c * LANES:(c + 1) * LANES]
        if mode == "norm":
            zc = _head_rms(zc, seg_ref, gain_ref[:, c * LANES:(c + 1) * LANES])
        elif mode == "sigmoid":
            zc = jax.nn.sigmoid(zc)
        chunks.append(zc)
    for o_ref, out in zip(o_refs, outs):
        if out[0] == "cols":
            _, start, width = out
            for c in range(width // LANES):
                o_ref[0, :, c * LANES:(c + 1) * LANES] = chunks[start // LANES + c]
        else:
            _, start, n_heads, dtype, scale = out
            for hh in range(n_heads):
                lo = start + hh * HEAD_DIM
                piece = chunks[lo // LANES][:, lo % LANES:lo % LANES + HEAD_DIM]
                o_ref[0, hh] = (piece * scale).astype(dtype)


def mixer_project(x, shift, scale, g, w, gain_cols, modes, outs):
    b, t, _ = x.shape
    tm = min(t, 512)
    wp = w.shape[1]
    mspec = _mod_spec(tm, shift.shape[1])
    out_shapes, out_specs = [], []
    for out in outs:
        if out[0] == "cols":
            out_shapes.append(jax.ShapeDtypeStruct((b, t, out[2]), F32))
            out_specs.append(pl.BlockSpec((1, tm, out[2]), lambda bb, i: (bb, i, 0)))
        else:
            out_shapes.append(jax.ShapeDtypeStruct((b, out[2], t, HEAD_DIM), out[3]))
            out_specs.append(pl.BlockSpec((1, out[2], tm, HEAD_DIM), lambda bb, i: (bb, 0, i, 0)))
    return pl.pallas_call(
        functools.partial(_proj_kernel, modes, outs),
        grid=(b, t // tm),
        in_specs=[
            pl.BlockSpec((1, tm, D_MODEL), lambda bb, i: (bb, i, 0)),
            mspec, mspec,
            pl.BlockSpec((1, D_MODEL), lambda bb, i: (0, 0)),
            pl.BlockSpec((D_MODEL, wp), lambda bb, i: (0, 0)),
            pl.BlockSpec((1, wp), lambda bb, i: (0, 0)),
            pl.BlockSpec((LANES, LANES), lambda bb, i: (0, 0)),
        ],
        out_specs=out_specs,
        out_shape=out_shapes,
        compiler_params=_cparams("parallel", "parallel"),
    )(x, shift, scale, g.reshape(1, D_MODEL), w, gain_cols, _seg_ones())


def _merge_kernel(n_pieces, y_ref, gt_ref, w_ref, *refs):
    out_ref = refs[n_pieces]
    heads = [refs[k][0, hh] for k in range(n_pieces) for hh in range(refs[k].shape[1])]
    o = jnp.concatenate(heads, axis=1).astype(BF16)
    out_ref[0] = y_ref[0] + gt_ref[0] * jnp.dot(o, w_ref[...], preferred_element_type=F32)


def mixer_merge(y, pieces, gate, w_out):
    b, t, _ = y.shape
    tm = min(t, 512)
    piece_specs = [pl.BlockSpec((1, p.shape[1], tm, HEAD_DIM), lambda bb, i: (bb, 0, i, 0)) for p in pieces]
    return pl.pallas_call(
        functools.partial(_merge_kernel, len(pieces)),
        grid=(b, t // tm),
        in_specs=[
            pl.BlockSpec((1, tm, D_MODEL), lambda bb, i: (bb, i, 0)),
            _mod_spec(tm, gate.shape[1]),
            pl.BlockSpec((D_MODEL, D_MODEL), lambda bb, i: (0, 0)),
        ] + piece_specs,
        out_specs=pl.BlockSpec((1, tm, D_MODEL), lambda bb, i: (bb, i, 0)),
        out_shape=jax.ShapeDtypeStruct(y.shape, F32),
        compiler_params=_cparams("parallel", "parallel"),
    )(y, gate, w_out, *pieces)


TQ = 128
CK = 2 * TQ


def _bias_tiles(table, t):
    d = jnp.arange(t)[:, None] - jnp.arange(t)[None, :]
    diag = jnp.where(d >= 0, _rel_bias(d, table), NEG_INF)
    adj = _rel_bias(t + d, table)
    far = _rel_bias(2 * t + d, table)
    masked = jnp.full_like(far, NEG_INF)
    edge = jnp.where(d < 0, far, NEG_INF)
    return jnp.stack([diag, adj, far, masked, edge], axis=1)


def _tile_kind(d):
    return jnp.where(d < 0, 3, jnp.minimum(d, 2))


def _softmax_step(carry, qb, k, v, add):
    m, l, acc = carry
    s = lax.dot_general(qb, k, NT_DIMS, preferred_element_type=F32) + add
    m_new = jnp.maximum(m, jnp.max(s, axis=1, keepdims=True))
    alpha = jnp.exp(m - m_new)
    p = jnp.exp(s - m_new)
    l = alpha * l + jnp.sum(p, axis=1, keepdims=True)
    acc = alpha * acc + jnp.dot(p.astype(BF16), v, preferred_element_type=F32)
    return m_new, l, acc


def _softmax_init(rows):
    return jnp.full((rows, 1), M_INIT, F32), jnp.zeros((rows, 1), F32), jnp.zeros((rows, HEAD_DIM), F32)


def _softmax_done(carry):
    _, l, acc = carry
    return acc / jnp.maximum(l, TINY)


def _top_k_rows(score, allowed, k):
    lane = lax.broadcasted_iota(jnp.int32, score.shape, 1)
    n = score.shape[1]
    remaining = allowed
    picked = jnp.zeros(score.shape, jnp.bool_)
    for _ in range(k):
        cur = jnp.where(remaining, score, -jnp.inf)
        best = jnp.max(cur, axis=1, keepdims=True)
        cand = jnp.logical_and(remaining, cur == best)
        first = jnp.min(jnp.where(cand, lane, n), axis=1, keepdims=True)
        hit = lane == first
        picked = jnp.logical_or(picked, hit)
        remaining = jnp.logical_and(remaining, jnp.logical_not(hit))
    return picked


def _count(mask):
    return jnp.sum(jnp.where(mask, 1.0, 0.0), axis=1, keepdims=True)


def _top_k_mask(score, valid, pos, k):
    bits = pltpu.bitcast(score, jnp.int32)
    key = jnp.where(bits < 0, bits ^ jnp.int32(0x7FFFFFFF), bits)
    key = jnp.where(valid, key, jnp.int32(INT_MIN))
    kf = float(k)

    thr = jnp.full((score.shape[0], 1), INT_MIN, jnp.int32)
    cand = thr ^ jnp.int32(INT_MIN)
    thr = jnp.where(_count(key >= cand) >= kf, cand, thr)

    def step(b, thr):
        cand = thr | lax.shift_left(jnp.int32(1), 30 - b)
        return jnp.where(_count(key >= cand) >= kf, cand, thr)

    thr = lax.fori_loop(0, 31, step, thr)
    above = key > thr
    tie = key == thr
    need = kf - _count(above)

    n_bits = max(int(score.shape[1] - 1).bit_length(), 1)

    def tie_step(b, cut):
        cand = cut | lax.shift_left(jnp.int32(1), n_bits - 1 - b)
        return jnp.where(_count(jnp.logical_and(tie, pos < cand)) < need, cand, cut)

    cut = lax.fori_loop(0, n_bits, tie_step, jnp.zeros((score.shape[0], 1), jnp.int32))
    keep = jnp.logical_or(above, jnp.logical_and(tie, pos <= cut))
    return jnp.logical_and(keep, valid)


def _moba_kernel(q_ref, k_ref, v_ref, bias_ref, o_ref, km_ref):
    i = pl.program_id(2)
    nb = km_ref.shape[0]
    t = MOBA_BLOCK

    @pl.when(i == 0)
    def _():
        km_ref[...] = jnp.mean(k_ref[0, 0].reshape(nb, t, HEAD_DIM), axis=1)

    q = q_ref[0, 0]
    gate = lax.dot_general(q, km_ref[...], NT_DIMS, preferred_element_type=F32, precision=HIGHEST)
    blk = lax.broadcasted_iota(jnp.int32, gate.shape, 1)
    chosen = jnp.logical_or(_top_k_rows(gate, blk < i, MOBA_TOPK), blk == i)
    blk_add = jnp.where(chosen, 0.0, NEG_INF)
    qb = (q * QK_SCALE).astype(BF16)

    def body(j, carry):
        rows = pl.ds(pl.multiple_of(j * t, t), t)
        col = jnp.sum(jnp.where(blk == j, blk_add, 0.0), axis=1, keepdims=True)
        add = bias_ref[0, jnp.minimum(i - j, 2)] + col
        return _softmax_step(carry, qb, k_ref[0, 0, rows, :].astype(BF16), v_ref[0, 0, rows, :], add)

    o_ref[0, 0] = _softmax_done(lax.fori_loop(0, i + 1, body, _softmax_init(t)))


def moba_prompt(q, k, v, bias):
    b, h, t, _ = q.shape
    tb = MOBA_BLOCK
    nb = t // tb
    full = pl.BlockSpec((1, 1, t, HEAD_DIM), lambda bb, hh, i: (bb, hh, 0, 0))
    return pl.pallas_call(
        _moba_kernel,
        grid=(b, h, nb),
        in_specs=[
            pl.BlockSpec((1, 1, tb, HEAD_DIM), lambda bb, hh, i: (bb, hh, i, 0)),
            full, full,
            pl.BlockSpec((1, 3, tb, tb), lambda bb, hh, i: (hh, 0, 0, 0)),
        ],
        out_specs=pl.BlockSpec((1, 1, tb, HEAD_DIM), lambda bb, hh, i: (bb, hh, i, 0)),
        out_shape=jax.ShapeDtypeStruct(q.shape, F32),
        scratch_shapes=[pltpu.VMEM((nb, HEAD_DIM), F32)],
        compiler_params=_cparams("parallel", "parallel", "arbitrary"),
    )(q, k, v, bias)


N_CHUNK = PAST_LEN // CMP_STRIDE
CHUNK_W = CMP_STRIDE * NSA_KW
N_CMP = (PAST_LEN - CMP_LEN) // CMP_STRIDE + 1
HALF_W = NSA_KV * CMP_HIDDEN


def _compress_weights(pe, w1, w2):
    eye = jnp.eye(NSA_KV, dtype=F32)
    pe_x = jnp.broadcast_to(pe.reshape(2, 2, CMP_STRIDE, 1, HEAD_DIM),
                            (2, 2, CMP_STRIDE, NSA_KV, HEAD_DIM)).reshape(2, 2, CHUNK_W)
    w1_h = w1.reshape(2, 2, CMP_STRIDE, HEAD_DIM, CMP_HIDDEN)
    w1_x = jnp.einsum("thrdj,gq->trgdhqj", w1_h, eye).reshape(2, CHUNK_W, 2 * HALF_W)
    w2_x = jnp.einsum("tjd,gq->tgjqd", w2, eye).reshape(2, HALF_W, NSA_KW)
    return pe_x, w1_x.astype(BF16), w2_x.astype(BF16)


def _compress_tail(ha, hb, w2):
    hid = jax.nn.gelu(ha + pltpu.roll(hb, N_CHUNK - 1, 0))
    return jnp.dot(hid.astype(BF16), w2, preferred_element_type=F32)


def _compress_one(r, pe_ref, w1_ref, w2_ref, t):
    ha = jnp.dot((r + pe_ref[t, 0:1]).astype(BF16), w1_ref[t, :, :HALF_W], preferred_element_type=F32)
    hb = jnp.dot((r + pe_ref[t, 1:2]).astype(BF16), w1_ref[t, :, HALF_W:], preferred_element_type=F32)
    return _compress_tail(ha, hb, w2_ref[t])


def _compress_kernel(rk_ref, rv_ref, pe_ref, w1_ref, w2_ref, gain_ref, seg_ref, ck_ref, cv_ref):
    ck_ref[0] = _head_rms(_compress_one(rk_ref[0], pe_ref, w1_ref, w2_ref, 0), seg_ref, gain_ref[...])
    cv_ref[0] = _compress_one(rv_ref[0], pe_ref, w1_ref, w2_ref, 1)


def nsa_compress_prompt(ck_raw, cv_raw, pe_x, w1_x, w2_x, gain_row):
    b = ck_raw.shape[0]
    rows = pl.BlockSpec((1, N_CHUNK, CHUNK_W), lambda bb: (bb, 0, 0))
    out = pl.BlockSpec((1, N_CHUNK, NSA_KW), lambda bb: (bb, 0, 0))
    const = lambda *shape: pl.BlockSpec(shape, lambda bb: (0,) * len(shape))
    return pl.pallas_call(
        _compress_kernel,
        grid=(b,),
        in_specs=[rows, rows, const(2, 2, CHUNK_W), const(2, CHUNK_W, 2 * HALF_W),
                  const(2, HALF_W, NSA_KW), const(1, NSA_KW), const(LANES, LANES)],
        out_specs=[out, out],
        out_shape=[jax.ShapeDtypeStruct((b, N_CHUNK, NSA_KW), F32)] * 2,
        compiler_params=_cparams("parallel"),
    )(ck_raw.reshape(b, N_CHUNK, CHUNK_W), cv_raw.reshape(b, N_CHUNK, CHUNK_W), pe_x, w1_x, w2_x, gain_row,
      _seg_ones())


def _overlap_matrix():
    c = jnp.arange(LANES)[:, None]
    n = jnp.arange(LANES)[None, :]
    return jnp.logical_and(c * CMP_STRIDE < n * SEL_BLOCK + SEL_BLOCK,
                           c * CMP_STRIDE + CMP_LEN > n * SEL_BLOCK).astype(F32)


def _block_expand(n_keys):
    return (jnp.arange(n_keys)[None, :] // SEL_BLOCK == jnp.arange(LANES)[:, None]).astype(BF16)


def _nsa_kernel(q_ref, gt_ref, ck_ref, cv_ref, sk_ref, sv_ref, wk_ref, wv_ref, bias_ref, cb_ref, ov_ref, ex_ref,
                o_ref, madd_ref):
    g = pl.program_id(1)
    i = pl.program_id(2)
    hg = NSA_GROUP
    rows = hg * TQ
    nq = madd_ref.shape[0]
    row = lax.broadcasted_iota(jnp.int32, (TQ, LANES), 0)
    lane = lax.broadcasted_iota(jnp.int32, (TQ, LANES), 1)
    qpos = i * TQ + row
    qf = q_ref[0].reshape(rows, HEAD_DIM)

    def group_half(x):
        return jnp.where(g == 0, x[:, :HEAD_DIM], x[:, HEAD_DIM:])

    cmask = jnp.logical_and(qpos - (lane * CMP_STRIDE + CMP_LEN - 1) >= 0, lane < N_CMP)
    cmask_all = jnp.concatenate([cmask] * hg, axis=0)
    lc = lax.dot_general(qf, group_half(ck_ref[0]), NT_DIMS, preferred_element_type=F32, precision=HIGHEST)
    lc = jnp.where(cmask_all, lc * QK_SCALE + cb_ref[...].reshape(rows, LANES), NEG_INF)
    e = jnp.where(cmask_all, jnp.exp(lc - jnp.max(lc, axis=1, keepdims=True)), 0.0)
    pc = e / jnp.maximum(jnp.sum(e, axis=1, keepdims=True), TINY)
    o_cmp = jnp.dot(pc.astype(BF16), group_half(cv_ref[0]).astype(BF16), preferred_element_type=F32)
    psum = jnp.sum(pc.reshape(hg, TQ, LANES), axis=0)

    imp = jnp.dot(psum, ov_ref[...], preferred_element_type=F32, precision=HIGHEST)
    cur = qpos // SEL_BLOCK
    forced = jnp.logical_or(lane == 0, jnp.logical_or(lane == cur, lane == cur - 1))
    picked = _top_k_rows(jnp.where(forced, SEL_FORCE, imp), lane <= cur, N_SEL)
    pick_b = jnp.where(picked, 1.0, 0.0).astype(BF16)

    def fill(j, _):
        hit = jnp.dot(pick_b, ex_ref[:, pl.ds(pl.multiple_of(j * TQ, TQ), TQ)], preferred_element_type=F32)
        madd_ref[j] = jnp.where(hit > 0.5, 0.0, NEG_INF)
        return 0

    lax.fori_loop(0, jnp.minimum(i + 2, nq), fill, 0)

    qb = (qf * QK_SCALE).astype(BF16)

    def bias_rows(kind):
        return bias_ref[:, kind].reshape(rows, TQ)

    def key_tile(ref, j):
        return ref[0, 0, pl.ds(pl.multiple_of(j * TQ, TQ), TQ), :]

    def slc_body(c, carry):
        j0 = 2 * c
        mask = jnp.concatenate([madd_ref[j0], madd_ref[j0 + 1]], axis=1)
        add = jnp.concatenate([bias_rows(_tile_kind(i - j0)), bias_rows(_tile_kind(i - j0 - 1))], axis=1)
        add = add + jnp.concatenate([mask] * hg, axis=0)
        keys = pl.ds(pl.multiple_of(c * CK, CK), CK)
        return _softmax_step(carry, qb, sk_ref[0, 0, keys, :], sv_ref[0, 0, keys, :], add)

    o_slc = _softmax_done(lax.fori_loop(0, i // 2 + 1, slc_body, _softmax_init(rows)))

    n_win = WINDOW // TQ
    carry = _softmax_init(rows)
    for w in range((n_win + 2) // 2):
        kinds, ks, vs = [], [], []
        for j in (i - n_win - 1 + 2 * w, i - n_win + 2 * w):
            d = i - j
            kind = jnp.where(d == n_win, 4, jnp.where(d > n_win, 3, jnp.minimum(d, 2)))
            kinds.append(jnp.where(j < 0, 3, kind))
            ks.append(key_tile(wk_ref, jnp.maximum(j, 0)))
            vs.append(key_tile(wv_ref, jnp.maximum(j, 0)))
        add = jnp.concatenate([bias_rows(kinds[0]), bias_rows(kinds[1])], axis=1)
        carry = _softmax_step(carry, qb, jnp.concatenate(ks, axis=0), jnp.concatenate(vs, axis=0), add)
    o_win = _softmax_done(carry)

    gt = gt_ref[0]

    def gate_col(branch):
        cols = []
        for h in range(hg):
            c0 = 3 * h + branch
            cols.append(jnp.where(g == 0, gt[:, c0:c0 + 1], gt[:, 3 * hg + c0:3 * hg + c0 + 1]))
        return jnp.concatenate(cols, axis=0)

    o = gate_col(0) * o_cmp + gate_col(1) * o_slc + gate_col(2) * o_win
    o_ref[0] = o.reshape(hg, TQ, HEAD_DIM)


def nsa_prompt(q, gates, ck, cv, sk, sv, wk, wv, bias, cbias):
    b, _, t, _ = q.shape
    hg = NSA_GROUP
    nq = t // TQ
    kv = pl.BlockSpec((1, 1, t, HEAD_DIM), lambda bb, gg, i: (bb, gg, 0, 0))
    cmp_spec = pl.BlockSpec((1, LANES, NSA_KW), lambda bb, gg, i: (bb, 0, 0))
    return pl.pallas_call(
        _nsa_kernel,
        grid=(b, NSA_KV, nq),
        in_specs=[
            pl.BlockSpec((1, hg, TQ, HEAD_DIM), lambda bb, gg, i: (bb, gg, i, 0)),
            pl.BlockSpec((1, TQ, LANES), lambda bb, gg, i: (bb, i, 0)),
            cmp_spec, cmp_spec, kv, kv, kv, kv,
            pl.BlockSpec((hg, 5, TQ, TQ), lambda bb, gg, i: (gg, 0, 0, 0)),
            pl.BlockSpec((hg, TQ, LANES), lambda bb, gg, i: (gg, i, 0)),
            pl.BlockSpec((LANES, LANES), lambda bb, gg, i: (0, 0)),
            pl.BlockSpec((LANES, t), lambda bb, gg, i: (0, 0)),
        ],
        out_specs=pl.BlockSpec((1, hg, TQ, HEAD_DIM), lambda bb, gg, i: (bb, gg, i, 0)),
        out_shape=jax.ShapeDtypeStruct(q.shape, F32),
        scratch_shapes=[pltpu.VMEM((nq, TQ, TQ), F32)],
        compiler_params=_cparams("parallel", "parallel", "arbitrary"),
    )(q, gates, ck, cv, sk, sv, wk, wv, bias, cbias, _overlap_matrix(), _block_expand(t))


def _dsa_index_kernel(qi_ref, wi_ref, ki_ref, o_ref):
    i = pl.program_id(1)
    t = ki_ref.shape[1]
    ki = ki_ref[0][:, :IDX_DIM]
    wi = wi_ref[0]
    score = jnp.zeros((TQ, t), F32)
    for h in range(IDX_HEADS):
        s = lax.dot_general(qi_ref[0, h], ki, NT_DIMS, preferred_element_type=F32,
                            precision=HIGHEST) * IDX_DIM ** -0.5
        score = score + wi[:, h:h + 1] * jnp.maximum(s, 0.0)
    score = score * IDX_HEADS ** -0.5
    kpos = lax.broadcasted_iota(jnp.int32, (TQ, t), 1)
    qpos = i * TQ + lax.broadcasted_iota(jnp.int32, (TQ, t), 0)
    keep = _top_k_mask(score, kpos <= qpos, kpos, min(DSA_TOPK, t // 4))
    add = jnp.where(keep, 0.0, NEG_INF).astype(BF16)
    for j in range(t // TQ):
        o_ref[0, 0, j] = add[:, j * TQ:(j + 1) * TQ]


def dsa_index_prompt(qi, wi, ki):
    b, ih, t, _ = qi.shape
    nq = t // TQ
    return pl.pallas_call(
        _dsa_index_kernel,
        grid=(b, nq),
        in_specs=[
            pl.BlockSpec((1, ih, TQ, IDX_DIM), lambda bb, i: (bb, 0, i, 0)),
            pl.BlockSpec((1, TQ, LANES), lambda bb, i: (bb, i, 0)),
            pl.BlockSpec((1, t, LANES), lambda bb, i: (bb, 0, 0)),
        ],
        out_specs=pl.BlockSpec((1, 1, nq, TQ, TQ), lambda bb, i: (bb, i, 0, 0, 0)),
        out_shape=jax.ShapeDtypeStruct((b, nq, nq, TQ, TQ), BF16),
        compiler_params=_cparams("parallel", "parallel"),
    )(qi, wi, ki)


def _dsa_attn_kernel(q_ref, k_ref, v_ref, bias_ref, m_ref, o_ref):
    i = pl.program_id(2)
    hg = DSA_GROUP
    rows = hg * TQ
    qb = q_ref[0].reshape(rows, HEAD_DIM)

    def body(c, carry):
        j0 = 2 * c
        mask = jnp.concatenate([m_ref[0, 0, j0], m_ref[0, 0, j0 + 1]], axis=1).astype(F32)
        add = jnp.concatenate([bias_ref[:, _tile_kind(i - j0)].reshape(rows, TQ),
                               bias_ref[:, _tile_kind(i - j0 - 1)].reshape(rows, TQ)], axis=1)
        add = add + jnp.concatenate([mask] * hg, axis=0)
        keys = pl.ds(pl.multiple_of(c * CK, CK), CK)
        return _softmax_step(carry, qb, k_ref[0, 0, keys, :], v_ref[0, 0, keys, :], add)

    o = _softmax_done(lax.fori_loop(0, i // 2 + 1, body, _softmax_init(rows)))
    o_ref[0] = o.reshape(hg, TQ, HEAD_DIM)


def dsa_attn_prompt(q, k, v, bias, mask):
    b, _, t, _ = q.shape
    hg = DSA_GROUP
    nq = t // TQ
    kv = pl.BlockSpec((1, 1, t, HEAD_DIM), lambda bb, gg, i: (bb, gg, 0, 0))
    return pl.pallas_call(
        _dsa_attn_kernel,
        grid=(b, DSA_KV, nq),
        in_specs=[
            pl.BlockSpec((1, hg, TQ, HEAD_DIM), lambda bb, gg, i: (bb, gg, i, 0)),
            kv, kv,
            pl.BlockSpec((hg, 5, TQ, TQ), lambda bb, gg, i: (gg, 0, 0, 0)),
            pl.BlockSpec((1, 1, nq, TQ, TQ), lambda bb, gg, i: (bb, i, 0, 0, 0)),
        ],
        out_specs=pl.BlockSpec((1, hg, TQ, HEAD_DIM), lambda bb, gg, i: (bb, gg, i, 0)),
        out_shape=jax.ShapeDtypeStruct(q.shape, F32),
        compiler_params=_cparams("parallel", "parallel", "arbitrary"),
    )(q, k, v, bias, mask)


def _page_specs(lane_block, width):
    return [pl.BlockSpec((1, PAGE_SIZE, width), functools.partial(lambda p, b, pt: (pt[b, p], 0, lane_block), p))
            for p in range(N_PAGES)]


def _row_spec(*shape):
    return pl.BlockSpec((1,) + shape, lambda b, pt: (b,) + (0,) * len(shape))


def _const_spec(*shape):
    return pl.BlockSpec(shape, lambda b, pt: (0,) * len(shape))


def _decode_call(kernel, n_req, page_table, in_specs, args, out_shapes, out_specs):
    return pl.pallas_call(
        kernel,
        grid_spec=pltpu.PrefetchScalarGridSpec(num_scalar_prefetch=1, grid=(n_req,), in_specs=in_specs,
                                               out_specs=out_specs),
        out_shape=out_shapes,
        compiler_params=_cparams("parallel"),
    )(page_table, *args)


def _paged_attention(qblk, k_pages, v_pages, bias_ref, mask_of_page, k_new, v_new, new_add):
    qb = (qblk * QK_SCALE).astype(BF16)
    s_new = jnp.sum(qblk * k_new, axis=1, keepdims=True) * QK_SCALE + new_add
    s_pages = []
    for p in range(N_PAGES):
        s = lax.dot_general(qb, k_pages[p][0].astype(BF16), NT_DIMS, preferred_element_type=F32)
        s_pages.append(s + bias_ref[:, p * PAGE_SIZE:(p + 1) * PAGE_SIZE] + mask_of_page(p))
    m = s_new
    for s in s_pages:
        m = jnp.maximum(m, jnp.max(s, axis=1, keepdims=True))
    e_new = jnp.exp(s_new - m)
    l = e_new
    acc = e_new * v_new
    for p in range(N_PAGES):
        e = jnp.exp(s_pages[p] - m)
        l = l + jnp.sum(e, axis=1, keepdims=True)
        acc = acc + jnp.dot(e.astype(BF16), v_pages[p][0].astype(BF16), preferred_element_type=F32)
    return acc / jnp.maximum(l, TINY)


def _group_lanes(o_full, heads_per_group):
    hp, w = o_full.shape
    grp = lax.broadcasted_iota(jnp.int32, (hp, HEAD_DIM), 0) // heads_per_group
    out = o_full[:, :HEAD_DIM]
    for gg in range(1, w // HEAD_DIM):
        out = jnp.where(grp == gg, o_full[:, gg * HEAD_DIM:(gg + 1) * HEAD_DIM], out)
    return out


MOBA_HP = 8


def _moba_decode_kernel(pt_ref, q_ref, kn_ref, vn_ref, bias_ref, bnew_ref, *refs):
    k_pages = refs[:N_PAGES]
    v_pages = refs[N_PAGES:2 * N_PAGES]
    o_ref = refs[2 * N_PAGES]
    qblk = q_ref[0]
    pages_per_block = MOBA_BLOCK // PAGE_SIZE
    n_blk = N_PAGES // pages_per_block
    means = []
    for blk in range(n_blk):
        tot = jnp.sum(k_pages[blk * pages_per_block][0], axis=0, keepdims=True)
        for p in range(blk * pages_per_block + 1, (blk + 1) * pages_per_block):
            tot = tot + jnp.sum(k_pages[p][0], axis=0, keepdims=True)
        means.append(tot * (1.0 / MOBA_BLOCK))
    k_mean = jnp.concatenate(means, axis=0)
    gate = lax.dot_general(qblk, k_mean, NT_DIMS, preferred_element_type=F32, precision=HIGHEST)
    chosen = _top_k_rows(gate, jnp.ones(gate.shape, jnp.bool_), MOBA_TOPK)
    blk_add = jnp.where(chosen, 0.0, NEG_INF)

    def mask_of_page(p):
        b0 = p // pages_per_block
        return blk_add[:, b0:b0 + 1]

    o_full = _paged_attention(qblk, k_pages, v_pages, bias_ref, mask_of_page, kn_ref[0], vn_ref[0],
                              bnew_ref[:, 0:1])
    o_ref[0] = _group_lanes(o_full, 1)


def moba_decode(page_table, qblk, k_new, v_new, bias, bnew, cache_k, cache_v, layer):
    n_req = qblk.shape[0]
    in_specs = ([_row_spec(MOBA_HP, MOBA_W), _row_spec(1, MOBA_W), _row_spec(1, MOBA_W),
                 _const_spec(MOBA_HP, PAST_LEN), _const_spec(MOBA_HP, LANES)]
                + _page_specs(layer, MOBA_W) + _page_specs(layer, MOBA_W))
    args = [qblk, k_new, v_new, bias, bnew] + [cache_k] * N_PAGES + [cache_v] * N_PAGES
    return _decode_call(_moba_decode_kernel, n_req, page_table, in_specs, args,
                        jax.ShapeDtypeStruct((n_req, MOBA_HP, HEAD_DIM), F32), _row_spec(MOBA_HP, HEAD_DIM))


NSA_HP = 16
GRP_ROWS = 8


def _compress_pages(pages, pe_ref, w1_ref, w2, t):
    ha = jnp.zeros((N_CHUNK, HALF_W), F32)
    hb = jnp.zeros((N_CHUNK, HALF_W), F32)
    per_page = PAGE_SIZE // CMP_STRIDE
    for r in range(CMP_STRIDE):
        xr = jnp.concatenate([pg[0, pl.ds(r, per_page, stride=CMP_STRIDE), :] for pg in pages], axis=0)
        lanes = slice(r * NSA_KW, (r + 1) * NSA_KW)
        w1 = w1_ref[t, r * NSA_KW:(r + 1) * NSA_KW, :]
        ha = ha + jnp.dot((xr + pe_ref[t, 0:1, lanes]).astype(BF16), w1[:, :HALF_W], preferred_element_type=F32)
        hb = hb + jnp.dot((xr + pe_ref[t, 1:2, lanes]).astype(BF16), w1[:, HALF_W:], preferred_element_type=F32)
    return _compress_tail(ha, hb, w2)


def _nsa_decode_kernel(pt_ref, q_ref, gt_ref, skn_ref, svn_ref, wkn_ref, wvn_ref, wkb_ref, wvb_ref,
                       pe_ref, w1_ref, w2_ref, gain_ref, seg_ref, cb_ref, sb_ref, wb_ref, bnew_ref,
                       ov_ref, ex_ref, g2_ref, p16_ref, *refs):
    ck_pages = refs[:N_PAGES]
    cv_pages = refs[N_PAGES:2 * N_PAGES]
    sk_pages = refs[2 * N_PAGES:3 * N_PAGES]
    sv_pages = refs[3 * N_PAGES:4 * N_PAGES]
    o_ref = refs[4 * N_PAGES]
    qblk = q_ref[0]
    new_add = bnew_ref[:, 0:1]
    lane = lax.broadcasted_iota(jnp.int32, (GRP_ROWS, LANES), 1)

    ck = _head_rms(_compress_pages(ck_pages, pe_ref, w1_ref, w2_ref[0], 0), seg_ref, gain_ref[...])
    cv = _compress_pages(cv_pages, pe_ref, w1_ref, w2_ref[1], 1)
    cvalid = lax.broadcasted_iota(jnp.int32, (NSA_HP, LANES), 1) < N_CMP
    lc = lax.dot_general(qblk, ck, NT_DIMS, preferred_element_type=F32, precision=HIGHEST)
    lc = jnp.where(cvalid, lc * QK_SCALE + cb_ref[...], NEG_INF)
    e = jnp.where(cvalid, jnp.exp(lc - jnp.max(lc, axis=1, keepdims=True)), 0.0)
    pc = e / jnp.maximum(jnp.sum(e, axis=1, keepdims=True), TINY)
    o_cmp = jnp.dot(pc.astype(BF16), cv.astype(BF16), preferred_element_type=F32)

    psum = jnp.dot(g2_ref[...], pc, preferred_element_type=F32, precision=HIGHEST)
    imp = jnp.dot(psum, ov_ref[...], preferred_element_type=F32, precision=HIGHEST)
    cur = PAST_LEN // SEL_BLOCK
    forced = jnp.logical_or(lane == 0, jnp.logical_or(lane == cur, lane == cur - 1))
    picked = _top_k_rows(jnp.where(forced, SEL_FORCE, imp), lane <= cur, N_SEL)
    pick_h = jnp.dot(p16_ref[...], jnp.where(picked, 1.0, 0.0).astype(BF16), preferred_element_type=F32)
    key_hit = jnp.dot(pick_h.astype(BF16), ex_ref[...], preferred_element_type=F32)
    key_add = jnp.where(key_hit > 0.5, 0.0, NEG_INF)

    o_slc = _paged_attention(qblk, sk_pages, sv_pages, sb_ref, lambda p: key_add[:, p * PAGE_SIZE:(p + 1) * PAGE_SIZE],
                             skn_ref[0], svn_ref[0], new_add)

    qb = (qblk * QK_SCALE).astype(BF16)
    sw = lax.dot_general(qb, wkb_ref[0].astype(BF16), NT_DIMS, preferred_element_type=F32) + wb_ref[...]
    s_new = jnp.sum(qblk * wkn_ref[0], axis=1, keepdims=True) * QK_SCALE + new_add
    m = jnp.maximum(s_new, jnp.max(sw, axis=1, keepdims=True))
    ew = jnp.exp(sw - m)
    e_new = jnp.exp(s_new - m)
    l = e_new + jnp.sum(ew, axis=1, keepdims=True)
    o_win = (e_new * wvn_ref[0] + jnp.dot(ew.astype(BF16), wvb_ref[0].astype(BF16), preferred_element_type=F32))
    o_win = o_win / jnp.maximum(l, TINY)

    gt = gt_ref[0]
    o_full = gt[:, 0:1] * o_cmp + gt[:, 1:2] * o_slc + gt[:, 2:3] * o_win
    o_ref[0] = _group_lanes(o_full, NSA_GROUP)


def nsa_decode(page_table, qblk, gates, sk_new, sv_new, wk_new, wv_new, win_k, win_v, cmp_w, gain_row,
               cbias, sbias, wbias, bnew, cache_ck, cache_cv, cache_sk, cache_sv, layer):
    n_req = qblk.shape[0]
    pe_x, w1_x, w2_x = cmp_w
    hh = jnp.arange(NSA_HP)
    g2 = jnp.logical_and(hh[None, :] // NSA_GROUP == jnp.arange(GRP_ROWS)[:, None], hh[None, :] < NSA_HEADS)
    p16 = (hh[:, None] // NSA_GROUP == jnp.arange(GRP_ROWS)[None, :]).astype(BF16)
    win_spec = pl.BlockSpec((1, WINDOW, NSA_KW), lambda b, pt: (b, 0, layer))
    in_specs = ([_row_spec(NSA_HP, NSA_KW), _row_spec(NSA_HP, LANES)] + [_row_spec(1, NSA_KW)] * 4
                + [win_spec, win_spec,
                   _const_spec(2, 2, CHUNK_W), _const_spec(2, CHUNK_W, 2 * HALF_W), _const_spec(2, HALF_W, NSA_KW),
                   _const_spec(1, NSA_KW), _const_spec(LANES, LANES),
                   _const_spec(NSA_HP, LANES), _const_spec(NSA_HP, PAST_LEN), _const_spec(NSA_HP, WINDOW),
                   _const_spec(NSA_HP, LANES), _const_spec(LANES, LANES), _const_spec(LANES, PAST_LEN),
                   _const_spec(GRP_ROWS, NSA_HP), _const_spec(NSA_HP, GRP_ROWS)]
                + _page_specs(layer, NSA_KW) * 4)
    args = ([qblk, gates, sk_new, sv_new, wk_new, wv_new, win_k, win_v, pe_x, w1_x, w2_x, gain_row, _seg_ones(),
             cbias, sbias, wbias, bnew, _overlap_matrix(), _block_expand(PAST_LEN), g2.astype(F32), p16]
            + [cache_ck] * N_PAGES + [cache_cv] * N_PAGES + [cache_sk] * N_PAGES + [cache_sv] * N_PAGES)
    return _decode_call(_nsa_decode_kernel, n_req, page_table, in_specs, args,
                        jax.ShapeDtypeStruct((n_req, NSA_HP, HEAD_DIM), F32), _row_spec(NSA_HP, HEAD_DIM))


KEYS_PAD = PAST_LEN + LANES


def _dsa_score_kernel(pt_ref, qi_ref, wi_ref, kin_ref, *refs):
    pages = refs[:N_PAGES]
    o_ref = refs[N_PAGES]
    qi = qi_ref[0]
    wi = wi_ref[0][:, 0:1]
    parts = []
    for p in range(N_PAGES):
        s = lax.dot_general(qi, pages[p][0], NT_DIMS, preferred_element_type=F32, precision=HIGHEST)
        parts.append(jnp.sum(wi * jnp.maximum(s * IDX_DIM ** -0.5, 0.0), axis=0, keepdims=True))
    s_new = jnp.sum(qi * kin_ref[0], axis=1, keepdims=True) * IDX_DIM ** -0.5
    new = jnp.sum(wi * jnp.maximum(s_new, 0.0), axis=0, keepdims=True)
    lane = lax.broadcasted_iota(jnp.int32, (1, LANES), 1)
    parts.append(jnp.where(lane == 0, new, 0.0))
    o_ref[0] = jnp.concatenate(parts, axis=1) * IDX_HEADS ** -0.5


def dsa_score_decode(page_table, qi, wi, ki_new, cache_idx):
    n_req = qi.shape[0]
    in_specs = ([_row_spec(IDX_HEADS, LANES), _row_spec(IDX_HEADS, LANES), _row_spec(1, LANES)]
                + _page_specs(0, LANES))
    args = [qi, wi, ki_new] + [cache_idx] * N_PAGES
    return _decode_call(_dsa_score_kernel, n_req, page_table, in_specs, args,
                        jax.ShapeDtypeStruct((n_req, 1, KEYS_PAD), F32), _row_spec(1, KEYS_PAD))


def _topk_rows_kernel(k, s_ref, o_ref):
    score = s_ref[...]
    pos = lax.broadcasted_iota(jnp.int32, score.shape, 1)
    keep = _top_k_mask(score, pos <= PAST_LEN, pos, k)
    o_ref[...] = jnp.where(keep, 0.0, NEG_INF)


def dsa_topk_decode(score):
    k = min(DSA_TOPK, (PAST_LEN + 1) // 4)
    return pl.pallas_call(
        functools.partial(_topk_rows_kernel, k),
        out_shape=jax.ShapeDtypeStruct(score.shape, F32),
        compiler_params=pltpu.CompilerParams(vmem_limit_bytes=VMEM_LIMIT),
    )(score)


def _dsa_decode_kernel(pt_ref, q_ref, kn_ref, vn_ref, m_ref, bias_ref, bnew_ref, *refs):
    k_pages = refs[:N_PAGES]
    v_pages = refs[N_PAGES:2 * N_PAGES]
    o_ref = refs[2 * N_PAGES]
    mask = m_ref[0]
    new_add = bnew_ref[:, 0:1] + mask[:, PAST_LEN:PAST_LEN + 1]
    o_full = _paged_attention(q_ref[0], k_pages, v_pages, bias_ref,
                              lambda p: mask[:, p * PAGE_SIZE:(p + 1) * PAGE_SIZE], kn_ref[0], vn_ref[0], new_add)
    o_ref[0] = _group_lanes(o_full, DSA_GROUP)


def dsa_attn_decode(page_table, qblk, k_new, v_new, mask, bias, bnew, cache_k, cache_v, layer):
    n_req = qblk.shape[0]
    in_specs = ([_row_spec(DSA_HEADS, DSA_KW), _row_spec(1, DSA_KW), _row_spec(1, DSA_KW), _row_spec(1, KEYS_PAD),
                 _const_spec(DSA_HEADS, PAST_LEN), _const_spec(DSA_HEADS, LANES)]
                + _page_specs(layer, DSA_KW) * 2)
    args = [qblk, k_new, v_new, mask, bias, bnew] + [cache_k] * N_PAGES + [cache_v] * N_PAGES
    return _decode_call(_dsa_decode_kernel, n_req, page_table, in_specs, args,
                        jax.ShapeDtypeStruct((n_req, DSA_HEADS, HEAD_DIM), F32), _row_spec(DSA_HEADS, HEAD_DIM))


EVEN_WP = 2432
EVEN_MODES = ("norm",) * 4 + ("raw",) * 2 + ("norm",) * 6 + ("raw", "raw", "norm", "raw", "norm", "raw", "sigmoid")
EVEN_CACHE_OUTS = (("cols", 256, 256), ("cols", 512, 256), ("cols", 1536, 128), ("cols", 1664, 128),
                   ("cols", 1792, 128), ("cols", 1920, 128), ("cols", 2048, 128), ("cols", 2176, 128))
EVEN_OUTS_PROMPT = EVEN_CACHE_OUTS + (
    ("cols", 2304, 128), ("heads", 0, 4, F32, 1.0), ("heads", 256, 4, F32, 1.0), ("heads", 512, 4, BF16, 1.0),
    ("heads", 768, 12, F32, 1.0), ("heads", 1792, 2, BF16, 1.0), ("heads", 1920, 2, BF16, 1.0),
    ("heads", 2048, 2, BF16, 1.0), ("heads", 2176, 2, BF16, 1.0))
EVEN_OUTS_DECODE = EVEN_CACHE_OUTS + (("cols", 2304, 128), ("cols", 0, 256), ("cols", 768, 768))
ODD_WP = 2304
ODD_MODES = ("norm",) * 10 + ("raw",) * 8
ODD_CACHE_OUTS = (("cols", 1024, 256), ("cols", 1280, 256), ("cols", 2048, 128))
ODD_OUTS_PROMPT = ODD_CACHE_OUTS + (
    ("cols", 2176, 128), ("heads", 0, 16, BF16, QK_SCALE), ("heads", 1024, 4, BF16, 1.0),
    ("heads", 1280, 4, BF16, 1.0), ("heads", 1536, 8, F32, 1.0))
ODD_OUTS_DECODE = ODD_CACHE_OUTS + (("cols", 2176, 128), ("cols", 0, 1024), ("cols", 1536, 512))


def _even_weights(w_in, moba_g, nsa_g):
    w = jnp.pad(w_in, ((0, 0), (0, EVEN_WP - w_in.shape[1]))).astype(BF16)
    z = lambda n: jnp.zeros((n,), F32)
    gain = jnp.concatenate([jnp.tile(moba_g[0], 4), jnp.tile(moba_g[1], 4), z(256), jnp.tile(nsa_g[0], 12), z(256),
                            jnp.tile(nsa_g[2], 2), z(128), jnp.tile(nsa_g[3], 2), z(256)])
    return w, gain.reshape(1, EVEN_WP)


def _odd_weights(w_in, qk_g):
    zc = lambda n: jnp.zeros((D_MODEL, n), F32)
    w = jnp.concatenate([w_in[:, :2112], zc(64), w_in[:, 2112:], zc(ODD_WP - 2176 - IDX_HEADS)], axis=1).astype(BF16)
    gain = jnp.concatenate([jnp.tile(qk_g[0], 16), jnp.tile(qk_g[1], 4), jnp.zeros((ODD_WP - 1280,), F32)])
    return w, gain.reshape(1, ODD_WP)


def _block_queries(q, n_heads, heads_per_group, rows):
    b = q.shape[0]
    n_groups = n_heads // heads_per_group
    qh = q.reshape(b, n_heads, 1, HEAD_DIM)
    grp = (jnp.arange(n_heads)[:, None] // heads_per_group == jnp.arange(n_groups)[None, :]).astype(F32)
    blk = (qh * grp[None, :, :, None]).reshape(b, n_heads, n_groups * HEAD_DIM)
    return jnp.pad(blk, ((0, 0), (0, rows - n_heads), (0, 0)))


def _decode_heads(o, n_heads):
    return o[:, :n_heads].transpose(1, 0, 2)[None]


def kernel(x_prompt, x_sample, cache_moba_k, cache_moba_v, cache_nsa_cmp_k, cache_nsa_cmp_v, cache_nsa_slc_k,
           cache_nsa_slc_v, state_nsa_win_k, state_nsa_win_v, cache_dsa_k, cache_dsa_v, cache_dsa_idx_k, page_table,
           c_prompt, c_sample, bias_table, norm_gain, ada_w, ada_b, ffn_w_in, ffn_w_out, even_w_in, even_w_out,
           moba_qk_gain, nsa_qk_gain, nsa_cmp_pe, nsa_cmp_w1, nsa_cmp_w2, odd_w_in, odd_w_out, dsa_qk_gain):
    bp, t, _ = x_prompt.shape
    bs = x_sample.shape[0]
    n_pool = cache_moba_k.shape[0]

    tab_m, tab_n = bias_table[:, :MOBA_HEADS], bias_table[:, MOBA_HEADS:]
    bias_m = _bias_tiles(tab_m, MOBA_BLOCK)[:, :3]
    bias_n = _bias_tiles(tab_n, TQ)
    bias_d = _bias_tiles(bias_table, TQ)
    cmp_end = jnp.arange(LANES) * CMP_STRIDE + CMP_LEN - 1
    cbias = _rel_bias(jnp.arange(t)[:, None] - cmp_end[None, :], tab_n)
    pad_rows = lambda a, rows: jnp.pad(a, ((0, rows - a.shape[0]), (0, 0)))
    past_dist = PAST_LEN - jnp.arange(PAST_LEN)
    dec_bias_m = pad_rows(_rel_bias(past_dist, tab_m), MOBA_HP)
    dec_bias_n = pad_rows(_rel_bias(past_dist, tab_n), NSA_HP)
    dec_bias_d = _rel_bias(past_dist, bias_table)
    dec_cbias = pad_rows(_rel_bias(PAST_LEN - cmp_end, tab_n), NSA_HP)
    win_dist = WINDOW - jnp.arange(WINDOW)
    dec_wbias = pad_rows(jnp.where(win_dist < WINDOW, _rel_bias(win_dist, tab_n), NEG_INF), NSA_HP)
    new_bias = jnp.broadcast_to(bias_table[0][:, None], (DSA_HEADS, LANES))
    bnew_m = pad_rows(new_bias[:MOBA_HEADS], MOBA_HP)
    bnew_n = pad_rows(new_bias[MOBA_HEADS:], NSA_HP)

    flat = lambda c: c.reshape(n_pool, PAGE_SIZE, -1)
    pool_mk, pool_mv = flat(cache_moba_k), flat(cache_moba_v)
    pool_ck, pool_cv = flat(cache_nsa_cmp_k), flat(cache_nsa_cmp_v)
    pool_sk, pool_sv = flat(cache_nsa_slc_k), flat(cache_nsa_slc_v)
    pool_dk, pool_dv, pool_di = flat(cache_dsa_k), flat(cache_dsa_v), flat(cache_dsa_idx_k)
    win_k_all = state_nsa_win_k.reshape(bs, WINDOW, -1)
    win_v_all = state_nsa_win_v.reshape(bs, WINDOW, -1)

    mods = adaln_all(jnp.concatenate([c_prompt, c_sample], axis=0), ada_w, ada_b)
    mods = mods.reshape(DEPTH, bp + bs, 3, 3, D_MODEL)
    ffn_in = ffn_w_in.astype(BF16)
    ffn_out = ffn_w_out.astype(BF16)

    yp = x_prompt
    ys = x_sample.reshape(1, bs, D_MODEL)
    rows_ep = [[] for _ in range(8)]
    rows_es = [[] for _ in range(8)]
    rows_op = [[] for _ in range(3)]
    rows_os = [[] for _ in range(3)]
    for li in range(DEPTH):
        mp = mods[li, :bp][:, None]
        ms = mods[li, bp:][None]
        mod = lambda m, s, k: m[:, :, s, k]

        def ffn(y, m, s, w_idx):
            return ffn_half(y, mod(m, s, 0), mod(m, s, 1), mod(m, s, 2), norm_gain[li, s], ffn_in[li, w_idx],
                            ffn_out[li, w_idx])

        yp, ys = ffn(yp, mp, 0, 0), ffn(ys, ms, 0, 0)
        if li % 2 == 0:
            e = li // 2
            w_p, gain_cols = _even_weights(even_w_in[e], moba_qk_gain[e], nsa_qk_gain[e])
            w_o = even_w_out[e].astype(BF16)
            cmp_w = _compress_weights(nsa_cmp_pe[e], nsa_cmp_w1[e], nsa_cmp_w2[e])
            cmp_gain = jnp.tile(nsa_qk_gain[e, 1], NSA_KV).reshape(1, NSA_KW)
            project = lambda y, m, outs: mixer_project(y, mod(m, 1, 0), mod(m, 1, 1), norm_gain[li, 1], w_p,
                                                       gain_cols, EVEN_MODES, outs)
            pp = project(yp, mp, EVEN_OUTS_PROMPT)
            gl, mq_h, mk_h, mv_h, nq_h, sk_h, sv_h, wk_h, wv_h = pp[8:]
            o_m = moba_prompt(mq_h, mk_h, mv_h, bias_m)
            ck, cv = nsa_compress_prompt(pp[2], pp[3], *cmp_w, cmp_gain)
            o_n = nsa_prompt(nq_h, gl, ck, cv, sk_h, sv_h, wk_h, wv_h, bias_n, cbias)
            pieces_p = [o_m, o_n]
            wb = min(WINDOW, t)
            new_p = list(pp[:6]) + [pp[6][:, t - wb:], pp[7][:, t - wb:]]
            ps = [a.reshape(bs, 1, a.shape[-1]) for a in project(ys, ms, EVEN_OUTS_DECODE)]
            mk_s, mv_s, ckr_s, cvr_s, sk_s, sv_s, wk_s, wv_s, gl_s, mq_s, nq_s = ps
            o_ms = moba_decode(page_table, _block_queries(mq_s[:, 0], MOBA_HEADS, 1, MOBA_HP), mk_s, mv_s,
                               dec_bias_m, bnew_m, pool_mk, pool_mv, e)
            gates_s = jnp.pad(gl_s[:, 0, :3 * NSA_HEADS].reshape(bs, NSA_HEADS, 3),
                              ((0, 0), (0, NSA_HP - NSA_HEADS), (0, LANES - 3)))
            o_ns = nsa_decode(page_table, _block_queries(nq_s[:, 0], NSA_HEADS, NSA_GROUP, NSA_HP), gates_s,
                              sk_s, sv_s, wk_s, wv_s, win_k_all, win_v_all, cmp_w, cmp_gain,
                              dec_cbias, dec_bias_n, dec_wbias, bnew_n, pool_ck, pool_cv, pool_sk, pool_sv, e)
            pieces_s = [_decode_heads(o_ms, MOBA_HEADS), _decode_heads(o_ns, NSA_HEADS)]
            keep = min(WINDOW, PAST_LEN + 1)
            lanes_e = slice(e * NSA_KW, (e + 1) * NSA_KW)
            new_s = [mk_s, mv_s, ckr_s, cvr_s, sk_s, sv_s,
                     jnp.concatenate([win_k_all[:, :, lanes_e], wk_s], axis=1)[:, -keep:],
                     jnp.concatenate([win_v_all[:, :, lanes_e], wv_s], axis=1)[:, -keep:]]
            for acc, r in zip(rows_ep, new_p):
                acc.append(r)
            for acc, r in zip(rows_es, new_s):
                acc.append(r)
        else:
            o = li // 2
            w_p, gain_cols = _odd_weights(odd_w_in[o], dsa_qk_gain[o])
            w_o = odd_w_out[o].astype(BF16)
            project = lambda y, m, outs: mixer_project(y, mod(m, 1, 0), mod(m, 1, 1), norm_gain[li, 1], w_p,
                                                       gain_cols, ODD_MODES, outs)
            pp = project(yp, mp, ODD_OUTS_PROMPT)
            wi, q_h, k_h, v_h, qi_h = pp[3:]
            keep_mask = dsa_index_prompt(qi_h, wi, pp[2])
            pieces_p = [dsa_attn_prompt(q_h, k_h, v_h, bias_d, keep_mask)]
            ps = [a.reshape(bs, 1, a.shape[-1]) for a in project(ys, ms, ODD_OUTS_DECODE)]
            k_s, v_s, ki_s, wi_s, q_s, qi_s = ps
            idx_lanes = lambda a: jnp.pad(a, ((0, 0), (0, 0), (o * IDX_DIM, LANES - (o + 1) * IDX_DIM)))
            qi_blk = idx_lanes(qi_s[:, 0].reshape(bs, IDX_HEADS, IDX_DIM))
            wi_rows = jnp.broadcast_to(wi_s[:, 0, :IDX_HEADS, None], (bs, IDX_HEADS, LANES))
            score = dsa_score_decode(page_table, qi_blk, wi_rows, idx_lanes(ki_s[:, :, :IDX_DIM]), pool_di)
            keep_s = dsa_topk_decode(score.reshape(bs, KEYS_PAD)).reshape(bs, 1, KEYS_PAD)
            o_ds = dsa_attn_decode(page_table, _block_queries(q_s[:, 0], DSA_HEADS, DSA_GROUP, DSA_HEADS), k_s, v_s,
                                   keep_s, dec_bias_d, new_bias, pool_dk, pool_dv, o)
            pieces_s = [_decode_heads(o_ds, DSA_HEADS)]
            for acc, r in zip(rows_op, (pp[0], pp[1], pp[2][..., :IDX_DIM])):
                acc.append(r)
            for acc, r in zip(rows_os, (k_s, v_s, ki_s[..., :IDX_DIM])):
                acc.append(r)
        yp = mixer_merge(yp, pieces_p, mod(mp, 1, 2), w_o)
        ys = mixer_merge(ys, pieces_s, mod(ms, 1, 2), w_o)
        yp, ys = ffn(yp, mp, 2, 1), ffn(ys, ms, 2, 1)

    def stack_layers(rows, n_heads):
        return jnp.stack([r.reshape(r.shape[0], r.shape[1], n_heads, HEAD_DIM) for r in rows], axis=2)

    even_heads = (MOBA_HEADS, MOBA_HEADS) + (NSA_KV,) * 6
    moba_k_p, moba_v_p, cmp_k_p, cmp_v_p, slc_k_p, slc_v_p, win_k_p, win_v_p = [
        stack_layers(r, n) for r, n in zip(rows_ep, even_heads)]
    moba_k_s, moba_v_s, cmp_k_s, cmp_v_s, slc_k_s, slc_v_s, win_k_s, win_v_s = [
        stack_layers(r, n) for r, n in zip(rows_es, even_heads)]
    dsa_k_p, dsa_v_p = stack_layers(rows_op[0], DSA_KV), stack_layers(rows_op[1], DSA_KV)
    dsa_k_s, dsa_v_s = stack_layers(rows_os[0], DSA_KV), stack_layers(rows_os[1], DSA_KV)
    dsa_idx_k_p, dsa_idx_k_s = jnp.stack(rows_op[2], axis=2), jnp.stack(rows_os[2], axis=2)
    return (yp, ys.reshape(bs, 1, D_MODEL), moba_k_p, moba_k_s, moba_v_p, moba_v_s, cmp_k_p, cmp_k_s, cmp_v_p,
            cmp_v_s, slc_k_p, slc_k_s, slc_v_p, slc_v_s, win_k_p, win_k_s, win_v_p, win_v_s,
            dsa_k_p, dsa_k_s, dsa_v_p, dsa_v_s, dsa_idx_k_p, dsa_idx_k_s)
```

```python
import functools
import math

import jax
import jax.numpy as jnp
from jax import lax
from jax.experimental import pallas as pl
from jax.experimental.pallas import tpu as pltpu

D_MODEL = 1024
DEPTH = 4
PAST_LEN = 2048
PAGE_SIZE = 128
N_PAGES = PAST_LEN // PAGE_SIZE
HEAD_DIM = 64
MOBA_HEADS = 4
NSA_HEADS = 12
NSA_KV = 2
NSA_GROUP = 6
DSA_HEADS = 16
DSA_KV = 4
DSA_GROUP = 4
IDX_HEADS = 8
IDX_DIM = 64
MOBA_BLOCK = 256
MOBA_TOPK = 3
CMP_LEN = 32
CMP_STRIDE = 16
CMP_HIDDEN = 128
SEL_BLOCK = 64
N_SEL = 8
WINDOW = 512
DSA_TOPK = 256
N_BUCKETS = 32
MAX_DISTANCE = 128
D_FF = 2816
MOBA_W = MOBA_HEADS * HEAD_DIM
NSA_W = NSA_HEADS * HEAD_DIM
NSA_KW = NSA_KV * HEAD_DIM
DSA_W = DSA_HEADS * HEAD_DIM
DSA_KW = DSA_KV * HEAD_DIM
RMS_EPS = 1e-6
NEG_INF = -1e30
TINY = 1e-30
SEL_FORCE = 1e4
QK_SCALE = HEAD_DIM ** -0.5

LANES = 128
VMEM_LIMIT = 56 * 1024 * 1024
BF16 = jnp.bfloat16
F32 = jnp.float32
HIGHEST = lax.Precision.HIGHEST
NT_DIMS = (((1,), (1,)), ((), ()))
M_INIT = -3e38
INT_MIN = -2 ** 31


def _cparams(*sem):
    return pltpu.CompilerParams(dimension_semantics=sem, vmem_limit_bytes=VMEM_LIMIT)


def _t5_bucket(dist):
    n = jnp.maximum(dist, 0)
    exact = N_BUCKETS // 2
    nf = jnp.maximum(n, 1).astype(F32)
    large = exact + (jnp.log(nf / exact) / math.log(MAX_DISTANCE / exact) * (N_BUCKETS - exact)).astype(jnp.int32)
    return jnp.where(n < exact, n, jnp.minimum(large, N_BUCKETS - 1))


def _rel_bias(dist, table):
    onehot = jax.nn.one_hot(_t5_bucket(dist), N_BUCKETS, dtype=F32)
    return jnp.einsum("...k,kh->h...", onehot, table, precision=HIGHEST)


def _adaln_kernel(c_ref, w_ref, b_ref, o_ref):
    c = c_ref[...]
    s = (c * jax.nn.sigmoid(c)).astype(BF16)
    o_ref[0] = jnp.dot(s, w_ref[0].astype(BF16), preferred_element_type=F32) + b_ref[0]


def adaln_all(c_all, ada_w, ada_b):
    r = c_all.shape[0]
    n_out = ada_w.shape[2]
    tn = 1024
    return pl.pallas_call(
        _adaln_kernel,
        grid=(DEPTH, n_out // tn),
        in_specs=[
            pl.BlockSpec((r, D_MODEL), lambda l, j: (0, 0)),
            pl.BlockSpec((1, D_MODEL, tn), lambda l, j: (l, 0, j)),
            pl.BlockSpec((1, 1, tn), lambda l, j: (l, 0, j)),
        ],
        out_specs=pl.BlockSpec((1, r, tn), lambda l, j: (l, 0, j)),
        out_shape=jax.ShapeDtypeStruct((DEPTH, r, n_out), F32),
        compiler_params=_cparams("parallel", "parallel"),
    )(c_all, ada_w, ada_b.reshape(DEPTH, 1, n_out))


def _modnorm(x, g, scale, shift):
    y = x * lax.rsqrt(jnp.mean(x * x, axis=-1, keepdims=True) + RMS_EPS)
    return (y * g) * (1.0 + scale) + shift


def _mod_spec(tm_rows, t_mod):
    if t_mod == 1:
        return pl.BlockSpec((1, 1, D_MODEL), lambda b, i, *_: (b, 0, 0))
    return pl.BlockSpec((1, tm_rows, D_MODEL), lambda b, i, *_: (b, i, 0))


def _seg_ones():
    return jnp.kron(jnp.eye(LANES // HEAD_DIM, dtype=F32), jnp.ones((HEAD_DIM, HEAD_DIM), F32))


def _head_rms(z, seg_ref, gain):
    ss = jnp.dot(z * z, seg_ref[...], preferred_element_type=F32, precision=HIGHEST)
    return z * lax.rsqrt(ss * (1.0 / HEAD_DIM) + RMS_EPS) * gain


def _ffn_kernel(x_ref, sh_ref, sc_ref, gt_ref, g_ref, wa_ref, wg_ref, wo_ref, o_ref, xn_ref, acc_ref):
    j = pl.program_id(2)

    @pl.when(j == 0)
    def _():
        xn_ref[...] = _modnorm(x_ref[0], g_ref[...], sc_ref[0], sh_ref[0]).astype(BF16)
        acc_ref[...] = jnp.zeros_like(acc_ref)

    xn = xn_ref[...]
    a = jnp.dot(xn, wa_ref[...], preferred_element_type=F32)
    g = jnp.dot(xn, wg_ref[...], preferred_element_type=F32)
    h = ((g * jax.nn.sigmoid(g)) * a).astype(BF16)
    acc_ref[...] += jnp.dot(h, wo_ref[...], preferred_element_type=F32)

    @pl.when(j == pl.num_programs(2) - 1)
    def _():
        o_ref[0] = x_ref[0] + (0.5 * gt_ref[0]) * acc_ref[...]


def ffn_half(x, shift, scale, gate, g, w_in, w_out):
    b, t, _ = x.shape
    tm = min(t, 512)
    fc = D_FF // 2
    nf = D_FF // fc
    mspec = _mod_spec(tm, shift.shape[1])
    return pl.pallas_call(
        _ffn_kernel,
        grid=(b, t // tm, nf),
        in_specs=[
            pl.BlockSpec((1, tm, D_MODEL), lambda bb, i, j: (bb, i, 0)),
            mspec, mspec, mspec,
            pl.BlockSpec((1, D_MODEL), lambda bb, i, j: (0, 0)),
            pl.BlockSpec((D_MODEL, fc), lambda bb, i, j: (0, j)),
            pl.BlockSpec((D_MODEL, fc), lambda bb, i, j: (0, j + nf)),
            pl.BlockSpec((fc, D_MODEL), lambda bb, i, j: (j, 0)),
        ],
        out_specs=pl.BlockSpec((1, tm, D_MODEL), lambda bb, i, j: (bb, i, 0)),
        out_shape=jax.ShapeDtypeStruct(x.shape, F32),
        scratch_shapes=[pltpu.VMEM((tm, D_MODEL), BF16), pltpu.VMEM((tm, D_MODEL), F32)],
        compiler_params=_cparams("parallel", "parallel", "arbitrary"),
    )(x, shift, scale, gate, g.reshape(1, D_MODEL), w_in, w_in, w_out)


def _proj_kernel(modes, outs, x_ref, sh_ref, sc_ref, g_ref, w_ref, gain_ref, seg_ref, *o_refs):
    h = _modnorm(x_ref[0], g_ref[...], sc_ref[0], sh_ref[0])
    z = jnp.dot(h.astype(BF16), w_ref[...], preferred_element_type=F32)
    chunks = []
    for c, mode in enumerate(modes):
        zc = z[:, c * LANES:(c + 1) * LANES]
        if mode == "norm":
            zc = _head_rms(zc, seg_ref, gain_ref[:, c * LANES:(c + 1) * LANES])
        elif mode == "sigmoid":
            zc = jax.nn.sigmoid(zc)
        chunks.append(zc)
    for o_ref, out in zip(o_refs, outs):
        if out[0] == "cols":
            _, start, width = out
            for c in range(width // LANES):
                o_ref[0, :, c * LANES:(c + 1) * LANES] = chunks[start // LANES + c]
        else:
            _, start, n_heads, dtype, scale = out
            for hh in range(n_heads):
                lo = start + hh * HEAD_DIM
                piece = chunks[lo // LANES][:, lo % LANES:lo % LANES + HEAD_DIM]
                o_ref[0, hh] = (piece * scale).astype(dtype)


def mixer_project(x, shift, scale, g, w, gain_cols, modes, outs):
    b, t, _ = x.shape
    tm = min(t, 512)
    wp = w.shape[1]
    mspec = _mod_spec(tm, shift.shape[1])
    out_shapes, out_specs = [], []
    for out in outs:
        if out[0] == "cols":
            out_shapes.append(jax.ShapeDtypeStruct((b, t, out[2]), F32))
            out_specs.append(pl.BlockSpec((1, tm, out[2]), lambda bb, i: (bb, i, 0)))
        else:
            out_shapes.append(jax.ShapeDtypeStruct((b, out[2], t, HEAD_DIM), out[3]))
            out_specs.append(pl.BlockSpec((1, out[2], tm, HEAD_DIM), lambda bb, i: (bb, 0, i, 0)))
    return pl.pallas_call(
        functools.partial(_proj_kernel, modes, outs),
        grid=(b, t // tm),
        in_specs=[
            pl.BlockSpec((1, tm, D_MODEL), lambda bb, i: (bb, i, 0)),
            mspec, mspec,
            pl.BlockSpec((1, D_MODEL), lambda bb, i: (0, 0)),
            pl.BlockSpec((D_MODEL, wp), lambda bb, i: (0, 0)),
            pl.BlockSpec((1, wp), lambda bb, i: (0, 0)),
            pl.BlockSpec((LANES, LANES), lambda bb, i: (0, 0)),
        ],
        out_specs=out_specs,
        out_shape=out_shapes,
        compiler_params=_cparams("parallel", "parallel"),
    )(x, shift, scale, g.reshape(1, D_MODEL), w, gain_cols, _seg_ones())


def _merge_kernel(n_pieces, y_ref, gt_ref, w_ref, *refs):
    out_ref = refs[n_pieces]
    heads = [refs[k][0, hh] for k in range(n_pieces) for hh in range(refs[k].shape[1])]
    o = jnp.concatenate(heads, axis=1).astype(BF16)
    out_ref[0] = y_ref[0] + gt_ref[0] * jnp.dot(o, w_ref[...], preferred_element_type=F32)


def mixer_merge(y, pieces, gate, w_out):
    b, t, _ = y.shape
    tm = min(t, 512)
    piece_specs = [pl.BlockSpec((1, p.shape[1], tm, HEAD_DIM), lambda bb, i: (bb, 0, i, 0)) for p in pieces]
    return pl.pallas_call(
        functools.partial(_merge_kernel, len(pieces)),
        grid=(b, t // tm),
        in_specs=[
            pl.BlockSpec((1, tm, D_MODEL), lambda bb, i: (bb, i, 0)),
            _mod_spec(tm, gate.shape[1]),
            pl.BlockSpec((D_MODEL, D_MODEL), lambda bb, i: (0, 0)),
        ] + piece_specs,
        out_specs=pl.BlockSpec((1, tm, D_MODEL), lambda bb, i: (bb, i, 0)),
        out_shape=jax.ShapeDtypeStruct(y.shape, F32),
        compiler_params=_cparams("parallel", "parallel"),
    )(y, gate, w_out, *pieces)


TQ = 128
CK = 2 * TQ


def _bias_tiles(table, t):
    d = jnp.arange(t)[:, None] - jnp.arange(t)[None, :]
    diag = jnp.where(d >= 0, _rel_bias(d, table), NEG_INF)
    adj = _rel_bias(t + d, table)
    far = _rel_bias(2 * t + d, table)
    masked = jnp.full_like(far, NEG_INF)
    edge = jnp.where(d < 0, far, NEG_INF)
    return jnp.stack([diag, adj, far, masked, edge], axis=1)


def _tile_kind(d):
    return jnp.where(d < 0, 3, jnp.minimum(d, 2))


def _loop(n, body, init, static):
    if static:
        for c in range(n):
            init = body(c, init)
        return init
    return lax.fori_loop(0, n, body, init)


def _attend(qb, n_chunks, k_of, v_of, add_of, s_ref, static=False):
    rows = qb.shape[0]

    def cols(c, w):
        return pl.ds(c * w if static else pl.multiple_of(c * w, w), w)

    def score(c, mx):
        s = lax.dot_general(qb, k_of(c), NT_DIMS, preferred_element_type=F32) + add_of(c)
        w = s.shape[1]
        s_ref[:, cols(c, w)] = s
        for part in range(w // LANES):
            mx = jnp.maximum(mx, s[:, part * LANES:(part + 1) * LANES])
        return mx

    mx = _loop(n_chunks, score, jnp.full((rows, LANES), M_INIT, F32), static)
    m = jnp.max(mx, axis=1, keepdims=True)

    def accumulate(c, acc):
        v = v_of(c)
        w = v.shape[0]
        p = jnp.exp(s_ref[:, cols(c, w)] - m).astype(BF16)
        return acc + jnp.dot(p, jnp.concatenate([v, jnp.ones_like(v)], axis=1), preferred_element_type=F32)

    acc = _loop(n_chunks, accumulate, jnp.zeros((rows, 2 * HEAD_DIM), F32), static)
    return acc[:, :HEAD_DIM] / jnp.maximum(acc[:, HEAD_DIM:HEAD_DIM + 1], TINY)


def _top_k_rows(score, allowed, k):
    lane = lax.broadcasted_iota(jnp.int32, score.shape, 1)
    n = score.shape[1]
    remaining = allowed
    picked = jnp.zeros(score.shape, jnp.bool_)
    for _ in range(k):
        cur = jnp.where(remaining, score, -jnp.inf)
        best = jnp.max(cur, axis=1, keepdims=True)
        cand = jnp.logical_and(remaining, cur == best)
        first = jnp.min(jnp.where(cand, lane, n), axis=1, keepdims=True)
        hit = lane == first
        picked = jnp.logical_or(picked, hit)
        remaining = jnp.logical_and(remaining, jnp.logical_not(hit))
    return picked


def _count(mask):
    return jnp.sum(jnp.where(mask, 1.0, 0.0), axis=1, keepdims=True)


def _top_k_mask(score, valid, pos, k):
    bits = pltpu.bitcast(score, jnp.int32)
    key = jnp.where(bits < 0, bits ^ jnp.int32(0x7FFFFFFF), bits)
    key = jnp.where(valid, key, jnp.int32(INT_MIN))
    kf = float(k)

    thr = jnp.full((score.shape[0], 1), INT_MIN, jnp.int32)
    cand = thr ^ jnp.int32(INT_MIN)
    thr = jnp.where(_count(key >= cand) >= kf, cand, thr)

    def step(b, thr):
        cand = thr | lax.shift_left(jnp.int32(1), 30 - b)
        return jnp.where(_count(key >= cand) >= kf, cand, thr)

    thr = lax.fori_loop(0, 31, step, thr)
    above = key > thr
    tie = jnp.logical_and(key == thr, valid)
    need = kf - _count(above)

    n_bits = max(int(score.shape[1] - 1).bit_length(), 1)
    all_pos = jnp.full((score.shape[0], 1), 2 ** n_bits - 1, jnp.int32)

    def tie_search():
        def tie_step(b, cut):
            cand = cut | lax.shift_left(jnp.int32(1), n_bits - 1 - b)
            return jnp.where(_count(jnp.logical_and(tie, pos < cand)) < need, cand, cut)

        return lax.fori_loop(0, n_bits, tie_step, jnp.zeros_like(all_pos))

    surplus = jnp.max(jnp.where(_count(tie) > need, 1.0, 0.0))
    cut = lax.cond(surplus > 0.0, tie_search, lambda: all_pos)
    keep = jnp.logical_or(above, jnp.logical_and(tie, pos <= cut))
    return jnp.logical_and(keep, valid)


def _moba_kernel(q_ref, k_ref, v_ref, bias_ref, o_ref, km_ref, s_ref):
    i = pl.program_id(2)
    nb = km_ref.shape[0]
    t = MOBA_BLOCK

    @pl.when(i == 0)
    def _():
        km_ref[...] = jnp.mean(k_ref[0, 0].reshape(nb, t, HEAD_DIM), axis=1)

    q = q_ref[0, 0]
    gate = lax.dot_general(q.astype(BF16), km_ref[...].astype(BF16), NT_DIMS, preferred_element_type=F32)
    blk = lax.broadcasted_iota(jnp.int32, gate.shape, 1)
    chosen = jnp.logical_or(_top_k_rows(gate, blk < i, MOBA_TOPK), blk == i)
    blk_add = jnp.where(chosen, 0.0, NEG_INF)

    def rows_of(j):
        return pl.ds(pl.multiple_of(j * t, t), t)

    def add_of(j):
        col = jnp.sum(jnp.where(blk == j, blk_add, 0.0), axis=1, keepdims=True)
        return bias_ref[0, jnp.minimum(i - j, 2)] + col

    o_ref[0, 0] = _attend((q * QK_SCALE).astype(BF16), i + 1, lambda j: k_ref[0, 0, rows_of(j), :].astype(BF16),
                          lambda j: v_ref[0, 0, rows_of(j), :], add_of, s_ref)


def moba_prompt(q, k, v, bias):
    b, h, t, _ = q.shape
    tb = MOBA_BLOCK
    nb = t // tb
    full = pl.BlockSpec((1, 1, t, HEAD_DIM), lambda bb, hh, i: (bb, hh, 0, 0))
    return pl.pallas_call(
        _moba_kernel,
        grid=(b, h, nb),
        in_specs=[
            pl.BlockSpec((1, 1, tb, HEAD_DIM), lambda bb, hh, i: (bb, hh, i, 0)),
            full, full,
            pl.BlockSpec((1, 3, tb, tb), lambda bb, hh, i: (hh, 0, 0, 0)),
        ],
        out_specs=pl.BlockSpec((1, 1, tb, HEAD_DIM), lambda bb, hh, i: (bb, hh, i, 0)),
        out_shape=jax.ShapeDtypeStruct(q.shape, F32),
        scratch_shapes=[pltpu.VMEM((nb, HEAD_DIM), F32), pltpu.VMEM((tb, t), F32)],
        compiler_params=_cparams("parallel", "parallel", "arbitrary"),
    )(q, k, v, bias)


N_CHUNK = PAST_LEN // CMP_STRIDE
CHUNK_W = CMP_STRIDE * NSA_KW
N_CMP = (PAST_LEN - CMP_LEN) // CMP_STRIDE + 1
HALF_W = NSA_KV * CMP_HIDDEN


def _compress_weights(pe, w1, w2):
    eye = jnp.eye(NSA_KV, dtype=F32)
    pe_x = jnp.broadcast_to(pe.reshape(2, 2, CMP_STRIDE, 1, HEAD_DIM),
                            (2, 2, CMP_STRIDE, NSA_KV, HEAD_DIM)).reshape(2, 2, CHUNK_W)
    w1_h = w1.reshape(2, 2, CMP_STRIDE, HEAD_DIM, CMP_HIDDEN)
    w1_x = jnp.einsum("thrdj,gq->trgdhqj", w1_h, eye).reshape(2, CHUNK_W, 2 * HALF_W)
    w2_x = jnp.einsum("tjd,gq->tgjqd", w2, eye).reshape(2, HALF_W, NSA_KW)
    return pe_x, w1_x.astype(BF16), w2_x.astype(BF16)


def _compress_tail(ha, hb, w2):
    hid = jax.nn.gelu(ha + pltpu.roll(hb, N_CHUNK - 1, 0))
    return jnp.dot(hid.astype(BF16), w2, preferred_element_type=F32)


def _compress_one(r, pe_ref, w1_ref, w2_ref, t):
    ha = jnp.dot((r + pe_ref[t, 0:1]).astype(BF16), w1_ref[t, :, :HALF_W], preferred_element_type=F32)
    hb = jnp.dot((r + pe_ref[t, 1:2]).astype(BF16), w1_ref[t, :, HALF_W:], preferred_element_type=F32)
    return _compress_tail(ha, hb, w2_ref[t])


def _compress_kernel(rk_ref, rv_ref, pe_ref, w1_ref, w2_ref, gain_ref, seg_ref, ck_ref, cv_ref):
    ck_ref[0] = _head_rms(_compress_one(rk_ref[0], pe_ref, w1_ref, w2_ref, 0), seg_ref, gain_ref[...])
    cv_ref[0] = _compress_one(rv_ref[0], pe_ref, w1_ref, w2_ref, 1)


def nsa_compress_prompt(ck_raw, cv_raw, pe_x, w1_x, w2_x, gain_row):
    b = ck_raw.shape[0]
    rows = pl.BlockSpec((1, N_CHUNK, CHUNK_W), lambda bb: (bb, 0, 0))
    out = pl.BlockSpec((1, N_CHUNK, NSA_KW), lambda bb: (bb, 0, 0))
    const = lambda *shape: pl.BlockSpec(shape, lambda bb: (0,) * len(shape))
    return pl.pallas_call(
        _compress_kernel,
        grid=(b,),
        in_specs=[rows, rows, const(2, 2, CHUNK_W), const(2, CHUNK_W, 2 * HALF_W),
                  const(2, HALF_W, NSA_KW), const(1, NSA_KW), const(LANES, LANES)],
        out_specs=[out, out],
        out_shape=[jax.ShapeDtypeStruct((b, N_CHUNK, NSA_KW), F32)] * 2,
        compiler_params=_cparams("parallel"),
    )(ck_raw.reshape(b, N_CHUNK, CHUNK_W), cv_raw.reshape(b, N_CHUNK, CHUNK_W), pe_x, w1_x, w2_x, gain_row,
      _seg_ones())


def _overlap_matrix():
    c = jnp.arange(LANES)[:, None]
    n = jnp.arange(LANES)[None, :]
    return jnp.logical_and(c * CMP_STRIDE < n * SEL_BLOCK + SEL_BLOCK,
                           c * CMP_STRIDE + CMP_LEN > n * SEL_BLOCK).astype(F32)


def _block_expand(n_keys):
    return (jnp.arange(n_keys)[None, :] // SEL_BLOCK == jnp.arange(LANES)[:, None]).astype(BF16)


def _nsa_kernel(q_ref, gt_ref, ck_ref, cv_ref, sk_ref, sv_ref, wk_ref, wv_ref, bias_ref, cb_ref, ov_ref, ex_ref,
                o_ref, madd_ref, s_ref):
    g = pl.program_id(1)
    i = pl.program_id(2)
    hg = NSA_GROUP
    rows = hg * TQ
    nq = madd_ref.shape[0]
    row = lax.broadcasted_iota(jnp.int32, (TQ, LANES), 0)
    lane = lax.broadcasted_iota(jnp.int32, (TQ, LANES), 1)
    qpos = i * TQ + row
    qf = q_ref[0].reshape(rows, HEAD_DIM)

    def group_half(x):
        return jnp.where(g == 0, x[:, :HEAD_DIM], x[:, HEAD_DIM:])

    cmask = jnp.logical_and(qpos - (lane * CMP_STRIDE + CMP_LEN - 1) >= 0, lane < N_CMP)
    cmask_all = jnp.concatenate([cmask] * hg, axis=0)
    lc = lax.dot_general(qf.astype(BF16), group_half(ck_ref[0]).astype(BF16), NT_DIMS, preferred_element_type=F32)
    lc = jnp.where(cmask_all, lc * QK_SCALE + cb_ref[...].reshape(rows, LANES), NEG_INF)
    e = jnp.where(cmask_all, jnp.exp(lc - jnp.max(lc, axis=1, keepdims=True)), 0.0)
    pc = e / jnp.maximum(jnp.sum(e, axis=1, keepdims=True), TINY)
    o_cmp = jnp.dot(pc.astype(BF16), group_half(cv_ref[0]).astype(BF16), preferred_element_type=F32)
    psum = jnp.sum(pc.reshape(hg, TQ, LANES), axis=0)

    imp = jnp.dot(psum.astype(BF16), ov_ref[...], preferred_element_type=F32)
    cur = qpos // SEL_BLOCK
    forced = jnp.logical_or(lane == 0, jnp.logical_or(lane == cur, lane == cur - 1))
    picked = _top_k_rows(jnp.where(forced, SEL_FORCE, imp), lane <= cur, N_SEL)
    pick_b = jnp.where(picked, 1.0, 0.0).astype(BF16)

    key_add = jnp.where(jnp.dot(pick_b, ex_ref[...], preferred_element_type=F32) > 0.5, 0.0, NEG_INF)
    for j in range(nq):
        madd_ref[j] = key_add[:, j * TQ:(j + 1) * TQ]

    qb = (qf * QK_SCALE).astype(BF16)

    def bias_rows(kind):
        return bias_ref[:, kind].reshape(rows, TQ)

    def key_tile(ref, j):
        return ref[0, 0, pl.ds(pl.multiple_of(j * TQ, TQ), TQ), :]

    def keys_of(c):
        return pl.ds(pl.multiple_of(c * CK, CK), CK)

    def slc_add(c):
        j0 = 2 * c
        mask = jnp.concatenate([madd_ref[j0], madd_ref[j0 + 1]], axis=1)
        add = jnp.concatenate([bias_rows(_tile_kind(i - j0)), bias_rows(_tile_kind(i - j0 - 1))], axis=1)
        return add + jnp.concatenate([mask] * hg, axis=0)

    o_slc = _attend(qb, i // 2 + 1, lambda c: sk_ref[0, 0, keys_of(c), :], lambda c: sv_ref[0, 0, keys_of(c), :],
                    slc_add, s_ref)

    n_win = WINDOW // TQ

    def win_tiles(w):
        return (i - n_win - 1 + 2 * w, i - n_win + 2 * w)

    def win_keys(ref):
        return lambda w: jnp.concatenate([key_tile(ref, jnp.maximum(j, 0)) for j in win_tiles(w)], axis=0)

    def win_add(w):
        kinds = []
        for j in win_tiles(w):
            d = i - j
            kind = jnp.where(d == n_win, 4, jnp.where(d > n_win, 3, jnp.minimum(d, 2)))
            kinds.append(jnp.where(j < 0, 3, kind))
        return jnp.concatenate([bias_rows(kinds[0]), bias_rows(kinds[1])], axis=1)

    o_win = _attend(qb, (n_win + 2) // 2, win_keys(wk_ref), win_keys(wv_ref), win_add, s_ref, static=True)

    gt = gt_ref[0]

    def gate_col(branch):
        cols = []
        for h in range(hg):
            c0 = 3 * h + branch
            cols.append(jnp.where(g == 0, gt[:, c0:c0 + 1], gt[:, 3 * hg + c0:3 * hg + c0 + 1]))
        return jnp.concatenate(cols, axis=0)

    o = gate_col(0) * o_cmp + gate_col(1) * o_slc + gate_col(2) * o_win
    o_ref[0] = o.reshape(hg, TQ, HEAD_DIM)


def nsa_prompt(q, gates, ck, cv, sk, sv, wk, wv, bias, cbias):
    b, _, t, _ = q.shape
    hg = NSA_GROUP
    nq = t // TQ
    kv = pl.BlockSpec((1, 1, t, HEAD_DIM), lambda bb, gg, i: (bb, gg, 0, 0))
    cmp_spec = pl.BlockSpec((1, LANES, NSA_KW), lambda bb, gg, i: (bb, 0, 0))
    return pl.pallas_call(
        _nsa_kernel,
        grid=(b, NSA_KV, nq),
        in_specs=[
            pl.BlockSpec((1, hg, TQ, HEAD_DIM), lambda bb, gg, i: (bb, gg, i, 0)),
            pl.BlockSpec((1, TQ, LANES), lambda bb, gg, i: (bb, i, 0)),
            cmp_spec, cmp_spec, kv, kv, kv, kv,
            pl.BlockSpec((hg, 5, TQ, TQ), lambda bb, gg, i: (gg, 0, 0, 0)),
            pl.BlockSpec((hg, TQ, LANES), lambda bb, gg, i: (gg, i, 0)),
            pl.BlockSpec((LANES, LANES), lambda bb, gg, i: (0, 0)),
            pl.BlockSpec((LANES, t), lambda bb, gg, i: (0, 0)),
        ],
        out_specs=pl.BlockSpec((1, hg, TQ, HEAD_DIM), lambda bb, gg, i: (bb, gg, i, 0)),
        out_shape=jax.ShapeDtypeStruct(q.shape, F32),
        scratch_shapes=[pltpu.VMEM((nq, TQ, TQ), F32), pltpu.VMEM((hg * TQ, t), F32)],
        compiler_params=_cparams("parallel", "parallel", "arbitrary"),
    )(q, gates, ck, cv, sk, sv, wk, wv, bias, cbias, _overlap_matrix().astype(BF16), _block_expand(t))


def _dsa_index_kernel(qi_ref, wi_ref, ki_ref, o_ref):
    i = pl.program_id(1)
    t = ki_ref.shape[1]
    nq = t // TQ
    wi = wi_ref[0]

    def select(width):
        ki = ki_ref[0, :width, :IDX_DIM].astype(BF16)
        score = jnp.zeros((TQ, width), F32)
        for h in range(IDX_HEADS):
            s = lax.dot_general(qi_ref[0, h].astype(BF16), ki, NT_DIMS, preferred_element_type=F32)
            score = score + wi[:, h:h + 1] * jnp.maximum(s * IDX_DIM ** -0.5, 0.0)
        score = score * IDX_HEADS ** -0.5
        kpos = lax.broadcasted_iota(jnp.int32, (TQ, width), 1)
        qpos = i * TQ + lax.broadcasted_iota(jnp.int32, (TQ, width), 0)
        keep = _top_k_mask(score, kpos <= qpos, kpos, min(DSA_TOPK, t // 4))
        add = jnp.where(keep, 0.0, NEG_INF).astype(BF16)
        for j in range(nq):
            lo = j * TQ
            o_ref[0, 0, j] = add[:, lo:lo + TQ] if lo < width else jnp.full((TQ, TQ), NEG_INF, BF16)

    widths = [w for w in (t // 4, t // 2, t) if w % TQ == 0 and w >= TQ]
    lo_tile = 0
    for w in widths:
        hi_tile = w // TQ
        pl.when(jnp.logical_and(i >= lo_tile, i < hi_tile))(functools.partial(select, w))
        lo_tile = hi_tile


def dsa_index_prompt(qi, wi, ki):
    b, ih, t, _ = qi.shape
    nq = t // TQ
    return pl.pallas_call(
        _dsa_index_kernel,
        grid=(b, nq),
        in_specs=[
            pl.BlockSpec((1, ih, TQ, IDX_DIM), lambda bb, i: (bb, 0, i, 0)),
            pl.BlockSpec((1, TQ, LANES), lambda bb, i: (bb, i, 0)),
            pl.BlockSpec((1, t, LANES), lambda bb, i: (bb, 0, 0)),
        ],
        out_specs=pl.BlockSpec((1, 1, nq, TQ, TQ), lambda bb, i: (bb, i, 0, 0, 0)),
        out_shape=jax.ShapeDtypeStruct((b, nq, nq, TQ, TQ), BF16),
        compiler_params=_cparams("parallel", "parallel"),
    )(qi, wi, ki)


def _dsa_attn_kernel(q_ref, k_ref, v_ref, bias_ref, m_ref, o_ref, s_ref):
    i = pl.program_id(2)
    hg = DSA_GROUP
    rows = hg * TQ

    def keys_of(c):
        return pl.ds(pl.multiple_of(c * CK, CK), CK)

    def add_of(c):
        j0 = 2 * c
        mask = jnp.concatenate([m_ref[0, 0, j0], m_ref[0, 0, j0 + 1]], axis=1).astype(F32)
        add = jnp.concatenate([bias_ref[:, _tile_kind(i - j0)].reshape(rows, TQ),
                               bias_ref[:, _tile_kind(i - j0 - 1)].reshape(rows, TQ)], axis=1)
        return add + jnp.concatenate([mask] * hg, axis=0)

    o = _attend(q_ref[0].reshape(rows, HEAD_DIM), i // 2 + 1, lambda c: k_ref[0, 0, keys_of(c), :],
                lambda c: v_ref[0, 0, keys_of(c), :], add_of, s_ref)
    o_ref[0] = o.reshape(hg, TQ, HEAD_DIM)


def dsa_attn_prompt(q, k, v, bias, mask):
    b, _, t, _ = q.shape
    hg = DSA_GROUP
    nq = t // TQ
    kv = pl.BlockSpec((1, 1, t, HEAD_DIM), lambda bb, gg, i: (bb, gg, 0, 0))
    return pl.pallas_call(
        _dsa_attn_kernel,
        grid=(b, DSA_KV, nq),
        in_specs=[
            pl.BlockSpec((1, hg, TQ, HEAD_DIM), lambda bb, gg, i: (bb, gg, i, 0)),
            kv, kv,
            pl.BlockSpec((hg, 5, TQ, TQ), lambda bb, gg, i: (gg, 0, 0, 0)),
            pl.BlockSpec((1, 1, nq, TQ, TQ), lambda bb, gg, i: (bb, i, 0, 0, 0)),
        ],
        out_specs=pl.BlockSpec((1, hg, TQ, HEAD_DIM), lambda bb, gg, i: (bb, gg, i, 0)),
        out_shape=jax.ShapeDtypeStruct(q.shape, F32),
        scratch_shapes=[pltpu.VMEM((hg * TQ, t), F32)],
        compiler_params=_cparams("parallel", "parallel", "arbitrary"),
    )(q, k, v, bias, mask)


def _page_specs(lane_block, width):
    return [pl.BlockSpec((1, PAGE_SIZE, width), functools.partial(lambda p, b, pt: (pt[b, p], 0, lane_block), p))
            for p in range(N_PAGES)]


def _row_spec(*shape):
    return pl.BlockSpec((1,) + shape, lambda b, pt: (b,) + (0,) * len(shape))


def _const_spec(*shape):
    return pl.BlockSpec(shape, lambda b, pt: (0,) * len(shape))


def _decode_call(kernel, n_req, page_table, in_specs, args, out_shapes, out_specs):
    return pl.pallas_call(
        kernel,
        grid_spec=pltpu.PrefetchScalarGridSpec(num_scalar_prefetch=1, grid=(n_req,), in_specs=in_specs,
                                               out_specs=out_specs),
        out_shape=out_shapes,
        compiler_params=_cparams("parallel"),
    )(page_table, *args)


def _paged_attention(qblk, k_pages, v_pages, bias_ref, mask_of_page, k_new, v_new, new_add):
    qb = (qblk * QK_SCALE).astype(BF16)
    s_new = jnp.sum(qblk * k_new, axis=1, keepdims=True) * QK_SCALE + new_add
    s_pages = []
    for p in range(N_PAGES):
        s = lax.dot_general(qb, k_pages[p][0].astype(BF16), NT_DIMS, preferred_element_type=F32)
        s_pages.append(s + bias_ref[:, p * PAGE_SIZE:(p + 1) * PAGE_SIZE] + mask_of_page(p))
    m = s_new
    for s in s_pages:
        m = jnp.maximum(m, jnp.max(s, axis=1, keepdims=True))
    e_new = jnp.exp(s_new - m)
    l = e_new
    acc = e_new * v_new
    for p in range(N_PAGES):
        e = jnp.exp(s_pages[p] - m)
        l = l + jnp.sum(e, axis=1, keepdims=True)
        acc = acc + jnp.dot(e.astype(BF16), v_pages[p][0].astype(BF16), preferred_element_type=F32)
    return acc / jnp.maximum(l, TINY)


def _group_lanes(o_full, heads_per_group):
    hp, w = o_full.shape
    grp = lax.broadcasted_iota(jnp.int32, (hp, HEAD_DIM), 0) // heads_per_group
    out = o_full[:, :HEAD_DIM]
    for gg in range(1, w // HEAD_DIM):
        out = jnp.where(grp == gg, o_full[:, gg * HEAD_DIM:(gg + 1) * HEAD_DIM], out)
    return out


MOBA_HP = 8


def _moba_decode_kernel(pt_ref, q_ref, kn_ref, vn_ref, bias_ref, bnew_ref, *refs):
    k_pages = refs[:N_PAGES]
    v_pages = refs[N_PAGES:2 * N_PAGES]
    o_ref = refs[2 * N_PAGES]
    qblk = q_ref[0]
    pages_per_block = MOBA_BLOCK // PAGE_SIZE
    n_blk = N_PAGES // pages_per_block
    means = []
    for blk in range(n_blk):
        tot = jnp.sum(k_pages[blk * pages_per_block][0], axis=0, keepdims=True)
        for p in range(blk * pages_per_block + 1, (blk + 1) * pages_per_block):
            tot = tot + jnp.sum(k_pages[p][0], axis=0, keepdims=True)
        means.append(tot * (1.0 / MOBA_BLOCK))
    k_mean = jnp.concatenate(means, axis=0)
    gate = lax.dot_general(qblk, k_mean, NT_DIMS, preferred_element_type=F32, precision=HIGHEST)
    chosen = _top_k_rows(gate, jnp.ones(gate.shape, jnp.bool_), MOBA_TOPK)
    blk_add = jnp.where(chosen, 0.0, NEG_INF)

    def mask_of_page(p):
        b0 = p // pages_per_block
        return blk_add[:, b0:b0 + 1]

    o_full = _paged_attention(qblk, k_pages, v_pages, bias_ref, mask_of_page, kn_ref[0], vn_ref[0],
                              bnew_ref[:, 0:1])
    o_ref[0] = _group_lanes(o_full, 1)


def moba_decode(page_table, qblk, k_new, v_new, bias, bnew, cache_k, cache_v, layer):
    n_req = qblk.shape[0]
    in_specs = ([_row_spec(MOBA_HP, MOBA_W), _row_spec(1, MOBA_W), _row_spec(1, MOBA_W),
                 _const_spec(MOBA_HP, PAST_LEN), _const_spec(MOBA_HP, LANES)]
                + _page_specs(layer, MOBA_W) + _page_specs(layer, MOBA_W))
    args = [qblk, k_new, v_new, bias, bnew] + [cache_k] * N_PAGES + [cache_v] * N_PAGES
    return _decode_call(_moba_decode_kernel, n_req, page_table, in_specs, args,
                        jax.ShapeDtypeStruct((n_req, MOBA_HP, HEAD_DIM), F32), _row_spec(MOBA_HP, HEAD_DIM))


NSA_HP = 16
GRP_ROWS = 8


def _compress_pages(pages, pe_ref, w1_ref, w2, t):
    ha = jnp.zeros((N_CHUNK, HALF_W), F32)
    hb = jnp.zeros((N_CHUNK, HALF_W), F32)
    per_page = PAGE_SIZE // CMP_STRIDE
    for r in range(CMP_STRIDE):
        xr = jnp.concatenate([pg[0, pl.ds(r, per_page, stride=CMP_STRIDE), :] for pg in pages], axis=0)
        lanes = slice(r * NSA_KW, (r + 1) * NSA_KW)
        w1 = w1_ref[t, r * NSA_KW:(r + 1) * NSA_KW, :]
        ha = ha + jnp.dot((xr + pe_ref[t, 0:1, lanes]).astype(BF16), w1[:, :HALF_W], preferred_element_type=F32)
        hb = hb + jnp.dot((xr + pe_ref[t, 1:2, lanes]).astype(BF16), w1[:, HALF_W:], preferred_element_type=F32)
    return _compress_tail(ha, hb, w2)


def _nsa_decode_kernel(pt_ref, q_ref, gt_ref, skn_ref, svn_ref, wkn_ref, wvn_ref, wkb_ref, wvb_ref,
                       pe_ref, w1_ref, w2_ref, gain_ref, seg_ref, cb_ref, sb_ref, wb_ref, bnew_ref,
                       ov_ref, ex_ref, g2_ref, p16_ref, *refs):
    ck_pages = refs[:N_PAGES]
    cv_pages = refs[N_PAGES:2 * N_PAGES]
    sk_pages = refs[2 * N_PAGES:3 * N_PAGES]
    sv_pages = refs[3 * N_PAGES:4 * N_PAGES]
    o_ref = refs[4 * N_PAGES]
    qblk = q_ref[0]
    new_add = bnew_ref[:, 0:1]
    lane = lax.broadcasted_iota(jnp.int32, (GRP_ROWS, LANES), 1)

    ck = _head_rms(_compress_pages(ck_pages, pe_ref, w1_ref, w2_ref[0], 0), seg_ref, gain_ref[...])
    cv = _compress_pages(cv_pages, pe_ref, w1_ref, w2_ref[1], 1)
    cvalid = lax.broadcasted_iota(jnp.int32, (NSA_HP, LANES), 1) < N_CMP
    lc = lax.dot_general(qblk, ck, NT_DIMS, preferred_element_type=F32, precision=HIGHEST)
    lc = jnp.where(cvalid, lc * QK_SCALE + cb_ref[...], NEG_INF)
    e = jnp.where(cvalid, jnp.exp(lc - jnp.max(lc, axis=1, keepdims=True)), 0.0)
    pc = e / jnp.maximum(jnp.sum(e, axis=1, keepdims=True), TINY)
    o_cmp = jnp.dot(pc.astype(BF16), cv.astype(BF16), preferred_element_type=F32)

    psum = jnp.dot(g2_ref[...], pc, preferred_element_type=F32, precision=HIGHEST)
    imp = jnp.dot(psum, ov_ref[...], preferred_element_type=F32, precision=HIGHEST)
    cur = PAST_LEN // SEL_BLOCK
    forced = jnp.logical_or(lane == 0, jnp.logical_or(lane == cur, lane == cur - 1))
    picked = _top_k_rows(jnp.where(forced, SEL_FORCE, imp), lane <= cur, N_SEL)
    pick_h = jnp.dot(p16_ref[...], jnp.where(picked, 1.0, 0.0).astype(BF16), preferred_element_type=F32)
    key_hit = jnp.dot(pick_h.astype(BF16), ex_ref[...], preferred_element_type=F32)
    key_add = jnp.where(key_hit > 0.5, 0.0, NEG_INF)

    o_slc = _paged_attention(qblk, sk_pages, sv_pages, sb_ref, lambda p: key_add[:, p * PAGE_SIZE:(p + 1) * PAGE_SIZE],
                             skn_ref[0], svn_ref[0], new_add)

    qb = (qblk * QK_SCALE).astype(BF16)
    sw = lax.dot_general(qb, wkb_ref[0].astype(BF16), NT_DIMS, preferred_element_type=F32) + wb_ref[...]
    s_new = jnp.sum(qblk * wkn_ref[0], axis=1, keepdims=True) * QK_SCALE + new_add
    m = jnp.maximum(s_new, jnp.max(sw, axis=1, keepdims=True))
    ew = jnp.exp(sw - m)
    e_new = jnp.exp(s_new - m)
    l = e_new + jnp.sum(ew, axis=1, keepdims=True)
    o_win = (e_new * wvn_ref[0] + jnp.dot(ew.astype(BF16), wvb_ref[0].astype(BF16), preferred_element_type=F32))
    o_win = o_win / jnp.maximum(l, TINY)

    gt = gt_ref[0]
    o_full = gt[:, 0:1] * o_cmp + gt[:, 1:2] * o_slc + gt[:, 2:3] * o_win
    o_ref[0] = _group_lanes(o_full, NSA_GROUP)


def nsa_decode(page_table, qblk, gates, sk_new, sv_new, wk_new, wv_new, win_k, win_v, cmp_w, gain_row,
               cbias, sbias, wbias, bnew, cache_ck, cache_cv, cache_sk, cache_sv, layer):
    n_req = qblk.shape[0]
    pe_x, w1_x, w2_x = cmp_w
    hh = jnp.arange(NSA_HP)
    g2 = jnp.logical_and(hh[None, :] // NSA_GROUP == jnp.arange(GRP_ROWS)[:, None], hh[None, :] < NSA_HEADS)
    p16 = (hh[:, None] // NSA_GROUP == jnp.arange(GRP_ROWS)[None, :]).astype(BF16)
    win_spec = pl.BlockSpec((1, WINDOW, NSA_KW), lambda b, pt: (b, 0, layer))
    in_specs = ([_row_spec(NSA_HP, NSA_KW), _row_spec(NSA_HP, LANES)] + [_row_spec(1, NSA_KW)] * 4
                + [win_spec, win_spec,
                   _const_spec(2, 2, CHUNK_W), _const_spec(2, CHUNK_W, 2 * HALF_W), _const_spec(2, HALF_W, NSA_KW),
                   _const_spec(1, NSA_KW), _const_spec(LANES, LANES),
                   _const_spec(NSA_HP, LANES), _const_spec(NSA_HP, PAST_LEN), _const_spec(NSA_HP, WINDOW),
                   _const_spec(NSA_HP, LANES), _const_spec(LANES, LANES), _const_spec(LANES, PAST_LEN),
                   _const_spec(GRP_ROWS, NSA_HP), _const_spec(NSA_HP, GRP_ROWS)]
                + _page_specs(layer, NSA_KW) * 4)
    args = ([qblk, gates, sk_new, sv_new, wk_new, wv_new, win_k, win_v, pe_x, w1_x, w2_x, gain_row, _seg_ones(),
             cbias, sbias, wbias, bnew, _overlap_matrix(), _block_expand(PAST_LEN), g2.astype(F32), p16]
            + [cache_ck] * N_PAGES + [cache_cv] * N_PAGES + [cache_sk] * N_PAGES + [cache_sv] * N_PAGES)
    return _decode_call(_nsa_decode_kernel, n_req, page_table, in_specs, args,
                        jax.ShapeDtypeStruct((n_req, NSA_HP, HEAD_DIM), F32), _row_spec(NSA_HP, HEAD_DIM))


KEYS_PAD = PAST_LEN + LANES


def _dsa_score_kernel(pt_ref, qi_ref, wi_ref, kin_ref, *refs):
    pages = refs[:N_PAGES]
    o_ref = refs[N_PAGES]
    qi = qi_ref[0]
    wi = wi_ref[0][:, 0:1]
    parts = []
    for p in range(N_PAGES):
        s = lax.dot_general(qi, pages[p][0], NT_DIMS, preferred_element_type=F32, precision=HIGHEST)
        parts.append(jnp.sum(wi * jnp.maximum(s * IDX_DIM ** -0.5, 0.0), axis=0, keepdims=True))
    s_new = jnp.sum(qi * kin_ref[0], axis=1, keepdims=True) * IDX_DIM ** -0.5
    new = jnp.sum(wi * jnp.maximum(s_new, 0.0), axis=0, keepdims=True)
    lane = lax.broadcasted_iota(jnp.int32, (1, LANES), 1)
    parts.append(jnp.where(lane == 0, new, 0.0))
    o_ref[0] = jnp.concatenate(parts, axis=1) * IDX_HEADS ** -0.5


def dsa_score_decode(page_table, qi, wi, ki_new, cache_idx):
    n_req = qi.shape[0]
    in_specs = ([_row_spec(IDX_HEADS, LANES), _row_spec(IDX_HEADS, LANES), _row_spec(1, LANES)]
                + _page_specs(0, LANES))
    args = [qi, wi, ki_new] + [cache_idx] * N_PAGES
    return _decode_call(_dsa_score_kernel, n_req, page_table, in_specs, args,
                        jax.ShapeDtypeStruct((n_req, 1, KEYS_PAD), F32), _row_spec(1, KEYS_PAD))


def _topk_rows_kernel(k, s_ref, o_ref):
    score = s_ref[...]
    pos = lax.broadcasted_iota(jnp.int32, score.shape, 1)
    keep = _top_k_mask(score, pos <= PAST_LEN, pos, k)
    o_ref[...] = jnp.where(keep, 0.0, NEG_INF)


def dsa_topk_decode(score):
    k = min(DSA_TOPK, (PAST_LEN + 1) // 4)
    return pl.pallas_call(
        functools.partial(_topk_rows_kernel, k),
        out_shape=jax.ShapeDtypeStruct(score.shape, F32),
        compiler_params=pltpu.CompilerParams(vmem_limit_bytes=VMEM_LIMIT),
    )(score)


def _dsa_decode_kernel(pt_ref, q_ref, kn_ref, vn_ref, m_ref, bias_ref, bnew_ref, *refs):
    k_pages = refs[:N_PAGES]
    v_pages = refs[N_PAGES:2 * N_PAGES]
    o_ref = refs[2 * N_PAGES]
    mask = m_ref[0]
    new_add = bnew_ref[:, 0:1] + mask[:, PAST_LEN:PAST_LEN + 1]
    o_full = _paged_attention(q_ref[0], k_pages, v_pages, bias_ref,
                              lambda p: mask[:, p * PAGE_SIZE:(p + 1) * PAGE_SIZE], kn_ref[0], vn_ref[0], new_add)
    o_ref[0] = _group_lanes(o_full, DSA_GROUP)


def dsa_attn_decode(page_table, qblk, k_new, v_new, mask, bias, bnew, cache_k, cache_v, layer):
    n_req = qblk.shape[0]
    in_specs = ([_row_spec(DSA_HEADS, DSA_KW), _row_spec(1, DSA_KW), _row_spec(1, DSA_KW), _row_spec(1, KEYS_PAD),
                 _const_spec(DSA_HEADS, PAST_LEN), _const_spec(DSA_HEADS, LANES)]
                + _page_specs(layer, DSA_KW) * 2)
    args = [qblk, k_new, v_new, mask, bias, bnew] + [cache_k] * N_PAGES + [cache_v] * N_PAGES
    return _decode_call(_dsa_decode_kernel, n_req, page_table, in_specs, args,
                        jax.ShapeDtypeStruct((n_req, DSA_HEADS, HEAD_DIM), F32), _row_spec(DSA_HEADS, HEAD_DIM))


EVEN_WP = 2432
EVEN_MODES = ("norm",) * 4 + ("raw",) * 2 + ("norm",) * 6 + ("raw", "raw", "norm", "raw", "norm", "raw", "sigmoid")
EVEN_CACHE_OUTS = (("cols", 256, 256), ("cols", 512, 256), ("cols", 1536, 128), ("cols", 1664, 128),
                   ("cols", 1792, 128), ("cols", 1920, 128), ("cols", 2048, 128), ("cols", 2176, 128))
EVEN_OUTS_PROMPT = EVEN_CACHE_OUTS + (
    ("cols", 2304, 128), ("heads", 0, 4, F32, 1.0), ("heads", 256, 4, F32, 1.0), ("heads", 512, 4, BF16, 1.0),
    ("heads", 768, 12, F32, 1.0), ("heads", 1792, 2, BF16, 1.0), ("heads", 1920, 2, BF16, 1.0),
    ("heads", 2048, 2, BF16, 1.0), ("heads", 2176, 2, BF16, 1.0))
EVEN_OUTS_DECODE = EVEN_CACHE_OUTS + (("cols", 2304, 128), ("cols", 0, 256), ("cols", 768, 768))
ODD_WP = 2304
ODD_MODES = ("norm",) * 10 + ("raw",) * 8
ODD_CACHE_OUTS = (("cols", 1024, 256), ("cols", 1280, 256), ("cols", 2048, 128))
ODD_OUTS_PROMPT = ODD_CACHE_OUTS + (
    ("cols", 2176, 128), ("heads", 0, 16, BF16, QK_SCALE), ("heads", 1024, 4, BF16, 1.0),
    ("heads", 1280, 4, BF16, 1.0), ("heads", 1536, 8, F32, 1.0))
ODD_OUTS_DECODE = ODD_CACHE_OUTS + (("cols", 2176, 128), ("cols", 0, 1024), ("cols", 1536, 512))


def _even_weights(w_in, moba_g, nsa_g):
    w = jnp.pad(w_in, ((0, 0), (0, EVEN_WP - w_in.shape[1]))).astype(BF16)
    z = lambda n: jnp.zeros((n,), F32)
    gain = jnp.concatenate([jnp.tile(moba_g[0], 4), jnp.tile(moba_g[1], 4), z(256), jnp.tile(nsa_g[0], 12), z(256),
                            jnp.tile(nsa_g[2], 2), z(128), jnp.tile(nsa_g[3], 2), z(256)])
    return w, gain.reshape(1, EVEN_WP)


def _odd_weights(w_in, qk_g):
    zc = lambda n: jnp.zeros((D_MODEL, n), F32)
    w = jnp.concatenate([w_in[:, :2112], zc(64), w_in[:, 2112:], zc(ODD_WP - 2176 - IDX_HEADS)], axis=1).astype(BF16)
    gain = jnp.concatenate([jnp.tile(qk_g[0], 16), jnp.tile(qk_g[1], 4), jnp.zeros((ODD_WP - 1280,), F32)])
    return w, gain.reshape(1, ODD_WP)


def _block_queries(q, n_heads, heads_per_group, rows):
    b = q.shape[0]
    n_groups = n_heads // heads_per_group
    qh = q.reshape(b, n_heads, 1, HEAD_DIM)
    grp = (jnp.arange(n_heads)[:, None] // heads_per_group == jnp.arange(n_groups)[None, :]).astype(F32)
    blk = (qh * grp[None, :, :, None]).reshape(b, n_heads, n_groups * HEAD_DIM)
    return jnp.pad(blk, ((0, 0), (0, rows - n_heads), (0, 0)))


def _decode_heads(o, n_heads):
    return o[:, :n_heads].transpose(1, 0, 2)[None]


def kernel(x_prompt, x_sample, cache_moba_k, cache_moba_v, cache_nsa_cmp_k, cache_nsa_cmp_v, cache_nsa_slc_k,
           cache_nsa_slc_v, state_nsa_win_k, state_nsa_win_v, cache_dsa_k, cache_dsa_v, cache_dsa_idx_k, page_table,
           c_prompt, c_sample, bias_table, norm_gain, ada_w, ada_b, ffn_w_in, ffn_w_out, even_w_in, even_w_out,
           moba_qk_gain, nsa_qk_gain, nsa_cmp_pe, nsa_cmp_w1, nsa_cmp_w2, odd_w_in, odd_w_out, dsa_qk_gain):
    bp, t, _ = x_prompt.shape
    bs = x_sample.shape[0]
    n_pool = cache_moba_k.shape[0]

    tab_m, tab_n = bias_table[:, :MOBA_HEADS], bias_table[:, MOBA_HEADS:]
    bias_m = _bias_tiles(tab_m, MOBA_BLOCK)[:, :3]
    bias_n = _bias_tiles(tab_n, TQ)
    bias_d = _bias_tiles(bias_table, TQ)
    cmp_end = jnp.arange(LANES) * CMP_STRIDE + CMP_LEN - 1
    cbias = _rel_bias(jnp.arange(t)[:, None] - cmp_end[None, :], tab_n)
    pad_rows = lambda a, rows: jnp.pad(a, ((0, rows - a.shape[0]), (0, 0)))
    past_dist = PAST_LEN - jnp.arange(PAST_LEN)
    dec_bias_m = pad_rows(_rel_bias(past_dist, tab_m), MOBA_HP)
    dec_bias_n = pad_rows(_rel_bias(past_dist, tab_n), NSA_HP)
    dec_bias_d = _rel_bias(past_dist, bias_table)
    dec_cbias = pad_rows(_rel_bias(PAST_LEN - cmp_end, tab_n), NSA_HP)
    win_dist = WINDOW - jnp.arange(WINDOW)
    dec_wbias = pad_rows(jnp.where(win_dist < WINDOW, _rel_bias(win_dist, tab_n), NEG_INF), NSA_HP)
    new_bias = jnp.broadcast_to(bias_table[0][:, None], (DSA_HEADS, LANES))
    bnew_m = pad_rows(new_bias[:MOBA_HEADS], MOBA_HP)
    bnew_n = pad_rows(new_bias[MOBA_HEADS:], NSA_HP)

    flat = lambda c: c.reshape(n_pool, PAGE_SIZE, -1)
    pool_mk, pool_mv = flat(cache_moba_k), flat(cache_moba_v)
    pool_ck, pool_cv = flat(cache_nsa_cmp_k), flat(cache_nsa_cmp_v)
    pool_sk, pool_sv = flat(cache_nsa_slc_k), flat(cache_nsa_slc_v)
    pool_dk, pool_dv, pool_di = flat(cache_dsa_k), flat(cache_dsa_v), flat(cache_dsa_idx_k)
    win_k_all = state_nsa_win_k.reshape(bs, WINDOW, -1)
    win_v_all = state_nsa_win_v.reshape(bs, WINDOW, -1)

    mods = adaln_all(jnp.concatenate([c_prompt, c_sample], axis=0), ada_w, ada_b)
    mods = mods.reshape(DEPTH, bp + bs, 3, 3, D_MODEL)
    ffn_in = ffn_w_in.astype(BF16)
    ffn_out = ffn_w_out.astype(BF16)

    yp = x_prompt
    ys = x_sample.reshape(1, bs, D_MODEL)
    rows_ep = [[] for _ in range(8)]
    rows_es = [[] for _ in range(8)]
    rows_op = [[] for _ in range(3)]
    rows_os = [[] for _ in range(3)]
    for li in range(DEPTH):
        mp = mods[li, :bp][:, None]
        ms = mods[li, bp:][None]
        mod = lambda m, s, k: m[:, :, s, k]

        def ffn(y, m, s, w_idx):
            return ffn_half(y, mod(m, s, 0), mod(m, s, 1), mod(m, s, 2), norm_gain[li, s], ffn_in[li, w_idx],
                            ffn_out[li, w_idx])

        yp, ys = ffn(yp, mp, 0, 0), ffn(ys, ms, 0, 0)
        if li % 2 == 0:
            e = li // 2
            w_p, gain_cols = _even_weights(even_w_in[e], moba_qk_gain[e], nsa_qk_gain[e])
            w_o = even_w_out[e].astype(BF16)
            cmp_w = _compress_weights(nsa_cmp_pe[e], nsa_cmp_w1[e], nsa_cmp_w2[e])
            cmp_gain = jnp.tile(nsa_qk_gain[e, 1], NSA_KV).reshape(1, NSA_KW)
            project = lambda y, m, outs: mixer_project(y, mod(m, 1, 0), mod(m, 1, 1), norm_gain[li, 1], w_p,
                                                       gain_cols, EVEN_MODES, outs)
            pp = project(yp, mp, EVEN_OUTS_PROMPT)
            gl, mq_h, mk_h, mv_h, nq_h, sk_h, sv_h, wk_h, wv_h = pp[8:]
            o_m = moba_prompt(mq_h, mk_h, mv_h, bias_m)
            ck, cv = nsa_compress_prompt(pp[2], pp[3], *cmp_w, cmp_gain)
            o_n = nsa_prompt(nq_h, gl, ck, cv, sk_h, sv_h, wk_h, wv_h, bias_n, cbias)
            pieces_p = [o_m, o_n]
            wb = min(WINDOW, t)
            new_p = list(pp[:6]) + [pp[6][:, t - wb:], pp[7][:, t - wb:]]
            ps = [a.reshape(bs, 1, a.shape[-1]) for a in project(ys, ms, EVEN_OUTS_DECODE)]
            mk_s, mv_s, ckr_s, cvr_s, sk_s, sv_s, wk_s, wv_s, gl_s, mq_s, nq_s = ps
            o_ms = moba_decode(page_table, _block_queries(mq_s[:, 0], MOBA_HEADS, 1, MOBA_HP), mk_s, mv_s,
                               dec_bias_m, bnew_m, pool_mk, pool_mv, e)
            gates_s = jnp.pad(gl_s[:, 0, :3 * NSA_HEADS].reshape(bs, NSA_HEADS, 3),
                              ((0, 0), (0, NSA_HP - NSA_HEADS), (0, LANES - 3)))
            o_ns = nsa_decode(page_table, _block_queries(nq_s[:, 0], NSA_HEADS, NSA_GROUP, NSA_HP), gates_s,
                              sk_s, sv_s, wk_s, wv_s, win_k_all, win_v_all, cmp_w, cmp_gain,
                              dec_cbias, dec_bias_n, dec_wbias, bnew_n, pool_ck, pool_cv, pool_sk, pool_sv, e)
            pieces_s = [_decode_heads(o_ms, MOBA_HEADS), _decode_heads(o_ns, NSA_HEADS)]
            keep = min(WINDOW, PAST_LEN + 1)
            lanes_e = slice(e * NSA_KW, (e + 1) * NSA_KW)
            new_s = [mk_s, mv_s, ckr_s, cvr_s, sk_s, sv_s,
                     jnp.concatenate([win_k_all[:, :, lanes_e], wk_s], axis=1)[:, -keep:],
                     jnp.concatenate([win_v_all[:, :, lanes_e], wv_s], axis=1)[:, -keep:]]
            for acc, r in zip(rows_ep, new_p):
                acc.append(r)
            for acc, r in zip(rows_es, new_s):
                acc.append(r)
        else:
            o = li // 2
            w_p, gain_cols = _odd_weights(odd_w_in[o], dsa_qk_gain[o])
            w_o = odd_w_out[o].astype(BF16)
            project = lambda y, m, outs: mixer_project(y, mod(m, 1, 0), mod(m, 1, 1), norm_gain[li, 1], w_p,
                                                       gain_cols, ODD_MODES, outs)
            pp = project(yp, mp, ODD_OUTS_PROMPT)
            wi, q_h, k_h, v_h, qi_h = pp[3:]
            keep_mask = dsa_index_prompt(qi_h, wi, pp[2])
            pieces_p = [dsa_attn_prompt(q_h, k_h, v_h, bias_d, keep_mask)]
            ps = [a.reshape(bs, 1, a.shape[-1]) for a in project(ys, ms, ODD_OUTS_DECODE)]
            k_s, v_s, ki_s, wi_s, q_s, qi_s = ps
            idx_lanes = lambda a: jnp.pad(a, ((0, 0), (0, 0), (o * IDX_DIM, LANES - (o + 1) * IDX_DIM)))
            qi_blk = idx_lanes(qi_s[:, 0].reshape(bs, IDX_HEADS, IDX_DIM))
            wi_rows = jnp.broadcast_to(wi_s[:, 0, :IDX_HEADS, None], (bs, IDX_HEADS, LANES))
            score = dsa_score_decode(page_table, qi_blk, wi_rows, idx_lanes(ki_s[:, :, :IDX_DIM]), pool_di)
            keep_s = dsa_topk_decode(score.reshape(bs, KEYS_PAD)).reshape(bs, 1, KEYS_PAD)
            o_ds = dsa_attn_decode(page_table, _block_queries(q_s[:, 0], DSA_HEADS, DSA_GROUP, DSA_HEADS), k_s, v_s,
                                   keep_s, dec_bias_d, new_bias, pool_dk, pool_dv, o)
            pieces_s = [_decode_heads(o_ds, DSA_HEADS)]
            for acc, r in zip(rows_op, (pp[0], pp[1], pp[2][..., :IDX_DIM])):
                acc.append(r)
            for acc, r in zip(rows_os, (k_s, v_s, ki_s[..., :IDX_DIM])):
                acc.append(r)
        yp = mixer_merge(yp, pieces_p, mod(mp, 1, 2), w_o)
        ys = mixer_merge(ys, pieces_s, mod(ms, 1, 2), w_o)
        yp, ys = ffn(yp, mp, 2, 1), ffn(ys, ms, 2, 1)

    def stack_layers(rows, n_heads):
        return jnp.stack([r.reshape(r.shape[0], r.shape[1], n_heads, HEAD_DIM) for r in rows], axis=2)

    even_heads = (MOBA_HEADS, MOBA_HEADS) + (NSA_KV,) * 6
    moba_k_p, moba_v_p, cmp_k_p, cmp_v_p, slc_k_p, slc_v_p, win_k_p, win_v_p = [
        stack_layers(r, n) for r, n in zip(rows_ep, even_heads)]
    moba_k_s, moba_v_s, cmp_k_s, cmp_v_s, slc_k_s, slc_v_s, win_k_s, win_v_s = [
        stack_layers(r, n) for r, n in zip(rows_es, even_heads)]
    dsa_k_p, dsa_v_p = stack_layers(rows_op[0], DSA_KV), stack_layers(rows_op[1], DSA_KV)
    dsa_k_s, dsa_v_s = stack_layers(rows_os[0], DSA_KV), stack_layers(rows_os[1], DSA_KV)
    dsa_idx_k_p, dsa_idx_k_s = jnp.stack(rows_op[2], axis=2), jnp.stack(rows_os[2], axis=2)
    return (yp, ys.reshape(bs, 1, D_MODEL), moba_k_p, moba_k_s, moba_v_p, moba_v_s, cmp_k_p, cmp_k_s, cmp_v_p,
            cmp_v_s, slc_k_p, slc_k_s, slc_v_p, slc_v_s, win_k_p, win_k_s, win_v_p, win_v_s,
            dsa_k_p, dsa_k_s, dsa_v_p, dsa_v_s, dsa_idx_k_p, dsa_idx_k_s)
```

```python
import functools
import math

import jax
import jax.numpy as jnp
from jax import lax
from jax.experimental import pallas as pl
from jax.experimental.pallas import tpu as pltpu

D_MODEL = 1024
DEPTH = 4
PAST_LEN = 2048
PAGE_SIZE = 128
N_PAGES = PAST_LEN // PAGE_SIZE
HEAD_DIM = 64
MOBA_HEADS = 4
NSA_HEADS = 12
NSA_KV = 2
NSA_GROUP = 6
DSA_HEADS = 16
DSA_KV = 4
DSA_GROUP = 4
IDX_HEADS = 8
IDX_DIM = 64
MOBA_BLOCK = 256
MOBA_TOPK = 3
CMP_LEN = 32
CMP_STRIDE = 16
CMP_HIDDEN = 128
SEL_BLOCK = 64
N_SEL = 8
WINDOW = 512
DSA_TOPK = 256
N_BUCKETS = 32
MAX_DISTANCE = 128
D_FF = 2816
MOBA_W = MOBA_HEADS * HEAD_DIM
NSA_W = NSA_HEADS * HEAD_DIM
NSA_KW = NSA_KV * HEAD_DIM
DSA_W = DSA_HEADS * HEAD_DIM
DSA_KW = DSA_KV * HEAD_DIM
RMS_EPS = 1e-6
NEG_INF = -1e30
TINY = 1e-30
SEL_FORCE = 1e4
QK_SCALE = HEAD_DIM ** -0.5

LANES = 128
VMEM_LIMIT = 56 * 1024 * 1024
BF16 = jnp.bfloat16
F32 = jnp.float32
HIGHEST = lax.Precision.HIGHEST
NT_DIMS = (((1,), (1,)), ((), ()))
M_INIT = -3e38
INT_MIN = -2 ** 31


def _cparams(*sem):
    return pltpu.CompilerParams(dimension_semantics=sem, vmem_limit_bytes=VMEM_LIMIT)


def _t5_bucket(dist):
    n = jnp.maximum(dist, 0)
    exact = N_BUCKETS // 2
    nf = jnp.maximum(n, 1).astype(F32)
    large = exact + (jnp.log(nf / exact) / math.log(MAX_DISTANCE / exact) * (N_BUCKETS - exact)).astype(jnp.int32)
    return jnp.where(n < exact, n, jnp.minimum(large, N_BUCKETS - 1))


def _rel_bias(dist, table):
    onehot = jax.nn.one_hot(_t5_bucket(dist), N_BUCKETS, dtype=F32)
    return jnp.einsum("...k,kh->h...", onehot, table, precision=HIGHEST)


def _adaln_kernel(c_ref, w_ref, b_ref, o_ref):
    c = c_ref[...]
    s = (c * jax.nn.sigmoid(c)).astype(BF16)
    o_ref[0] = jnp.dot(s, w_ref[0].astype(BF16), preferred_element_type=F32) + b_ref[0]


def adaln_all(c_all, ada_w, ada_b):
    r = c_all.shape[0]
    n_out = ada_w.shape[2]
    tn = 1024
    return pl.pallas_call(
        _adaln_kernel,
        grid=(DEPTH, n_out // tn),
        in_specs=[
            pl.BlockSpec((r, D_MODEL), lambda l, j: (0, 0)),
            pl.BlockSpec((1, D_MODEL, tn), lambda l, j: (l, 0, j)),
            pl.BlockSpec((1, 1, tn), lambda l, j: (l, 0, j)),
        ],
        out_specs=pl.BlockSpec((1, r, tn), lambda l, j: (l, 0, j)),
        out_shape=jax.ShapeDtypeStruct((DEPTH, r, n_out), F32),
        compiler_params=_cparams("parallel", "parallel"),
    )(c_all, ada_w, ada_b.reshape(DEPTH, 1, n_out))


def _modnorm(x, g, scale, shift):
    y = x * lax.rsqrt(jnp.mean(x * x, axis=-1, keepdims=True) + RMS_EPS)
    return (y * g) * (1.0 + scale) + shift


def _mod_spec(tm_rows, t_mod):
    if t_mod == 1:
        return pl.BlockSpec((1, 1, D_MODEL), lambda b, i, *_: (b, 0, 0))
    return pl.BlockSpec((1, tm_rows, D_MODEL), lambda b, i, *_: (b, i, 0))


def _seg_ones():
    return jnp.kron(jnp.eye(LANES // HEAD_DIM, dtype=F32), jnp.ones((HEAD_DIM, HEAD_DIM), F32))


def _head_rms(z, seg_ref, gain):
    ss = jnp.dot(z * z, seg_ref[...], preferred_element_type=F32, precision=HIGHEST)
    return z * lax.rsqrt(ss * (1.0 / HEAD_DIM) + RMS_EPS) * gain


def _ffn_kernel(x_ref, sh_ref, sc_ref, gt_ref, g_ref, wa_ref, wg_ref, wo_ref, o_ref, xn_ref, acc_ref):
    j = pl.program_id(2)

    @pl.when(j == 0)
    def _():
        xn_ref[...] = _modnorm(x_ref[0], g_ref[...], sc_ref[0], sh_ref[0]).astype(BF16)
        acc_ref[...] = jnp.zeros_like(acc_ref)

    xn = xn_ref[...]
    a = jnp.dot(xn, wa_ref[...], preferred_element_type=F32)
    g = jnp.dot(xn, wg_ref[...], preferred_element_type=F32)
    h = ((g * jax.nn.sigmoid(g)) * a).astype(BF16)
    acc_ref[...] += jnp.dot(h, wo_ref[...], preferred_element_type=F32)

    @pl.when(j == pl.num_programs(2) - 1)
    def _():
        o_ref[0] = x_ref[0] + (0.5 * gt_ref[0]) * acc_ref[...]


def ffn_half(x, shift, scale, gate, g, w_in, w_out):
    b, t, _ = x.shape
    tm = min(t, 512)
    fc = D_FF // 2
    nf = D_FF // fc
    mspec = _mod_spec(tm, shift.shape[1])
    return pl.pallas_call(
        _ffn_kernel,
        grid=(b, t // tm, nf),
        in_specs=[
            pl.BlockSpec((1, tm, D_MODEL), lambda bb, i, j: (bb, i, 0)),
            mspec, mspec, mspec,
            pl.BlockSpec((1, D_MODEL), lambda bb, i, j: (0, 0)),
            pl.BlockSpec((D_MODEL, fc), lambda bb, i, j: (0, j)),
            pl.BlockSpec((D_MODEL, fc), lambda bb, i, j: (0, j + nf)),
            pl.BlockSpec((fc, D_MODEL), lambda bb, i, j: (j, 0)),
        ],
        out_specs=pl.BlockSpec((1, tm, D_MODEL), lambda bb, i, j: (bb, i, 0)),
        out_shape=jax.ShapeDtypeStruct(x.shape, F32),
        scratch_shapes=[pltpu.VMEM((tm, D_MODEL), BF16), pltpu.VMEM((tm, D_MODEL), F32)],
        compiler_params=_cparams("parallel", "parallel", "arbitrary"),
    )(x, shift, scale, gate, g.reshape(1, D_MODEL), w_in, w_in, w_out)


def _proj_kernel(modes, outs, x_ref, sh_ref, sc_ref, g_ref, w_ref, gain_ref, seg_ref, *o_refs):
    h = _modnorm(x_ref[0], g_ref[...], sc_ref[0], sh_ref[0])
    z = jnp.dot(h.astype(BF16), w_ref[...], preferred_element_type=F32)
    chunks = []
    for c, mode in enumerate(modes):
        zc = z[:, c * LANES:(c + 1) * LANES]
        if mode == "norm":
            zc = _head_rms(zc, seg_ref, gain_ref[:, c * LANES:(c + 1) * LANES])
        elif mode == "sigmoid":
            zc = jax.nn.sigmoid(zc)
        chunks.append(zc)
    for o_ref, out in zip(o_refs, outs):
        if out[0] == "cols":
            _, start, width = out
            for c in range(width // LANES):
                o_ref[0, :, c * LANES:(c + 1) * LANES] = chunks[start // LANES + c]
        else:
            _, start, n_heads, dtype, scale = out
            for hh in range(n_heads):
                lo = start + hh * HEAD_DIM
                piece = chunks[lo // LANES][:, lo % LANES:lo % LANES + HEAD_DIM]
                o_ref[0, hh] = (piece * scale).astype(dtype)


def mixer_project(x, shift, scale, g, w, gain_cols, modes, outs):
    b, t, _ = x.shape
    tm = min(t, 512)
    wp = w.shape[1]
    mspec = _mod_spec(tm, shift.shape[1])
    out_shapes, out_specs = [], []
    for out in outs:
        if out[0] == "cols":
            out_shapes.append(jax.ShapeDtypeStruct((b, t, out[2]), F32))
            out_specs.append(pl.BlockSpec((1, tm, out[2]), lambda bb, i: (bb, i, 0)))
        else:
            out_shapes.append(jax.ShapeDtypeStruct((b, out[2], t, HEAD_DIM), out[3]))
            out_specs.append(pl.BlockSpec((1, out[2], tm, HEAD_DIM), lambda bb, i: (bb, 0, i, 0)))
    return pl.pallas_call(
        functools.partial(_proj_kernel, modes, outs),
        grid=(b, t // tm),
        in_specs=[
            pl.BlockSpec((1, tm, D_MODEL), lambda bb, i: (bb, i, 0)),
            mspec, mspec,
            pl.BlockSpec((1, D_MODEL), lambda bb, i: (0, 0)),
            pl.BlockSpec((D_MODEL, wp), lambda bb, i: (0, 0)),
            pl.BlockSpec((1, wp), lambda bb, i: (0, 0)),
            pl.BlockSpec((LANES, LANES), lambda bb, i: (0, 0)),
        ],
        out_specs=out_specs,
        out_shape=out_shapes,
        compiler_params=_cparams("parallel", "parallel"),
    )(x, shift, scale, g.reshape(1, D_MODEL), w, gain_cols, _seg_ones())


def _merge_kernel(n_pieces, y_ref, gt_ref, w_ref, *refs):
    out_ref = refs[n_pieces]
    heads = [refs[k][0, hh] for k in range(n_pieces) for hh in range(refs[k].shape[1])]
    o = jnp.concatenate(heads, axis=1).astype(BF16)
    out_ref[0] = y_ref[0] + gt_ref[0] * jnp.dot(o, w_ref[...], preferred_element_type=F32)


def mixer_merge(y, pieces, gate, w_out):
    b, t, _ = y.shape
    tm = min(t, 512)
    piece_specs = [pl.BlockSpec((1, p.shape[1], tm, HEAD_DIM), lambda bb, i: (bb, 0, i, 0)) for p in pieces]
    return pl.pallas_call(
        functools.partial(_merge_kernel, len(pieces)),
        grid=(b, t // tm),
        in_specs=[
            pl.BlockSpec((1, tm, D_MODEL), lambda bb, i: (bb, i, 0)),
            _mod_spec(tm, gate.shape[1]),
            pl.BlockSpec((D_MODEL, D_MODEL), lambda bb, i: (0, 0)),
        ] + piece_specs,
        out_specs=pl.BlockSpec((1, tm, D_MODEL), lambda bb, i: (bb, i, 0)),
        out_shape=jax.ShapeDtypeStruct(y.shape, F32),
        compiler_params=_cparams("parallel", "parallel"),
    )(y, gate, w_out, *pieces)


TQ = 128
CHUNK_TILES = 4
CK = CHUNK_TILES * TQ


def _bias_tiles(table, t):
    d = jnp.arange(t)[:, None] - jnp.arange(t)[None, :]
    diag = jnp.where(d >= 0, _rel_bias(d, table), NEG_INF)
    adj = _rel_bias(t + d, table)
    far = _rel_bias(2 * t + d, table)
    masked = jnp.full_like(far, NEG_INF)
    edge = jnp.where(d < 0, far, NEG_INF)
    return jnp.stack([diag, adj, far, masked, edge], axis=1)


def _tile_kind(d):
    return jnp.where(d < 0, 3, jnp.minimum(d, 2))


def _loop(n, body, init, static):
    if static:
        for c in range(n):
            init = body(c, init)
        return init
    return lax.fori_loop(0, n, body, init)


def _attend(qb, n_chunks, k_of, v_of, add_of, s_ref, static=False):
    rows = qb.shape[0]

    def cols(c, w):
        return pl.ds(c * w if static else pl.multiple_of(c * w, w), w)

    def score(c, mx):
        s = lax.dot_general(qb, k_of(c), NT_DIMS, preferred_element_type=F32) + add_of(c)
        w = s.shape[1]
        s_ref[:, cols(c, w)] = s
        for part in range(w // LANES):
            mx = jnp.maximum(mx, s[:, part * LANES:(part + 1) * LANES])
        return mx

    mx = _loop(n_chunks, score, jnp.full((rows, LANES), M_INIT, F32), static)
    m = jnp.max(mx, axis=1, keepdims=True)

    def accumulate(c, acc):
        v = v_of(c)
        w = v.shape[0]
        p = jnp.exp(s_ref[:, cols(c, w)] - m).astype(BF16)
        return acc + jnp.dot(p, jnp.concatenate([v, jnp.ones_like(v)], axis=1), preferred_element_type=F32)

    acc = _loop(n_chunks, accumulate, jnp.zeros((rows, 2 * HEAD_DIM), F32), static)
    return acc[:, :HEAD_DIM] / jnp.maximum(acc[:, HEAD_DIM:HEAD_DIM + 1], TINY)


def _top_k_rows(score, allowed, k):
    lane = lax.broadcasted_iota(jnp.int32, score.shape, 1)
    n = score.shape[1]
    remaining = allowed
    picked = jnp.zeros(score.shape, jnp.bool_)
    for _ in range(k):
        cur = jnp.where(remaining, score, -jnp.inf)
        best = jnp.max(cur, axis=1, keepdims=True)
        cand = jnp.logical_and(remaining, cur == best)
        first = jnp.min(jnp.where(cand, lane, n), axis=1, keepdims=True)
        hit = lane == first
        picked = jnp.logical_or(picked, hit)
        remaining = jnp.logical_and(remaining, jnp.logical_not(hit))
    return picked


def _count(mask):
    return jnp.sum(jnp.where(mask, 1.0, 0.0), axis=1, keepdims=True)


def _top_k_mask(score, valid, pos, k):
    bits = pltpu.bitcast(score, jnp.int32)
    key = jnp.where(bits < 0, bits ^ jnp.int32(0x7FFFFFFF), bits)
    key = jnp.where(valid, key, jnp.int32(INT_MIN))
    kf = float(k)

    thr = jnp.full((score.shape[0], 1), INT_MIN, jnp.int32)
    cand = thr ^ jnp.int32(INT_MIN)
    thr = jnp.where(_count(key >= cand) >= kf, cand, thr)

    def step(b, thr):
        cand = thr | lax.shift_left(jnp.int32(1), 30 - b)
        return jnp.where(_count(key >= cand) >= kf, cand, thr)

    thr = lax.fori_loop(0, 31, step, thr)
    above = key > thr
    tie = jnp.logical_and(key == thr, valid)
    need = kf - _count(above)

    n_bits = max(int(score.shape[1] - 1).bit_length(), 1)
    all_pos = jnp.full((score.shape[0], 1), 2 ** n_bits - 1, jnp.int32)

    def tie_search():
        def tie_step(b, cut):
            cand = cut | lax.shift_left(jnp.int32(1), n_bits - 1 - b)
            return jnp.where(_count(jnp.logical_and(tie, pos < cand)) < need, cand, cut)

        return lax.fori_loop(0, n_bits, tie_step, jnp.zeros_like(all_pos))

    surplus = jnp.max(jnp.where(_count(tie) > need, 1.0, 0.0))
    cut = lax.cond(surplus > 0.0, tie_search, lambda: all_pos)
    keep = jnp.logical_or(above, jnp.logical_and(tie, pos <= cut))
    return jnp.logical_and(keep, valid)


def _moba_kernel(q_ref, k_ref, v_ref, bias_ref, o_ref, km_ref, s_ref):
    i = pl.program_id(2)
    nb = km_ref.shape[0]
    t = MOBA_BLOCK

    @pl.when(i == 0)
    def _():
        km_ref[...] = jnp.mean(k_ref[0, 0].reshape(nb, t, HEAD_DIM), axis=1)

    q = q_ref[0, 0]
    gate = lax.dot_general(q.astype(BF16), km_ref[...].astype(BF16), NT_DIMS, preferred_element_type=F32)
    blk = lax.broadcasted_iota(jnp.int32, gate.shape, 1)
    chosen = jnp.logical_or(_top_k_rows(gate, blk < i, MOBA_TOPK), blk == i)
    blk_add = jnp.where(chosen, 0.0, NEG_INF)

    def rows_of(j):
        return pl.ds(pl.multiple_of(j * t, t), t)

    def add_of(j):
        col = jnp.sum(jnp.where(blk == j, blk_add, 0.0), axis=1, keepdims=True)
        return bias_ref[0, jnp.minimum(i - j, 2)] + col

    o_ref[0, 0] = _attend((q * QK_SCALE).astype(BF16), i + 1, lambda j: k_ref[0, 0, rows_of(j), :].astype(BF16),
                          lambda j: v_ref[0, 0, rows_of(j), :], add_of, s_ref)


def moba_prompt(q, k, v, bias):
    b, h, t, _ = q.shape
    tb = MOBA_BLOCK
    nb = t // tb
    full = pl.BlockSpec((1, 1, t, HEAD_DIM), lambda bb, hh, i: (bb, hh, 0, 0))
    return pl.pallas_call(
        _moba_kernel,
        grid=(b, h, nb),
        in_specs=[
            pl.BlockSpec((1, 1, tb, HEAD_DIM), lambda bb, hh, i: (bb, hh, i, 0)),
            full, full,
            pl.BlockSpec((1, 3, tb, tb), lambda bb, hh, i: (hh, 0, 0, 0)),
        ],
        out_specs=pl.BlockSpec((1, 1, tb, HEAD_DIM), lambda bb, hh, i: (bb, hh, i, 0)),
        out_shape=jax.ShapeDtypeStruct(q.shape, F32),
        scratch_shapes=[pltpu.VMEM((nb, HEAD_DIM), F32), pltpu.VMEM((tb, t), F32)],
        compiler_params=_cparams("parallel", "parallel", "arbitrary"),
    )(q, k, v, bias)


N_CHUNK = PAST_LEN // CMP_STRIDE
CHUNK_W = CMP_STRIDE * NSA_KW
N_CMP = (PAST_LEN - CMP_LEN) // CMP_STRIDE + 1
HALF_W = NSA_KV * CMP_HIDDEN


def _compress_weights(pe, w1, w2):
    eye = jnp.eye(NSA_KV, dtype=F32)
    pe_x = jnp.broadcast_to(pe.reshape(2, 2, CMP_STRIDE, 1, HEAD_DIM),
                            (2, 2, CMP_STRIDE, NSA_KV, HEAD_DIM)).reshape(2, 2, CHUNK_W)
    w1_h = w1.reshape(2, 2, CMP_STRIDE, HEAD_DIM, CMP_HIDDEN)
    w1_x = jnp.einsum("thrdj,gq->trgdhqj", w1_h, eye).reshape(2, CHUNK_W, 2 * HALF_W)
    w2_x = jnp.einsum("tjd,gq->tgjqd", w2, eye).reshape(2, HALF_W, NSA_KW)
    return pe_x, w1_x.astype(BF16), w2_x.astype(BF16)


def _compress_tail(ha, hb, w2):
    hid = jax.nn.gelu(ha + pltpu.roll(hb, N_CHUNK - 1, 0))
    return jnp.dot(hid.astype(BF16), w2, preferred_element_type=F32)


def _compress_one(r, pe_ref, w1_ref, w2_ref, t):
    ha = jnp.dot((r + pe_ref[t, 0:1]).astype(BF16), w1_ref[t, :, :HALF_W], preferred_element_type=F32)
    hb = jnp.dot((r + pe_ref[t, 1:2]).astype(BF16), w1_ref[t, :, HALF_W:], preferred_element_type=F32)
    return _compress_tail(ha, hb, w2_ref[t])


def _compress_kernel(rk_ref, rv_ref, pe_ref, w1_ref, w2_ref, gain_ref, seg_ref, ck_ref, cv_ref):
    ck_ref[0] = _head_rms(_compress_one(rk_ref[0], pe_ref, w1_ref, w2_ref, 0), seg_ref, gain_ref[...])
    cv_ref[0] = _compress_one(rv_ref[0], pe_ref, w1_ref, w2_ref, 1)


def nsa_compress_prompt(ck_raw, cv_raw, pe_x, w1_x, w2_x, gain_row):
    b = ck_raw.shape[0]
    rows = pl.BlockSpec((1, N_CHUNK, CHUNK_W), lambda bb: (bb, 0, 0))
    out = pl.BlockSpec((1, N_CHUNK, NSA_KW), lambda bb: (bb, 0, 0))
    const = lambda *shape: pl.BlockSpec(shape, lambda bb: (0,) * len(shape))
    return pl.pallas_call(
        _compress_kernel,
        grid=(b,),
        in_specs=[rows, rows, const(2, 2, CHUNK_W), const(2, CHUNK_W, 2 * HALF_W),
                  const(2, HALF_W, NSA_KW), const(1, NSA_KW), const(LANES, LANES)],
        out_specs=[out, out],
        out_shape=[jax.ShapeDtypeStruct((b, N_CHUNK, NSA_KW), F32)] * 2,
        compiler_params=_cparams("parallel"),
    )(ck_raw.reshape(b, N_CHUNK, CHUNK_W), cv_raw.reshape(b, N_CHUNK, CHUNK_W), pe_x, w1_x, w2_x, gain_row,
      _seg_ones())


def _overlap_matrix():
    c = jnp.arange(LANES)[:, None]
    n = jnp.arange(LANES)[None, :]
    return jnp.logical_and(c * CMP_STRIDE < n * SEL_BLOCK + SEL_BLOCK,
                           c * CMP_STRIDE + CMP_LEN > n * SEL_BLOCK).astype(F32)


def _block_expand(n_keys):
    return (jnp.arange(n_keys)[None, :] // SEL_BLOCK == jnp.arange(LANES)[:, None]).astype(BF16)


def _nsa_kernel(q_ref, gt_ref, ck_ref, cv_ref, sk_ref, sv_ref, wk_ref, wv_ref, bias_ref, cb_ref, ov_ref, ex_ref,
                o_ref, madd_ref, s_ref):
    g = pl.program_id(1)
    i = pl.program_id(2)
    hg = NSA_GROUP
    rows = hg * TQ
    nq = madd_ref.shape[0]
    row = lax.broadcasted_iota(jnp.int32, (TQ, LANES), 0)
    lane = lax.broadcasted_iota(jnp.int32, (TQ, LANES), 1)
    qpos = i * TQ + row
    qf = q_ref[0].reshape(rows, HEAD_DIM)

    def group_half(x):
        return jnp.where(g == 0, x[:, :HEAD_DIM], x[:, HEAD_DIM:])

    cmask = jnp.logical_and(qpos - (lane * CMP_STRIDE + CMP_LEN - 1) >= 0, lane < N_CMP)
    cmask_all = jnp.concatenate([cmask] * hg, axis=0)
    lc = lax.dot_general(qf.astype(BF16), group_half(ck_ref[0]).astype(BF16), NT_DIMS, preferred_element_type=F32)
    lc = jnp.where(cmask_all, lc * QK_SCALE + cb_ref[...].reshape(rows, LANES), NEG_INF)
    e = jnp.where(cmask_all, jnp.exp(lc - jnp.max(lc, axis=1, keepdims=True)), 0.0)
    pc = e / jnp.maximum(jnp.sum(e, axis=1, keepdims=True), TINY)
    o_cmp = jnp.dot(pc.astype(BF16), group_half(cv_ref[0]).astype(BF16), preferred_element_type=F32)
    psum = jnp.sum(pc.reshape(hg, TQ, LANES), axis=0)

    imp = jnp.dot(psum.astype(BF16), ov_ref[...], preferred_element_type=F32)
    cur = qpos // SEL_BLOCK
    forced = jnp.logical_or(lane == 0, jnp.logical_or(lane == cur, lane == cur - 1))
    picked = _top_k_rows(jnp.where(forced, SEL_FORCE, imp), lane <= cur, N_SEL)
    pick_b = jnp.where(picked, 1.0, 0.0).astype(BF16)

    key_add = jnp.where(jnp.dot(pick_b, ex_ref[...], preferred_element_type=F32) > 0.5, 0.0, NEG_INF)
    for j in range(nq):
        madd_ref[j] = key_add[:, j * TQ:(j + 1) * TQ]

    qb = (qf * QK_SCALE).astype(BF16)

    def bias_rows(kind):
        return bias_ref[:, kind].reshape(rows, TQ)

    def key_tile(ref, j):
        return ref[0, 0, pl.ds(pl.multiple_of(j * TQ, TQ), TQ), :]

    def keys_of(c):
        return pl.ds(pl.multiple_of(c * CK, CK), CK)

    def slc_add(c):
        tiles = [CHUNK_TILES * c + k for k in range(CHUNK_TILES)]
        mask = jnp.concatenate([madd_ref[j] for j in tiles], axis=1)
        add = jnp.concatenate([bias_rows(_tile_kind(i - j)) for j in tiles], axis=1)
        return add + jnp.concatenate([mask] * hg, axis=0)

    o_slc = _attend(qb, i // CHUNK_TILES + 1, lambda c: sk_ref[0, 0, keys_of(c), :],
                    lambda c: sv_ref[0, 0, keys_of(c), :], slc_add, s_ref)

    n_win = WINDOW // TQ
    win_chunk = (n_win + 2) // 2

    def win_tiles(w):
        first = i - 2 * win_chunk + 1 + w * win_chunk
        return [first + k for k in range(win_chunk)]

    def win_keys(ref):
        return lambda w: jnp.concatenate([key_tile(ref, jnp.maximum(j, 0)) for j in win_tiles(w)], axis=0)

    def win_add(w):
        kinds = []
        for j in win_tiles(w):
            d = i - j
            kind = jnp.where(d == n_win, 4, jnp.where(d > n_win, 3, jnp.minimum(d, 2)))
            kinds.append(jnp.where(j < 0, 3, kind))
        return jnp.concatenate([bias_rows(kind) for kind in kinds], axis=1)

    o_win = _attend(qb, 2, win_keys(wk_ref), win_keys(wv_ref), win_add, s_ref, static=True)

    gt = gt_ref[0]

    def gate_col(branch):
        cols = []
        for h in range(hg):
            c0 = 3 * h + branch
            cols.append(jnp.where(g == 0, gt[:, c0:c0 + 1], gt[:, 3 * hg + c0:3 * hg + c0 + 1]))
        return jnp.concatenate(cols, axis=0)

    o = gate_col(0) * o_cmp + gate_col(1) * o_slc + gate_col(2) * o_win
    o_ref[0] = o.reshape(hg, TQ, HEAD_DIM)


def nsa_prompt(q, gates, ck, cv, sk, sv, wk, wv, bias, cbias):
    b, _, t, _ = q.shape
    hg = NSA_GROUP
    nq = t // TQ
    kv = pl.BlockSpec((1, 1, t, HEAD_DIM), lambda bb, gg, i: (bb, gg, 0, 0))
    cmp_spec = pl.BlockSpec((1, LANES, NSA_KW), lambda bb, gg, i: (bb, 0, 0))
    return pl.pallas_call(
        _nsa_kernel,
        grid=(b, NSA_KV, nq),
        in_specs=[
            pl.BlockSpec((1, hg, TQ, HEAD_DIM), lambda bb, gg, i: (bb, gg, i, 0)),
            pl.BlockSpec((1, TQ, LANES), lambda bb, gg, i: (bb, i, 0)),
            cmp_spec, cmp_spec, kv, kv, kv, kv,
            pl.BlockSpec((hg, 5, TQ, TQ), lambda bb, gg, i: (gg, 0, 0, 0)),
            pl.BlockSpec((hg, TQ, LANES), lambda bb, gg, i: (gg, i, 0)),
            pl.BlockSpec((LANES, LANES), lambda bb, gg, i: (0, 0)),
            pl.BlockSpec((LANES, t), lambda bb, gg, i: (0, 0)),
        ],
        out_specs=pl.BlockSpec((1, hg, TQ, HEAD_DIM), lambda bb, gg, i: (bb, gg, i, 0)),
        out_shape=jax.ShapeDtypeStruct(q.shape, F32),
        scratch_shapes=[pltpu.VMEM((nq, TQ, TQ), F32), pltpu.VMEM((hg * TQ, t), F32)],
        compiler_params=_cparams("parallel", "parallel", "arbitrary"),
    )(q, gates, ck, cv, sk, sv, wk, wv, bias, cbias, _overlap_matrix().astype(BF16), _block_expand(t))


def _dsa_index_kernel(qi_ref, wi_ref, ki_ref, o_ref):
    i = pl.program_id(1)
    t = ki_ref.shape[1]
    nq = t // TQ
    wi = wi_ref[0]

    def select(width):
        ki = ki_ref[0, :width, :IDX_DIM].astype(BF16)
        score = jnp.zeros((TQ, width), F32)
        for h in range(IDX_HEADS):
            s = lax.dot_general(qi_ref[0, h].astype(BF16), ki, NT_DIMS, preferred_element_type=F32)
            score = score + wi[:, h:h + 1] * jnp.maximum(s * IDX_DIM ** -0.5, 0.0)
        score = score * IDX_HEADS ** -0.5
        kpos = lax.broadcasted_iota(jnp.int32, (TQ, width), 1)
        qpos = i * TQ + lax.broadcasted_iota(jnp.int32, (TQ, width), 0)
        keep = _top_k_mask(score, kpos <= qpos, kpos, min(DSA_TOPK, t // 4))
        add = jnp.where(keep, 0.0, NEG_INF).astype(BF16)
        for j in range(nq):
            lo = j * TQ
            o_ref[0, 0, j] = add[:, lo:lo + TQ] if lo < width else jnp.full((TQ, TQ), NEG_INF, BF16)

    widths = [w for w in (t // 4, t // 2, t) if w % TQ == 0 and w >= TQ]
    lo_tile = 0
    for w in widths:
        hi_tile = w // TQ
        pl.when(jnp.logical_and(i >= lo_tile, i < hi_tile))(functools.partial(select, w))
        lo_tile = hi_tile


def dsa_index_prompt(qi, wi, ki):
    b, ih, t, _ = qi.shape
    nq = t // TQ
    return pl.pallas_call(
        _dsa_index_kernel,
        grid=(b, nq),
        in_specs=[
            pl.BlockSpec((1, ih, TQ, IDX_DIM), lambda bb, i: (bb, 0, i, 0)),
            pl.BlockSpec((1, TQ, LANES), lambda bb, i: (bb, i, 0)),
            pl.BlockSpec((1, t, LANES), lambda bb, i: (bb, 0, 0)),
        ],
        out_specs=pl.BlockSpec((1, 1, nq, TQ, TQ), lambda bb, i: (bb, i, 0, 0, 0)),
        out_shape=jax.ShapeDtypeStruct((b, nq, nq, TQ, TQ), BF16),
        compiler_params=_cparams("parallel", "parallel"),
    )(qi, wi, ki)


def _dsa_attn_kernel(q_ref, k_ref, v_ref, bias_ref, m_ref, o_ref, s_ref):
    i = pl.program_id(2)
    hg = DSA_GROUP
    rows = hg * TQ

    def keys_of(c):
        return pl.ds(pl.multiple_of(c * CK, CK), CK)

    def add_of(c):
        tiles = [CHUNK_TILES * c + k for k in range(CHUNK_TILES)]
        mask = jnp.concatenate([m_ref[0, 0, j] for j in tiles], axis=1).astype(F32)
        add = jnp.concatenate([bias_ref[:, _tile_kind(i - j)].reshape(rows, TQ) for j in tiles], axis=1)
        return add + jnp.concatenate([mask] * hg, axis=0)

    o = _attend(q_ref[0].reshape(rows, HEAD_DIM), i // CHUNK_TILES + 1, lambda c: k_ref[0, 0, keys_of(c), :],
                lambda c: v_ref[0, 0, keys_of(c), :], add_of, s_ref)
    o_ref[0] = o.reshape(hg, TQ, HEAD_DIM)


def dsa_attn_prompt(q, k, v, bias, mask):
    b, _, t, _ = q.shape
    hg = DSA_GROUP
    nq = t // TQ
    kv = pl.BlockSpec((1, 1, t, HEAD_DIM), lambda bb, gg, i: (bb, gg, 0, 0))
    return pl.pallas_call(
        _dsa_attn_kernel,
        grid=(b, DSA_KV, nq),
        in_specs=[
            pl.BlockSpec((1, hg, TQ, HEAD_DIM), lambda bb, gg, i: (bb, gg, i, 0)),
            kv, kv,
            pl.BlockSpec((hg, 5, TQ, TQ), lambda bb, gg, i: (gg, 0, 0, 0)),
            pl.BlockSpec((1, 1, nq, TQ, TQ), lambda bb, gg, i: (bb, i, 0, 0, 0)),
        ],
        out_specs=pl.BlockSpec((1, hg, TQ, HEAD_DIM), lambda bb, gg, i: (bb, gg, i, 0)),
        out_shape=jax.ShapeDtypeStruct(q.shape, F32),
        scratch_shapes=[pltpu.VMEM((hg * TQ, t), F32)],
        compiler_params=_cparams("parallel", "parallel", "arbitrary"),
    )(q, k, v, bias, mask)


REQ = 2


def _row_spec(*shape):
    return pl.BlockSpec((REQ,) + shape, lambda b, pt: (b,) + (0,) * len(shape))


def _const_spec(*shape):
    return pl.BlockSpec(shape, lambda b, pt: (0,) * len(shape))


def _decode_call(kernel, n_req, page_table, in_specs, args, out_shapes, out_specs):
    return pl.pallas_call(
        kernel,
        grid_spec=pltpu.PrefetchScalarGridSpec(num_scalar_prefetch=1, grid=(n_req // REQ,), in_specs=in_specs,
                                               out_specs=out_specs),
        out_shape=out_shapes,
        compiler_params=_cparams("parallel"),
    )(page_table, *args)


def _page_specs(lane_block, width):
    return [pl.BlockSpec((1, PAGE_SIZE, width),
                         functools.partial(lambda r, p, b, pt: (pt[b * REQ + r, p], 0, lane_block), r, p))
            for r in range(REQ) for p in range(N_PAGES)]


def _request_pages(refs, n_tensors, r):
    per = REQ * N_PAGES
    return [refs[t * per + r * N_PAGES:t * per + (r + 1) * N_PAGES] for t in range(n_tensors)]


def _paged_attention(qblk, k_pages, v_pages, bias_ref, mask_of_page, k_new, v_new, new_add):
    qb = (qblk * QK_SCALE).astype(BF16)
    s_new = jnp.sum(qblk * k_new, axis=1, keepdims=True) * QK_SCALE + new_add
    s_pages = []
    for p in range(N_PAGES):
        s = lax.dot_general(qb, k_pages[p][0].astype(BF16), NT_DIMS, preferred_element_type=F32)
        s_pages.append(s + bias_ref[:, p * PAGE_SIZE:(p + 1) * PAGE_SIZE] + mask_of_page(p))
    m = s_new
    for s in s_pages:
        m = jnp.maximum(m, jnp.max(s, axis=1, keepdims=True))
    e_new = jnp.exp(s_new - m)
    l = e_new
    acc = e_new * v_new
    for p in range(N_PAGES):
        e = jnp.exp(s_pages[p] - m)
        l = l + jnp.sum(e, axis=1, keepdims=True)
        acc = acc + jnp.dot(e.astype(BF16), v_pages[p][0].astype(BF16), preferred_element_type=F32)
    return acc / jnp.maximum(l, TINY)


def _group_lanes(o_full, heads_per_group):
    hp, w = o_full.shape
    grp = lax.broadcasted_iota(jnp.int32, (hp, HEAD_DIM), 0) // heads_per_group
    out = o_full[:, :HEAD_DIM]
    for gg in range(1, w // HEAD_DIM):
        out = jnp.where(grp == gg, o_full[:, gg * HEAD_DIM:(gg + 1) * HEAD_DIM], out)
    return out


MOBA_HP = 8


def _moba_decode_kernel(pt_ref, q_ref, kn_ref, vn_ref, bias_ref, bnew_ref, *refs):
    o_ref = refs[2 * REQ * N_PAGES]
    for r in range(REQ):
        k_pages, v_pages = _request_pages(refs, 2, r)
        o_ref[r] = _moba_decode_one(q_ref[r], kn_ref[r], vn_ref[r], bias_ref, bnew_ref, k_pages, v_pages)


def _moba_decode_one(qblk, k_new, v_new, bias_ref, bnew_ref, k_pages, v_pages):
    pages_per_block = MOBA_BLOCK // PAGE_SIZE
    n_blk = N_PAGES // pages_per_block
    means = []
    for blk in range(n_blk):
        tot = jnp.sum(k_pages[blk * pages_per_block][0], axis=0, keepdims=True)
        for p in range(blk * pages_per_block + 1, (blk + 1) * pages_per_block):
            tot = tot + jnp.sum(k_pages[p][0], axis=0, keepdims=True)
        means.append(tot * (1.0 / MOBA_BLOCK))
    k_mean = jnp.concatenate(means, axis=0)
    gate = lax.dot_general(qblk, k_mean, NT_DIMS, preferred_element_type=F32, precision=HIGHEST)
    chosen = _top_k_rows(gate, jnp.ones(gate.shape, jnp.bool_), MOBA_TOPK)
    blk_add = jnp.where(chosen, 0.0, NEG_INF)

    def mask_of_page(p):
        b0 = p // pages_per_block
        return blk_add[:, b0:b0 + 1]

    o_full = _paged_attention(qblk, k_pages, v_pages, bias_ref, mask_of_page, k_new, v_new, bnew_ref[:, 0:1])
    return _group_lanes(o_full, 1)


def moba_decode(page_table, qblk, k_new, v_new, bias, bnew, cache_k, cache_v, layer):
    n_req = qblk.shape[0]
    in_specs = ([_row_spec(MOBA_HP, MOBA_W), _row_spec(1, MOBA_W), _row_spec(1, MOBA_W),
                 _const_spec(MOBA_HP, PAST_LEN), _const_spec(MOBA_HP, LANES)]
                + _page_specs(layer, MOBA_W) * 2)
    args = [qblk, k_new, v_new, bias, bnew] + [cache_k] * (REQ * N_PAGES) + [cache_v] * (REQ * N_PAGES)
    return _decode_call(_moba_decode_kernel, n_req, page_table, in_specs, args,
                        jax.ShapeDtypeStruct((n_req, MOBA_HP, HEAD_DIM), F32), _row_spec(MOBA_HP, HEAD_DIM))


NSA_HP = 16
GRP_ROWS = 8


def _chunk_rows(pages):
    per_page = PAGE_SIZE // CMP_STRIDE
    return jnp.concatenate(
        [jnp.concatenate([pg[0, pl.ds(r, per_page, stride=CMP_STRIDE), :] for pg in pages], axis=0)
         for r in range(CMP_STRIDE)], axis=1)


def _nsa_decode_kernel(pt_ref, q_ref, gt_ref, skn_ref, svn_ref, wkn_ref, wvn_ref, wkb_ref, wvb_ref,
                       pe_ref, w1_ref, w2_ref, gain_ref, seg_ref, cb_ref, sb_ref, wb_ref, bnew_ref,
                       ov_ref, ex_ref, g2_ref, p16_ref, *refs):
    o_ref = refs[4 * REQ * N_PAGES]
    consts = (pe_ref, w1_ref, w2_ref, gain_ref, seg_ref, cb_ref, sb_ref, wb_ref, bnew_ref, ov_ref, ex_ref, g2_ref,
              p16_ref)
    for r in range(REQ):
        o_ref[r] = _nsa_decode_one(q_ref[r], gt_ref[r], skn_ref[r], svn_ref[r], wkn_ref[r], wvn_ref[r], wkb_ref[r],
                                   wvb_ref[r], consts, _request_pages(refs, 4, r))


def _nsa_decode_one(qblk, gt, sk_new, sv_new, wk_new, wv_new, wk_buf, wv_buf, consts, pages):
    pe_ref, w1_ref, w2_ref, gain_ref, seg_ref, cb_ref, sb_ref, wb_ref, bnew_ref, ov_ref, ex_ref, g2_ref, p16_ref = consts
    ck_pages, cv_pages, sk_pages, sv_pages = pages
    new_add = bnew_ref[:, 0:1]
    lane = lax.broadcasted_iota(jnp.int32, (GRP_ROWS, LANES), 1)

    ck = _head_rms(_compress_one(_chunk_rows(ck_pages), pe_ref, w1_ref, w2_ref, 0), seg_ref, gain_ref[...])
    cv = _compress_one(_chunk_rows(cv_pages), pe_ref, w1_ref, w2_ref, 1)
    cvalid = lax.broadcasted_iota(jnp.int32, (NSA_HP, LANES), 1) < N_CMP
    lc = lax.dot_general(qblk, ck, NT_DIMS, preferred_element_type=F32, precision=HIGHEST)
    lc = jnp.where(cvalid, lc * QK_SCALE + cb_ref[...], NEG_INF)
    e = jnp.where(cvalid, jnp.exp(lc - jnp.max(lc, axis=1, keepdims=True)), 0.0)
    pc = e / jnp.maximum(jnp.sum(e, axis=1, keepdims=True), TINY)
    o_cmp = jnp.dot(pc.astype(BF16), cv.astype(BF16), preferred_element_type=F32)

    psum = jnp.dot(g2_ref[...], pc, preferred_element_type=F32, precision=HIGHEST)
    imp = jnp.dot(psum, ov_ref[...], preferred_element_type=F32, precision=HIGHEST)
    cur = PAST_LEN // SEL_BLOCK
    forced = jnp.logical_or(lane == 0, jnp.logical_or(lane == cur, lane == cur - 1))
    picked = _top_k_rows(jnp.where(forced, SEL_FORCE, imp), lane <= cur, N_SEL)
    pick_h = jnp.dot(p16_ref[...], jnp.where(picked, 1.0, 0.0).astype(BF16), preferred_element_type=F32)
    key_hit = jnp.dot(pick_h.astype(BF16), ex_ref[...], preferred_element_type=F32)
    key_add = jnp.where(key_hit > 0.5, 0.0, NEG_INF)

    o_slc = _paged_attention(qblk, sk_pages, sv_pages, sb_ref, lambda p: key_add[:, p * PAGE_SIZE:(p + 1) * PAGE_SIZE],
                             sk_new, sv_new, new_add)

    qb = (qblk * QK_SCALE).astype(BF16)
    sw = lax.dot_general(qb, wk_buf.astype(BF16), NT_DIMS, preferred_element_type=F32) + wb_ref[...]
    s_new = jnp.sum(qblk * wk_new, axis=1, keepdims=True) * QK_SCALE + new_add
    m = jnp.maximum(s_new, jnp.max(sw, axis=1, keepdims=True))
    ew = jnp.exp(sw - m)
    e_new = jnp.exp(s_new - m)
    l = e_new + jnp.sum(ew, axis=1, keepdims=True)
    o_win = e_new * wv_new + jnp.dot(ew.astype(BF16), wv_buf.astype(BF16), preferred_element_type=F32)
    o_win = o_win / jnp.maximum(l, TINY)

    o_full = gt[:, 0:1] * o_cmp + gt[:, 1:2] * o_slc + gt[:, 2:3] * o_win
    return _group_lanes(o_full, NSA_GROUP)


def nsa_decode(page_table, qblk, gates, sk_new, sv_new, wk_new, wv_new, win_k, win_v, cmp_w, gain_row,
               cbias, sbias, wbias, bnew, cache_ck, cache_cv, cache_sk, cache_sv, layer):
    n_req = qblk.shape[0]
    pe_x, w1_x, w2_x = cmp_w
    hh = jnp.arange(NSA_HP)
    g2 = jnp.logical_and(hh[None, :] // NSA_GROUP == jnp.arange(GRP_ROWS)[:, None], hh[None, :] < NSA_HEADS)
    p16 = (hh[:, None] // NSA_GROUP == jnp.arange(GRP_ROWS)[None, :]).astype(BF16)
    win_spec = pl.BlockSpec((REQ, WINDOW, NSA_KW), lambda b, pt: (b, 0, layer))
    in_specs = ([_row_spec(NSA_HP, NSA_KW), _row_spec(NSA_HP, LANES)] + [_row_spec(1, NSA_KW)] * 4
                + [win_spec, win_spec,
                   _const_spec(2, 2, CHUNK_W), _const_spec(2, CHUNK_W, 2 * HALF_W), _const_spec(2, HALF_W, NSA_KW),
                   _const_spec(1, NSA_KW), _const_spec(LANES, LANES),
                   _const_spec(NSA_HP, LANES), _const_spec(NSA_HP, PAST_LEN), _const_spec(NSA_HP, WINDOW),
                   _const_spec(NSA_HP, LANES), _const_spec(LANES, LANES), _const_spec(LANES, PAST_LEN),
                   _const_spec(GRP_ROWS, NSA_HP), _const_spec(NSA_HP, GRP_ROWS)]
                + _page_specs(layer, NSA_KW) * 4)
    args = ([qblk, gates, sk_new, sv_new, wk_new, wv_new, win_k, win_v, pe_x, w1_x, w2_x, gain_row, _seg_ones(),
             cbias, sbias, wbias, bnew, _overlap_matrix(), _block_expand(PAST_LEN), g2.astype(F32), p16]
            + [c for c in (cache_ck, cache_cv, cache_sk, cache_sv) for _ in range(REQ * N_PAGES)])
    return _decode_call(_nsa_decode_kernel, n_req, page_table, in_specs, args,
                        jax.ShapeDtypeStruct((n_req, NSA_HP, HEAD_DIM), F32), _row_spec(NSA_HP, HEAD_DIM))


KEYS_PAD = PAST_LEN + LANES


def _dsa_score_kernel(layer, pt_ref, qi_ref, wi_ref, kin_ref, *refs):
    o_ref = refs[REQ * N_PAGES]
    lane = lax.broadcasted_iota(jnp.int32, (1, LANES), 1)
    for r in range(REQ):
        pages, = _request_pages(refs, 1, r)
        qi = qi_ref[r]
        wi = wi_ref[r][:, 0:1]
        parts = []
        for p in range(N_PAGES):
            s = lax.dot_general(qi, pages[p][0, :, layer, :], NT_DIMS, preferred_element_type=F32,
                                precision=HIGHEST)
            parts.append(jnp.sum(wi * jnp.maximum(s * IDX_DIM ** -0.5, 0.0), axis=0, keepdims=True))
        s_new = jnp.sum(qi * kin_ref[r], axis=1, keepdims=True) * IDX_DIM ** -0.5
        new = jnp.sum(wi * jnp.maximum(s_new, 0.0), axis=0, keepdims=True)
        parts.append(jnp.where(lane == 0, new, 0.0))
        o_ref[r] = jnp.concatenate(parts, axis=1) * IDX_HEADS ** -0.5


def dsa_score_decode(page_table, qi, wi, ki_new, cache_idx, layer):
    n_req = qi.shape[0]
    n_layers = cache_idx.shape[2]
    pages = [pl.BlockSpec((1, PAGE_SIZE, n_layers, IDX_DIM),
                          functools.partial(lambda r, p, b, pt: (pt[b * REQ + r, p], 0, 0, 0), r, p))
             for r in range(REQ) for p in range(N_PAGES)]
    in_specs = [_row_spec(IDX_HEADS, IDX_DIM), _row_spec(IDX_HEADS, LANES), _row_spec(1, IDX_DIM)] + pages
    args = [qi, wi, ki_new] + [cache_idx] * (REQ * N_PAGES)
    return _decode_call(functools.partial(_dsa_score_kernel, layer), n_req, page_table, in_specs, args,
                        jax.ShapeDtypeStruct((n_req, 1, KEYS_PAD), F32), _row_spec(1, KEYS_PAD))


def _topk_rows_kernel(k, s_ref, o_ref):
    score = s_ref[...]
    pos = lax.broadcasted_iota(jnp.int32, score.shape, 1)
    keep = _top_k_mask(score, pos <= PAST_LEN, pos, k)
    o_ref[...] = jnp.where(keep, 0.0, NEG_INF)


def dsa_topk_decode(score):
    k = min(DSA_TOPK, (PAST_LEN + 1) // 4)
    return pl.pallas_call(
        functools.partial(_topk_rows_kernel, k),
        out_shape=jax.ShapeDtypeStruct(score.shape, F32),
        compiler_params=pltpu.CompilerParams(vmem_limit_bytes=VMEM_LIMIT),
    )(score)


def _dsa_decode_kernel(pt_ref, q_ref, kn_ref, vn_ref, m_ref, bias_ref, bnew_ref, *refs):
    o_ref = refs[2 * REQ * N_PAGES]
    for r in range(REQ):
        k_pages, v_pages = _request_pages(refs, 2, r)
        mask = m_ref[r]
        new_add = bnew_ref[:, 0:1] + mask[:, PAST_LEN:PAST_LEN + 1]
        o_full = _paged_attention(q_ref[r], k_pages, v_pages, bias_ref,
                                  functools.partial(lambda m, p: m[:, p * PAGE_SIZE:(p + 1) * PAGE_SIZE], mask),
                                  kn_ref[r], vn_ref[r], new_add)
        o_ref[r] = _group_lanes(o_full, DSA_GROUP)


def dsa_attn_decode(page_table, qblk, k_new, v_new, mask, bias, bnew, cache_k, cache_v, layer):
    n_req = qblk.shape[0]
    in_specs = ([_row_spec(DSA_HEADS, DSA_KW), _row_spec(1, DSA_KW), _row_spec(1, DSA_KW), _row_spec(1, KEYS_PAD),
                 _const_spec(DSA_HEADS, PAST_LEN), _const_spec(DSA_HEADS, LANES)]
                + _page_specs(layer, DSA_KW) * 2)
    args = [qblk, k_new, v_new, mask, bias, bnew] + [cache_k] * (REQ * N_PAGES) + [cache_v] * (REQ * N_PAGES)
    return _decode_call(_dsa_decode_kernel, n_req, page_table, in_specs, args,
                        jax.ShapeDtypeStruct((n_req, DSA_HEADS, HEAD_DIM), F32), _row_spec(DSA_HEADS, HEAD_DIM))


EVEN_WP = 2432
EVEN_MODES = ("norm",) * 4 + ("raw",) * 2 + ("norm",) * 6 + ("raw", "raw", "norm", "raw", "norm", "raw", "sigmoid")
EVEN_CACHE_OUTS = (("cols", 256, 256), ("cols", 512, 256), ("cols", 1536, 128), ("cols", 1664, 128),
                   ("cols", 1792, 128), ("cols", 1920, 128), ("cols", 2048, 128), ("cols", 2176, 128))
EVEN_OUTS_PROMPT = EVEN_CACHE_OUTS + (
    ("cols", 2304, 128), ("heads", 0, 4, F32, 1.0), ("heads", 256, 4, F32, 1.0), ("heads", 512, 4, BF16, 1.0),
    ("heads", 768, 12, F32, 1.0), ("heads", 1792, 2, BF16, 1.0), ("heads", 1920, 2, BF16, 1.0),
    ("heads", 2048, 2, BF16, 1.0), ("heads", 2176, 2, BF16, 1.0))
EVEN_OUTS_DECODE = EVEN_CACHE_OUTS + (("cols", 2304, 128), ("cols", 0, 256), ("cols", 768, 768))
ODD_WP = 2304
ODD_MODES = ("norm",) * 10 + ("raw",) * 8
ODD_CACHE_OUTS = (("cols", 1024, 256), ("cols", 1280, 256), ("cols", 2048, 128))
ODD_OUTS_PROMPT = ODD_CACHE_OUTS + (
    ("cols", 2176, 128), ("heads", 0, 16, BF16, QK_SCALE), ("heads", 1024, 4, BF16, 1.0),
    ("heads", 1280, 4, BF16, 1.0), ("heads", 1536, 8, F32, 1.0))
ODD_OUTS_DECODE = ODD_CACHE_OUTS + (("cols", 2176, 128), ("cols", 0, 1024), ("cols", 1536, 512))


def _even_weights(w_in, moba_g, nsa_g):
    w = jnp.pad(w_in, ((0, 0), (0, EVEN_WP - w_in.shape[1]))).astype(BF16)
    z = lambda n: jnp.zeros((n,), F32)
    gain = jnp.concatenate([jnp.tile(moba_g[0], 4), jnp.tile(moba_g[1], 4), z(256), jnp.tile(nsa_g[0], 12), z(256),
                            jnp.tile(nsa_g[2], 2), z(128), jnp.tile(nsa_g[3], 2), z(256)])
    return w, gain.reshape(1, EVEN_WP)


def _odd_weights(w_in, qk_g):
    zc = lambda n: jnp.zeros((D_MODEL, n), F32)
    w = jnp.concatenate([w_in[:, :2112], zc(64), w_in[:, 2112:], zc(ODD_WP - 2176 - IDX_HEADS)], axis=1).astype(BF16)
    gain = jnp.concatenate([jnp.tile(qk_g[0], 16), jnp.tile(qk_g[1], 4), jnp.zeros((ODD_WP - 1280,), F32)])
    return w, gain.reshape(1, ODD_WP)


def _block_queries(q, n_heads, heads_per_group, rows):
    b = q.shape[0]
    n_groups = n_heads // heads_per_group
    qh = q.reshape(b, n_heads, 1, HEAD_DIM)
    grp = (jnp.arange(n_heads)[:, None] // heads_per_group == jnp.arange(n_groups)[None, :]).astype(F32)
    blk = (qh * grp[None, :, :, None]).reshape(b, n_heads, n_groups * HEAD_DIM)
    return jnp.pad(blk, ((0, 0), (0, rows - n_heads), (0, 0)))


def _decode_heads(o, n_heads):
    return o[:, :n_heads].transpose(1, 0, 2)[None]


def kernel(x_prompt, x_sample, cache_moba_k, cache_moba_v, cache_nsa_cmp_k, cache_nsa_cmp_v, cache_nsa_slc_k,
           cache_nsa_slc_v, state_nsa_win_k, state_nsa_win_v, cache_dsa_k, cache_dsa_v, cache_dsa_idx_k, page_table,
           c_prompt, c_sample, bias_table, norm_gain, ada_w, ada_b, ffn_w_in, ffn_w_out, even_w_in, even_w_out,
           moba_qk_gain, nsa_qk_gain, nsa_cmp_pe, nsa_cmp_w1, nsa_cmp_w2, odd_w_in, odd_w_out, dsa_qk_gain):
    bp, t, _ = x_prompt.shape
    bs = x_sample.shape[0]

    tab_m, tab_n = bias_table[:, :MOBA_HEADS], bias_table[:, MOBA_HEADS:]
    bias_m = _bias_tiles(tab_m, MOBA_BLOCK)[:, :3]
    bias_n = _bias_tiles(tab_n, TQ)
    bias_d = _bias_tiles(bias_table, TQ)
    cmp_end = jnp.arange(LANES) * CMP_STRIDE + CMP_LEN - 1
    cbias = _rel_bias(jnp.arange(t)[:, None] - cmp_end[None, :], tab_n)
    pad_rows = lambda a, rows: jnp.pad(a, ((0, rows - a.shape[0]), (0, 0)))
    past_dist = PAST_LEN - jnp.arange(PAST_LEN)
    dec_bias_m = pad_rows(_rel_bias(past_dist, tab_m), MOBA_HP)
    dec_bias_n = pad_rows(_rel_bias(past_dist, tab_n), NSA_HP)
    dec_bias_d = _rel_bias(past_dist, bias_table)
    dec_cbias = pad_rows(_rel_bias(PAST_LEN - cmp_end, tab_n), NSA_HP)
    win_dist = WINDOW - jnp.arange(WINDOW)
    dec_wbias = pad_rows(jnp.where(win_dist < WINDOW, _rel_bias(win_dist, tab_n), NEG_INF), NSA_HP)
    new_bias = jnp.broadcast_to(bias_table[0][:, None], (DSA_HEADS, LANES))
    bnew_m = pad_rows(new_bias[:MOBA_HEADS], MOBA_HP)
    bnew_n = pad_rows(new_bias[MOBA_HEADS:], NSA_HP)

    flat = lambda c: c.reshape(c.shape[0], c.shape[1], -1)
    pool_mk, pool_mv = flat(cache_moba_k), flat(cache_moba_v)
    pool_ck, pool_cv = flat(cache_nsa_cmp_k), flat(cache_nsa_cmp_v)
    pool_sk, pool_sv = flat(cache_nsa_slc_k), flat(cache_nsa_slc_v)
    pool_dk, pool_dv = flat(cache_dsa_k), flat(cache_dsa_v)
    win_k_all, win_v_all = flat(state_nsa_win_k), flat(state_nsa_win_v)

    mods = adaln_all(jnp.concatenate([c_prompt, c_sample], axis=0), ada_w, ada_b)
    mods = mods.reshape(DEPTH, bp + bs, 3, 3, D_MODEL)
    ffn_in = ffn_w_in.astype(BF16)
    ffn_out = ffn_w_out.astype(BF16)

    yp = x_prompt
    ys = x_sample.reshape(1, bs, D_MODEL)
    rows_ep = [[] for _ in range(8)]
    rows_es = [[] for _ in range(8)]
    rows_op = [[] for _ in range(3)]
    rows_os = [[] for _ in range(3)]
    for li in range(DEPTH):
        mp = mods[li, :bp][:, None]
        ms = mods[li, bp:][None]
        mod = lambda m, s, k: m[:, :, s, k]

        def ffn(y, m, s, w_idx):
            return ffn_half(y, mod(m, s, 0), mod(m, s, 1), mod(m, s, 2), norm_gain[li, s], ffn_in[li, w_idx],
                            ffn_out[li, w_idx])

        yp, ys = ffn(yp, mp, 0, 0), ffn(ys, ms, 0, 0)
        if li % 2 == 0:
            e = li // 2
            w_p, gain_cols = _even_weights(even_w_in[e], moba_qk_gain[e], nsa_qk_gain[e])
            w_o = even_w_out[e].astype(BF16)
            cmp_w = _compress_weights(nsa_cmp_pe[e], nsa_cmp_w1[e], nsa_cmp_w2[e])
            cmp_gain = jnp.tile(nsa_qk_gain[e, 1], NSA_KV).reshape(1, NSA_KW)
            project = lambda y, m, outs: mixer_project(y, mod(m, 1, 0), mod(m, 1, 1), norm_gain[li, 1], w_p,
                                                       gain_cols, EVEN_MODES, outs)
            pp = project(yp, mp, EVEN_OUTS_PROMPT)
            gl, mq_h, mk_h, mv_h, nq_h, sk_h, sv_h, wk_h, wv_h = pp[8:]
            o_m = moba_prompt(mq_h, mk_h, mv_h, bias_m)
            ck, cv = nsa_compress_prompt(pp[2], pp[3], *cmp_w, cmp_gain)
            o_n = nsa_prompt(nq_h, gl, ck, cv, sk_h, sv_h, wk_h, wv_h, bias_n, cbias)
            pieces_p = [o_m, o_n]
            wb = min(WINDOW, t)
            new_p = list(pp[:6]) + [pp[6][:, t - wb:], pp[7][:, t - wb:]]
            ps = [a.reshape(bs, 1, a.shape[-1]) for a in project(ys, ms, EVEN_OUTS_DECODE)]
            mk_s, mv_s, ckr_s, cvr_s, sk_s, sv_s, wk_s, wv_s, gl_s, mq_s, nq_s = ps
            o_ms = moba_decode(page_table, _block_queries(mq_s[:, 0], MOBA_HEADS, 1, MOBA_HP), mk_s, mv_s,
                               dec_bias_m, bnew_m, pool_mk, pool_mv, e)
            gates_s = jnp.pad(gl_s[:, 0, :3 * NSA_HEADS].reshape(bs, NSA_HEADS, 3),
                              ((0, 0), (0, NSA_HP - NSA_HEADS), (0, LANES - 3)))
            o_ns = nsa_decode(page_table, _block_queries(nq_s[:, 0], NSA_HEADS, NSA_GROUP, NSA_HP), gates_s,
                              sk_s, sv_s, wk_s, wv_s, win_k_all, win_v_all, cmp_w, cmp_gain,
                              dec_cbias, dec_bias_n, dec_wbias, bnew_n, pool_ck, pool_cv, pool_sk, pool_sv, e)
            pieces_s = [_decode_heads(o_ms, MOBA_HEADS), _decode_heads(o_ns, NSA_HEADS)]
            keep = min(WINDOW, PAST_LEN + 1)
            lanes_e = slice(e * NSA_KW, (e + 1) * NSA_KW)
            new_s = [mk_s, mv_s, ckr_s, cvr_s, sk_s, sv_s,
                     jnp.concatenate([win_k_all[:, :, lanes_e], wk_s], axis=1)[:, -keep:],
                     jnp.concatenate([win_v_all[:, :, lanes_e], wv_s], axis=1)[:, -keep:]]
            for acc, r in zip(rows_ep, new_p):
                acc.append(r)
            for acc, r in zip(rows_es, new_s):
                acc.append(r)
        else:
            o = li // 2
            w_p, gain_cols = _odd_weights(odd_w_in[o], dsa_qk_gain[o])
            w_o = odd_w_out[o].astype(BF16)
            project = lambda y, m, outs: mixer_project(y, mod(m, 1, 0), mod(m, 1, 1), norm_gain[li, 1], w_p,
                                                       gain_cols, ODD_MODES, outs)
            pp = project(yp, mp, ODD_OUTS_PROMPT)
            wi, q_h, k_h, v_h, qi_h = pp[3:]
            keep_mask = dsa_index_prompt(qi_h, wi, pp[2])
            pieces_p = [dsa_attn_prompt(q_h, k_h, v_h, bias_d, keep_mask)]
            ps = [a.reshape(bs, 1, a.shape[-1]) for a in project(ys, ms, ODD_OUTS_DECODE)]
            k_s, v_s, ki_s, wi_s, q_s, qi_s = ps
            wi_rows = jnp.broadcast_to(wi_s[:, 0, :IDX_HEADS, None], (bs, IDX_HEADS, LANES))
            score = dsa_score_decode(page_table, qi_s[:, 0].reshape(bs, IDX_HEADS, IDX_DIM), wi_rows,
                                     ki_s[:, :, :IDX_DIM], cache_dsa_idx_k, o)
            keep_s = dsa_topk_decode(score.reshape(bs, KEYS_PAD)).reshape(bs, 1, KEYS_PAD)
            o_ds = dsa_attn_decode(page_table, _block_queries(q_s[:, 0], DSA_HEADS, DSA_GROUP, DSA_HEADS), k_s, v_s,
                                   keep_s, dec_bias_d, new_bias, pool_dk, pool_dv, o)
            pieces_s = [_decode_heads(o_ds, DSA_HEADS)]
            for acc, r in zip(rows_op, (pp[0], pp[1], pp[2][..., :IDX_DIM])):
                acc.append(r)
            for acc, r in zip(rows_os, (k_s, v_s, ki_s[..., :IDX_DIM])):
                acc.append(r)
        yp = mixer_merge(yp, pieces_p, mod(mp, 1, 2), w_o)
        ys = mixer_merge(ys, pieces_s, mod(ms, 1, 2), w_o)
        yp, ys = ffn(yp, mp, 2, 1), ffn(ys, ms, 2, 1)

    def stack_layers(rows, n_heads):
        return jnp.stack([r.reshape(r.shape[0], r.shape[1], n_heads, HEAD_DIM) for r in rows], axis=2)

    even_heads = (MOBA_HEADS, MOBA_HEADS) + (NSA_KV,) * 6
    moba_k_p, moba_v_p, cmp_k_p, cmp_v_p, slc_k_p, slc_v_p, win_k_p, win_v_p = [
        stack_layers(r, n) for r, n in zip(rows_ep, even_heads)]
    moba_k_s, moba_v_s, cmp_k_s, cmp_v_s, slc_k_s, slc_v_s, win_k_s, win_v_s = [
        stack_layers(r, n) for r, n in zip(rows_es, even_heads)]
    dsa_k_p, dsa_v_p = stack_layers(rows_op[0], DSA_KV), stack_layers(rows_op[1], DSA_KV)
    dsa_k_s, dsa_v_s = stack_layers(rows_os[0], DSA_KV), stack_layers(rows_os[1], DSA_KV)
    dsa_idx_k_p, dsa_idx_k_s = jnp.stack(rows_op[2], axis=2), jnp.stack(rows_os[2], axis=2)
    return (yp, ys.reshape(bs, 1, D_MODEL), moba_k_p, moba_k_s, moba_v_p, moba_v_s, cmp_k_p, cmp_k_s, cmp_v_p,
            cmp_v_s, slc_k_p, slc_k_s, slc_v_p, slc_v_s, win_k_p, win_k_s, win_v_p, win_v_s,
            dsa_k_p, dsa_k_s, dsa_v_p, dsa_v_s, dsa_idx_k_p, dsa_idx_k_s)
```

```python
import functools
import math

import jax
import jax.numpy as jnp
from jax import lax
from jax.experimental import pallas as pl
from jax.experimental.pallas import tpu as pltpu

D_MODEL = 1024
DEPTH = 4
PAST_LEN = 2048
PAGE_SIZE = 128
N_PAGES = PAST_LEN // PAGE_SIZE
HEAD_DIM = 64
MOBA_HEADS = 4
NSA_HEADS = 12
NSA_KV = 2
NSA_GROUP = 6
DSA_HEADS = 16
DSA_KV = 4
DSA_GROUP = 4
IDX_HEADS = 8
IDX_DIM = 64
MOBA_BLOCK = 256
MOBA_TOPK = 3
CMP_LEN = 32
CMP_STRIDE = 16
CMP_HIDDEN = 128
SEL_BLOCK = 64
N_SEL = 8
WINDOW = 512
DSA_TOPK = 256
N_BUCKETS = 32
MAX_DISTANCE = 128
D_FF = 2816
MOBA_W = MOBA_HEADS * HEAD_DIM
NSA_W = NSA_HEADS * HEAD_DIM
NSA_KW = NSA_KV * HEAD_DIM
DSA_W = DSA_HEADS * HEAD_DIM
DSA_KW = DSA_KV * HEAD_DIM
RMS_EPS = 1e-6
NEG_INF = -1e30
TINY = 1e-30
SEL_FORCE = 1e4
QK_SCALE = HEAD_DIM ** -0.5

LANES = 128
VMEM_LIMIT = 56 * 1024 * 1024
BF16 = jnp.bfloat16
F32 = jnp.float32
HIGHEST = lax.Precision.HIGHEST
NT_DIMS = (((1,), (1,)), ((), ()))
M_INIT = -3e38
INT_MIN = -2 ** 31


def _cparams(*sem):
    return pltpu.CompilerParams(dimension_semantics=sem, vmem_limit_bytes=VMEM_LIMIT)


def _t5_bucket(dist):
    n = jnp.maximum(dist, 0)
    exact = N_BUCKETS // 2
    nf = jnp.maximum(n, 1).astype(F32)
    large = exact + (jnp.log(nf / exact) / math.log(MAX_DISTANCE / exact) * (N_BUCKETS - exact)).astype(jnp.int32)
    return jnp.where(n < exact, n, jnp.minimum(large, N_BUCKETS - 1))


def _rel_bias(dist, table):
    onehot = jax.nn.one_hot(_t5_bucket(dist), N_BUCKETS, dtype=F32)
    return jnp.einsum("...k,kh->h...", onehot, table, precision=HIGHEST)


def _adaln_kernel(c_ref, w_ref, b_ref, o_ref):
    c = c_ref[...]
    s = (c * jax.nn.sigmoid(c)).astype(BF16)
    o_ref[0] = jnp.dot(s, w_ref[0].astype(BF16), preferred_element_type=F32) + b_ref[0]


def adaln_all(c_all, ada_w, ada_b):
    r = c_all.shape[0]
    n_out = ada_w.shape[2]
    tn = 1024
    return pl.pallas_call(
        _adaln_kernel,
        grid=(DEPTH, n_out // tn),
        in_specs=[
            pl.BlockSpec((r, D_MODEL), lambda l, j: (0, 0)),
            pl.BlockSpec((1, D_MODEL, tn), lambda l, j: (l, 0, j)),
            pl.BlockSpec((1, 1, tn), lambda l, j: (l, 0, j)),
        ],
        out_specs=pl.BlockSpec((1, r, tn), lambda l, j: (l, 0, j)),
        out_shape=jax.ShapeDtypeStruct((DEPTH, r, n_out), F32),
        compiler_params=_cparams("parallel", "parallel"),
    )(c_all, ada_w, ada_b.reshape(DEPTH, 1, n_out))


def _modnorm(x, g, scale, shift):
    y = x * lax.rsqrt(jnp.mean(x * x, axis=-1, keepdims=True) + RMS_EPS)
    return (y * g) * (1.0 + scale) + shift


def _mod_spec(tm_rows, t_mod):
    if t_mod == 1:
        return pl.BlockSpec((1, 1, D_MODEL), lambda b, i, *_: (b, 0, 0))
    return pl.BlockSpec((1, tm_rows, D_MODEL), lambda b, i, *_: (b, i, 0))


def _seg_ones():
    return jnp.kron(jnp.eye(LANES // HEAD_DIM, dtype=F32), jnp.ones((HEAD_DIM, HEAD_DIM), F32))


def _head_rms(z, seg_ref, gain):
    ss = jnp.dot(z * z, seg_ref[...], preferred_element_type=F32, precision=HIGHEST)
    return z * lax.rsqrt(ss * (1.0 / HEAD_DIM) + RMS_EPS) * gain


def _ffn_kernel(x_ref, sh_ref, sc_ref, gt_ref, g_ref, wa_ref, wg_ref, wo_ref, o_ref, xn_ref, acc_ref):
    j = pl.program_id(2)

    @pl.when(j == 0)
    def _():
        xn_ref[...] = _modnorm(x_ref[0], g_ref[...], sc_ref[0], sh_ref[0]).astype(BF16)
        acc_ref[...] = jnp.zeros_like(acc_ref)

    xn = xn_ref[...]
    a = jnp.dot(xn, wa_ref[...], preferred_element_type=F32)
    g = jnp.dot(xn, wg_ref[...], preferred_element_type=F32)
    h = ((g * jax.nn.sigmoid(g)) * a).astype(BF16)
    acc_ref[...] += jnp.dot(h, wo_ref[...], preferred_element_type=F32)

    @pl.when(j == pl.num_programs(2) - 1)
    def _():
        o_ref[0] = x_ref[0] + (0.5 * gt_ref[0]) * acc_ref[...]


def ffn_half(x, shift, scale, gate, g, w_in, w_out):
    b, t, _ = x.shape
    tm = min(t, 512)
    fc = D_FF // 2
    nf = D_FF // fc
    mspec = _mod_spec(tm, shift.shape[1])
    return pl.pallas_call(
        _ffn_kernel,
        grid=(b, t // tm, nf),
        in_specs=[
            pl.BlockSpec((1, tm, D_MODEL), lambda bb, i, j: (bb, i, 0)),
            mspec, mspec, mspec,
            pl.BlockSpec((1, D_MODEL), lambda bb, i, j: (0, 0)),
            pl.BlockSpec((D_MODEL, fc), lambda bb, i, j: (0, j)),
            pl.BlockSpec((D_MODEL, fc), lambda bb, i, j: (0, j + nf)),
            pl.BlockSpec((fc, D_MODEL), lambda bb, i, j: (j, 0)),
        ],
        out_specs=pl.BlockSpec((1, tm, D_MODEL), lambda bb, i, j: (bb, i, 0)),
        out_shape=jax.ShapeDtypeStruct(x.shape, F32),
        scratch_shapes=[pltpu.VMEM((tm, D_MODEL), BF16), pltpu.VMEM((tm, D_MODEL), F32)],
        compiler_params=_cparams("parallel", "parallel", "arbitrary"),
    )(x, shift, scale, gate, g.reshape(1, D_MODEL), w_in, w_in, w_out)


def _proj_kernel(modes, outs, x_ref, sh_ref, sc_ref, g_ref, w_ref, gain_ref, seg_ref, *o_refs):
    h = _modnorm(x_ref[0], g_ref[...], sc_ref[0], sh_ref[0])
    z = jnp.dot(h.astype(BF16), w_ref[...], preferred_element_type=F32)
    chunks = []
    for c, mode in enumerate(modes):
        zc = z[:, c * LANES:(c + 1) * LANES]
        if mode == "norm":
            zc = _head_rms(zc, seg_ref, gain_ref[:, c * LANES:(c + 1) * LANES])
        elif mode == "sigmoid":
            zc = jax.nn.sigmoid(zc)
        chunks.append(zc)
    for o_ref, out in zip(o_refs, outs):
        if out[0] == "cols":
            _, start, width = out
            for c in range(width // LANES):
                o_ref[0, :, c * LANES:(c + 1) * LANES] = chunks[start // LANES + c]
        else:
            _, start, n_heads, dtype, scale = out
            for hh in range(n_heads):
                lo = start + hh * HEAD_DIM
                piece = chunks[lo // LANES][:, lo % LANES:lo % LANES + HEAD_DIM]
                o_ref[0, hh] = (piece * scale).astype(dtype)


def mixer_project(x, shift, scale, g, w, gain_cols, modes, outs):
    b, t, _ = x.shape
    tm = min(t, 512)
    wp = w.shape[1]
    mspec = _mod_spec(tm, shift.shape[1])
    out_shapes, out_specs = [], []
    for out in outs:
        if out[0] == "cols":
            out_shapes.append(jax.ShapeDtypeStruct((b, t, out[2]), F32))
            out_specs.append(pl.BlockSpec((1, tm, out[2]), lambda bb, i: (bb, i, 0)))
        else:
            out_shapes.append(jax.ShapeDtypeStruct((b, out[2], t, HEAD_DIM), out[3]))
            out_specs.append(pl.BlockSpec((1, out[2], tm, HEAD_DIM), lambda bb, i: (bb, 0, i, 0)))
    return pl.pallas_call(
        functools.partial(_proj_kernel, modes, outs),
        grid=(b, t // tm),
        in_specs=[
            pl.BlockSpec((1, tm, D_MODEL), lambda bb, i: (bb, i, 0)),
            mspec, mspec,
            pl.BlockSpec((1, D_MODEL), lambda bb, i: (0, 0)),
            pl.BlockSpec((D_MODEL, wp), lambda bb, i: (0, 0)),
            pl.BlockSpec((1, wp), lambda bb, i: (0, 0)),
            pl.BlockSpec((LANES, LANES), lambda bb, i: (0, 0)),
        ],
        out_specs=out_specs,
        out_shape=out_shapes,
        compiler_params=_cparams("parallel", "parallel"),
    )(x, shift, scale, g.reshape(1, D_MODEL), w, gain_cols, _seg_ones())


def _merge_kernel(n_pieces, y_ref, gt_ref, w_ref, *refs):
    out_ref = refs[n_pieces]
    heads = [refs[k][0, hh] for k in range(n_pieces) for hh in range(refs[k].shape[1])]
    o = jnp.concatenate(heads, axis=1).astype(BF16)
    out_ref[0] = y_ref[0] + gt_ref[0] * jnp.dot(o, w_ref[...], preferred_element_type=F32)


def mixer_merge(y, pieces, gate, w_out):
    b, t, _ = y.shape
    tm = min(t, 512)
    piece_specs = [pl.BlockSpec((1, p.shape[1], tm, HEAD_DIM), lambda bb, i: (bb, 0, i, 0)) for p in pieces]
    return pl.pallas_call(
        functools.partial(_merge_kernel, len(pieces)),
        grid=(b, t // tm),
        in_specs=[
            pl.BlockSpec((1, tm, D_MODEL), lambda bb, i: (bb, i, 0)),
            _mod_spec(tm, gate.shape[1]),
            pl.BlockSpec((D_MODEL, D_MODEL), lambda bb, i: (0, 0)),
        ] + piece_specs,
        out_specs=pl.BlockSpec((1, tm, D_MODEL), lambda bb, i: (bb, i, 0)),
        out_shape=jax.ShapeDtypeStruct(y.shape, F32),
        compiler_params=_cparams("parallel", "parallel"),
    )(y, gate, w_out, *pieces)


TQ = 128
CHUNK_TILES = 4
CK = CHUNK_TILES * TQ


def _bias_tiles(table, t):
    d = jnp.arange(t)[:, None] - jnp.arange(t)[None, :]
    diag = jnp.where(d >= 0, _rel_bias(d, table), NEG_INF)
    adj = _rel_bias(t + d, table)
    far = _rel_bias(2 * t + d, table)
    masked = jnp.full_like(far, NEG_INF)
    edge = jnp.where(d < 0, far, NEG_INF)
    return jnp.stack([diag, adj, far, masked, edge], axis=1)


def _tile_kind(d):
    return jnp.where(d < 0, 3, jnp.minimum(d, 2))


def _loop(n, body, init, static):
    if static:
        for c in range(n):
            init = body(c, init)
        return init
    return lax.fori_loop(0, n, body, init)


def _attend(qb, n_chunks, k_of, v_of, add_of, s_ref, static=False):
    rows = qb.shape[0]

    def cols(c, w):
        return pl.ds(c * w if static else pl.multiple_of(c * w, w), w)

    def score(c, mx):
        s = lax.dot_general(qb, k_of(c), NT_DIMS, preferred_element_type=F32) + add_of(c)
        w = s.shape[1]
        s_ref[:, cols(c, w)] = s
        for part in range(w // LANES):
            mx = jnp.maximum(mx, s[:, part * LANES:(part + 1) * LANES])
        return mx

    mx = _loop(n_chunks, score, jnp.full((rows, LANES), M_INIT, F32), static)
    m = jnp.max(mx, axis=1, keepdims=True)

    def accumulate(c, acc):
        v = v_of(c)
        w = v.shape[0]
        p = jnp.exp(s_ref[:, cols(c, w)] - m).astype(BF16)
        return acc + jnp.dot(p, jnp.concatenate([v, jnp.ones_like(v)], axis=1), preferred_element_type=F32)

    acc = _loop(n_chunks, accumulate, jnp.zeros((rows, 2 * HEAD_DIM), F32), static)
    return acc[:, :HEAD_DIM] / jnp.maximum(acc[:, HEAD_DIM:HEAD_DIM + 1], TINY)


def _top_k_rows(score, allowed, k, n_cand):
    lane = lax.broadcasted_iota(jnp.int32, score.shape, 1)
    s = jnp.where(allowed, score, -jnp.inf)
    ahead = jnp.zeros(score.shape, F32)
    for j in range(n_cand):
        col = s[:, j:j + 1]
        ahead = ahead + jnp.where(col > s, 1.0, jnp.where(col == s, jnp.where(lane > j, 1.0, 0.0), 0.0))
    return jnp.logical_and(allowed, ahead < float(k))


def _count(mask):
    return jnp.sum(jnp.where(mask, 1.0, 0.0), axis=1, keepdims=True)


def _top_k_mask(score, valid, pos, k):
    bits = pltpu.bitcast(score, jnp.int32)
    key = jnp.where(bits < 0, bits ^ jnp.int32(0x7FFFFFFF), bits)
    key = jnp.where(valid, key, jnp.int32(INT_MIN))
    kf = float(k)

    thr = jnp.full((score.shape[0], 1), INT_MIN, jnp.int32)
    cand = thr ^ jnp.int32(INT_MIN)
    thr = jnp.where(_count(key >= cand) >= kf, cand, thr)

    def step(b, thr):
        cand = thr | lax.shift_left(jnp.int32(1), 30 - b)
        return jnp.where(_count(key >= cand) >= kf, cand, thr)

    thr = lax.fori_loop(0, 31, step, thr)
    above = key > thr
    tie = jnp.logical_and(key == thr, valid)
    need = kf - _count(above)

    n_bits = max(int(score.shape[1] - 1).bit_length(), 1)
    all_pos = jnp.full((score.shape[0], 1), 2 ** n_bits - 1, jnp.int32)

    def tie_search():
        def tie_step(b, cut):
            cand = cut | lax.shift_left(jnp.int32(1), n_bits - 1 - b)
            return jnp.where(_count(jnp.logical_and(tie, pos < cand)) < need, cand, cut)

        return lax.fori_loop(0, n_bits, tie_step, jnp.zeros_like(all_pos))

    surplus = jnp.max(jnp.where(_count(tie) > need, 1.0, 0.0))
    cut = lax.cond(surplus > 0.0, tie_search, lambda: all_pos)
    keep = jnp.logical_or(above, jnp.logical_and(tie, pos <= cut))
    return jnp.logical_and(keep, valid)


def _moba_kernel(q_ref, k_ref, v_ref, bias_ref, o_ref, km_ref, s_ref):
    i = pl.program_id(2)
    nb = km_ref.shape[0]
    t = MOBA_BLOCK

    @pl.when(i == 0)
    def _():
        km_ref[...] = jnp.mean(k_ref[0, 0].reshape(nb, t, HEAD_DIM), axis=1)

    q = q_ref[0, 0]
    gate = lax.dot_general(q.astype(BF16), km_ref[...].astype(BF16), NT_DIMS, preferred_element_type=F32)
    blk = lax.broadcasted_iota(jnp.int32, gate.shape, 1)
    chosen = jnp.logical_or(_top_k_rows(gate, blk < i, MOBA_TOPK, nb), blk == i)
    blk_add = jnp.where(chosen, 0.0, NEG_INF)

    def rows_of(j):
        return pl.ds(pl.multiple_of(j * t, t), t)

    def add_of(j):
        col = jnp.sum(jnp.where(blk == j, blk_add, 0.0), axis=1, keepdims=True)
        return bias_ref[0, jnp.minimum(i - j, 2)] + col

    o_ref[0, 0] = _attend((q * QK_SCALE).astype(BF16), i + 1, lambda j: k_ref[0, 0, rows_of(j), :].astype(BF16),
                          lambda j: v_ref[0, 0, rows_of(j), :], add_of, s_ref)


def moba_prompt(q, k, v, bias):
    b, h, t, _ = q.shape
    tb = MOBA_BLOCK
    nb = t // tb
    full = pl.BlockSpec((1, 1, t, HEAD_DIM), lambda bb, hh, i: (bb, hh, 0, 0))
    return pl.pallas_call(
        _moba_kernel,
        grid=(b, h, nb),
        in_specs=[
            pl.BlockSpec((1, 1, tb, HEAD_DIM), lambda bb, hh, i: (bb, hh, i, 0)),
            full, full,
            pl.BlockSpec((1, 3, tb, tb), lambda bb, hh, i: (hh, 0, 0, 0)),
        ],
        out_specs=pl.BlockSpec((1, 1, tb, HEAD_DIM), lambda bb, hh, i: (bb, hh, i, 0)),
        out_shape=jax.ShapeDtypeStruct(q.shape, F32),
        scratch_shapes=[pltpu.VMEM((nb, HEAD_DIM), F32), pltpu.VMEM((tb, t), F32)],
        compiler_params=_cparams("parallel", "parallel", "arbitrary"),
    )(q, k, v, bias)


N_CHUNK = PAST_LEN // CMP_STRIDE
CHUNK_W = CMP_STRIDE * NSA_KW
N_CMP = (PAST_LEN - CMP_LEN) // CMP_STRIDE + 1
HALF_W = NSA_KV * CMP_HIDDEN


def _compress_weights(pe, w1, w2):
    eye = jnp.eye(NSA_KV, dtype=F32)
    pe_x = jnp.broadcast_to(pe.reshape(2, 2, CMP_STRIDE, 1, HEAD_DIM),
                            (2, 2, CMP_STRIDE, NSA_KV, HEAD_DIM)).reshape(2, 2, CHUNK_W)
    w1_h = w1.reshape(2, 2, CMP_STRIDE, HEAD_DIM, CMP_HIDDEN)
    w1_x = jnp.einsum("thrdj,gq->trgdhqj", w1_h, eye).reshape(2, CHUNK_W, 2 * HALF_W)
    w2_x = jnp.einsum("tjd,gq->tgjqd", w2, eye).reshape(2, HALF_W, NSA_KW)
    return pe_x, w1_x.astype(BF16), w2_x.astype(BF16)


def _compress_tail(ha, hb, w2):
    hid = jax.nn.gelu(ha + pltpu.roll(hb, N_CHUNK - 1, 0))
    return jnp.dot(hid.astype(BF16), w2, preferred_element_type=F32)


def _compress_one(r, pe_ref, w1_ref, w2_ref, t):
    ha = jnp.dot((r + pe_ref[t, 0:1]).astype(BF16), w1_ref[t, :, :HALF_W], preferred_element_type=F32)
    hb = jnp.dot((r + pe_ref[t, 1:2]).astype(BF16), w1_ref[t, :, HALF_W:], preferred_element_type=F32)
    return _compress_tail(ha, hb, w2_ref[t])


def _compress_kernel(rk_ref, rv_ref, pe_ref, w1_ref, w2_ref, gain_ref, seg_ref, ck_ref, cv_ref):
    ck_ref[0] = _head_rms(_compress_one(rk_ref[0], pe_ref, w1_ref, w2_ref, 0), seg_ref, gain_ref[...])
    cv_ref[0] = _compress_one(rv_ref[0], pe_ref, w1_ref, w2_ref, 1)


def nsa_compress_prompt(ck_raw, cv_raw, pe_x, w1_x, w2_x, gain_row):
    b = ck_raw.shape[0]
    rows = pl.BlockSpec((1, N_CHUNK, CHUNK_W), lambda bb: (bb, 0, 0))
    out = pl.BlockSpec((1, N_CHUNK, NSA_KW), lambda bb: (bb, 0, 0))
    const = lambda *shape: pl.BlockSpec(shape, lambda bb: (0,) * len(shape))
    return pl.pallas_call(
        _compress_kernel,
        grid=(b,),
        in_specs=[rows, rows, const(2, 2, CHUNK_W), const(2, CHUNK_W, 2 * HALF_W),
                  const(2, HALF_W, NSA_KW), const(1, NSA_KW), const(LANES, LANES)],
        out_specs=[out, out],
        out_shape=[jax.ShapeDtypeStruct((b, N_CHUNK, NSA_KW), F32)] * 2,
        compiler_params=_cparams("parallel"),
    )(ck_raw.reshape(b, N_CHUNK, CHUNK_W), cv_raw.reshape(b, N_CHUNK, CHUNK_W), pe_x, w1_x, w2_x, gain_row,
      _seg_ones())


def _overlap_matrix():
    c = jnp.arange(LANES)[:, None]
    n = jnp.arange(LANES)[None, :]
    return jnp.logical_and(c * CMP_STRIDE < n * SEL_BLOCK + SEL_BLOCK,
                           c * CMP_STRIDE + CMP_LEN > n * SEL_BLOCK).astype(F32)


def _block_expand(n_keys):
    return (jnp.arange(n_keys)[None, :] // SEL_BLOCK == jnp.arange(LANES)[:, None]).astype(BF16)


def _nsa_kernel(q_ref, gt_ref, ck_ref, cv_ref, sk_ref, sv_ref, wk_ref, wv_ref, bias_ref, cb_ref, ov_ref, ex_ref,
                o_ref, madd_ref, s_ref):
    g = pl.program_id(1)
    i = pl.program_id(2)
    hg = NSA_GROUP
    rows = hg * TQ
    nq = madd_ref.shape[0]
    row = lax.broadcasted_iota(jnp.int32, (TQ, LANES), 0)
    lane = lax.broadcasted_iota(jnp.int32, (TQ, LANES), 1)
    qpos = i * TQ + row
    qf = q_ref[0].reshape(rows, HEAD_DIM)

    def group_half(x):
        return jnp.where(g == 0, x[:, :HEAD_DIM], x[:, HEAD_DIM:])

    cmask = jnp.logical_and(qpos - (lane * CMP_STRIDE + CMP_LEN - 1) >= 0, lane < N_CMP)
    cmask_all = jnp.concatenate([cmask] * hg, axis=0)
    lc = lax.dot_general(qf.astype(BF16), group_half(ck_ref[0]).astype(BF16), NT_DIMS, preferred_element_type=F32)
    lc = jnp.where(cmask_all, lc * QK_SCALE + cb_ref[...].reshape(rows, LANES), NEG_INF)
    e = jnp.where(cmask_all, jnp.exp(lc - jnp.max(lc, axis=1, keepdims=True)), 0.0)
    pc = e / jnp.maximum(jnp.sum(e, axis=1, keepdims=True), TINY)
    o_cmp = jnp.dot(pc.astype(BF16), group_half(cv_ref[0]).astype(BF16), preferred_element_type=F32)
    psum = jnp.sum(pc.reshape(hg, TQ, LANES), axis=0)

    imp = jnp.dot(psum.astype(BF16), ov_ref[...], preferred_element_type=F32)
    cur = qpos // SEL_BLOCK
    forced = jnp.logical_or(lane == 0, jnp.logical_or(lane == cur, lane == cur - 1))
    picked = _top_k_rows(jnp.where(forced, SEL_FORCE, imp), lane <= cur, N_SEL, nq * TQ // SEL_BLOCK)
    pick_b = jnp.where(picked, 1.0, 0.0).astype(BF16)

    key_add = jnp.where(jnp.dot(pick_b, ex_ref[...], preferred_element_type=F32) > 0.5, 0.0, NEG_INF)
    for j in range(nq):
        madd_ref[j] = key_add[:, j * TQ:(j + 1) * TQ]

    qb = (qf * QK_SCALE).astype(BF16)

    def bias_rows(kind):
        return bias_ref[:, kind].reshape(rows, TQ)

    def key_tile(ref, j):
        return ref[0, 0, pl.ds(pl.multiple_of(j * TQ, TQ), TQ), :]

    def keys_of(c):
        return pl.ds(pl.multiple_of(c * CK, CK), CK)

    def slc_add(c):
        tiles = [CHUNK_TILES * c + k for k in range(CHUNK_TILES)]
        mask = jnp.concatenate([madd_ref[j] for j in tiles], axis=1)
        add = jnp.concatenate([bias_rows(_tile_kind(i - j)) for j in tiles], axis=1)
        return add + jnp.concatenate([mask] * hg, axis=0)

    o_slc = _attend(qb, i // CHUNK_TILES + 1, lambda c: sk_ref[0, 0, keys_of(c), :],
                    lambda c: sv_ref[0, 0, keys_of(c), :], slc_add, s_ref)

    n_win = WINDOW // TQ
    win_chunk = (n_win + 2) // 2

    def win_tiles(w):
        first = i - 2 * win_chunk + 1 + w * win_chunk
        return [first + k for k in range(win_chunk)]

    def win_keys(ref):
        return lambda w: jnp.concatenate([key_tile(ref, jnp.maximum(j, 0)) for j in win_tiles(w)], axis=0)

    def win_add(w):
        kinds = []
        for j in win_tiles(w):
            d = i - j
            kind = jnp.where(d == n_win, 4, jnp.where(d > n_win, 3, jnp.minimum(d, 2)))
            kinds.append(jnp.where(j < 0, 3, kind))
        return jnp.concatenate([bias_rows(kind) for kind in kinds], axis=1)

    o_win = _attend(qb, 2, win_keys(wk_ref), win_keys(wv_ref), win_add, s_ref, static=True)

    gt = gt_ref[0]

    def gate_col(branch):
        cols = []
        for h in range(hg):
            c0 = 3 * h + branch
            cols.append(jnp.where(g == 0, gt[:, c0:c0 + 1], gt[:, 3 * hg + c0:3 * hg + c0 + 1]))
        return jnp.concatenate(cols, axis=0)

    o = gate_col(0) * o_cmp + gate_col(1) * o_slc + gate_col(2) * o_win
    o_ref[0] = o.reshape(hg, TQ, HEAD_DIM)


def nsa_prompt(q, gates, ck, cv, sk, sv, wk, wv, bias, cbias):
    b, _, t, _ = q.shape
    hg = NSA_GROUP
    nq = t // TQ
    kv = pl.BlockSpec((1, 1, t, HEAD_DIM), lambda bb, gg, i: (bb, gg, 0, 0))
    cmp_spec = pl.BlockSpec((1, LANES, NSA_KW), lambda bb, gg, i: (bb, 0, 0))
    return pl.pallas_call(
        _nsa_kernel,
        grid=(b, NSA_KV, nq),
        in_specs=[
            pl.BlockSpec((1, hg, TQ, HEAD_DIM), lambda bb, gg, i: (bb, gg, i, 0)),
            pl.BlockSpec((1, TQ, LANES), lambda bb, gg, i: (bb, i, 0)),
            cmp_spec, cmp_spec, kv, kv, kv, kv,
            pl.BlockSpec((hg, 5, TQ, TQ), lambda bb, gg, i: (gg, 0, 0, 0)),
            pl.BlockSpec((hg, TQ, LANES), lambda bb, gg, i: (gg, i, 0)),
            pl.BlockSpec((LANES, LANES), lambda bb, gg, i: (0, 0)),
            pl.BlockSpec((LANES, t), lambda bb, gg, i: (0, 0)),
        ],
        out_specs=pl.BlockSpec((1, hg, TQ, HEAD_DIM), lambda bb, gg, i: (bb, gg, i, 0)),
        out_shape=jax.ShapeDtypeStruct(q.shape, F32),
        scratch_shapes=[pltpu.VMEM((nq, TQ, TQ), F32), pltpu.VMEM((hg * TQ, t), F32)],
        compiler_params=_cparams("parallel", "parallel", "arbitrary"),
    )(q, gates, ck, cv, sk, sv, wk, wv, bias, cbias, _overlap_matrix().astype(BF16), _block_expand(t))


def _dsa_index_kernel(qi_ref, wi_ref, ki_ref, o_ref):
    i = pl.program_id(1)
    t = ki_ref.shape[1]
    nq = t // TQ
    wi = wi_ref[0]

    def select(width):
        ki = ki_ref[0, :width, :IDX_DIM].astype(BF16)
        score = jnp.zeros((TQ, width), F32)
        for h in range(IDX_HEADS):
            s = lax.dot_general(qi_ref[0, h].astype(BF16), ki, NT_DIMS, preferred_element_type=F32)
            score = score + wi[:, h:h + 1] * jnp.maximum(s * IDX_DIM ** -0.5, 0.0)
        score = score * IDX_HEADS ** -0.5
        kpos = lax.broadcasted_iota(jnp.int32, (TQ, width), 1)
        qpos = i * TQ + lax.broadcasted_iota(jnp.int32, (TQ, width), 0)
        keep = _top_k_mask(score, kpos <= qpos, kpos, min(DSA_TOPK, t // 4))
        add = jnp.where(keep, 0.0, NEG_INF).astype(BF16)
        for j in range(nq):
            lo = j * TQ
            o_ref[0, 0, j] = add[:, lo:lo + TQ] if lo < width else jnp.full((TQ, TQ), NEG_INF, BF16)

    widths = [w for w in (t // 4, t // 2, t) if w % TQ == 0 and w >= TQ]
    lo_tile = 0
    for w in widths:
        hi_tile = w // TQ
        pl.when(jnp.logical_and(i >= lo_tile, i < hi_tile))(functools.partial(select, w))
        lo_tile = hi_tile


def dsa_index_prompt(qi, wi, ki):
    b, ih, t, _ = qi.shape
    nq = t // TQ
    return pl.pallas_call(
        _dsa_index_kernel,
        grid=(b, nq),
        in_specs=[
            pl.BlockSpec((1, ih, TQ, IDX_DIM), lambda bb, i: (bb, 0, i, 0)),
            pl.BlockSpec((1, TQ, LANES), lambda bb, i: (bb, i, 0)),
            pl.BlockSpec((1, t, LANES), lambda bb, i: (bb, 0, 0)),
        ],
        out_specs=pl.BlockSpec((1, 1, nq, TQ, TQ), lambda bb, i: (bb, i, 0, 0, 0)),
        out_shape=jax.ShapeDtypeStruct((b, nq, nq, TQ, TQ), BF16),
        compiler_params=_cparams("parallel", "parallel"),
    )(qi, wi, ki)


def _dsa_attn_kernel(q_ref, k_ref, v_ref, bias_ref, m_ref, o_ref, s_ref):
    i = pl.program_id(2)
    hg = DSA_GROUP
    rows = hg * TQ

    def keys_of(c):
        return pl.ds(pl.multiple_of(c * CK, CK), CK)

    def add_of(c):
        tiles = [CHUNK_TILES * c + k for k in range(CHUNK_TILES)]
        mask = jnp.concatenate([m_ref[0, 0, j] for j in tiles], axis=1).astype(F32)
        add = jnp.concatenate([bias_ref[:, _tile_kind(i - j)].reshape(rows, TQ) for j in tiles], axis=1)
        return add + jnp.concatenate([mask] * hg, axis=0)

    o = _attend(q_ref[0].reshape(rows, HEAD_DIM), i // CHUNK_TILES + 1, lambda c: k_ref[0, 0, keys_of(c), :],
                lambda c: v_ref[0, 0, keys_of(c), :], add_of, s_ref)
    o_ref[0] = o.reshape(hg, TQ, HEAD_DIM)


def dsa_attn_prompt(q, k, v, bias, mask):
    b, _, t, _ = q.shape
    hg = DSA_GROUP
    nq = t // TQ
    kv = pl.BlockSpec((1, 1, t, HEAD_DIM), lambda bb, gg, i: (bb, gg, 0, 0))
    return pl.pallas_call(
        _dsa_attn_kernel,
        grid=(b, DSA_KV, nq),
        in_specs=[
            pl.BlockSpec((1, hg, TQ, HEAD_DIM), lambda bb, gg, i: (bb, gg, i, 0)),
            kv, kv,
            pl.BlockSpec((hg, 5, TQ, TQ), lambda bb, gg, i: (gg, 0, 0, 0)),
            pl.BlockSpec((1, 1, nq, TQ, TQ), lambda bb, gg, i: (bb, i, 0, 0, 0)),
        ],
        out_specs=pl.BlockSpec((1, hg, TQ, HEAD_DIM), lambda bb, gg, i: (bb, gg, i, 0)),
        out_shape=jax.ShapeDtypeStruct(q.shape, F32),
        scratch_shapes=[pltpu.VMEM((hg * TQ, t), F32)],
        compiler_params=_cparams("parallel", "parallel", "arbitrary"),
    )(q, k, v, bias, mask)


REQ = 2


def _row_spec(*shape):
    return pl.BlockSpec((REQ,) + shape, lambda b, pt: (b,) + (0,) * len(shape))


def _const_spec(*shape):
    return pl.BlockSpec(shape, lambda b, pt: (0,) * len(shape))


def _decode_call(kernel, n_req, page_table, in_specs, args, out_shapes, out_specs):
    return pl.pallas_call(
        kernel,
        grid_spec=pltpu.PrefetchScalarGridSpec(num_scalar_prefetch=1, grid=(n_req // REQ,), in_specs=in_specs,
                                               out_specs=out_specs),
        out_shape=out_shapes,
        compiler_params=_cparams("parallel"),
    )(page_table, *args)


def _page_specs(lane_block, width):
    return [pl.BlockSpec((1, PAGE_SIZE, width),
                         functools.partial(lambda r, p, b, pt: (pt[b * REQ + r, p], 0, lane_block), r, p))
            for r in range(REQ) for p in range(N_PAGES)]


def _request_pages(refs, n_tensors, r):
    per = REQ * N_PAGES
    return [refs[t * per + r * N_PAGES:t * per + (r + 1) * N_PAGES] for t in range(n_tensors)]


def _paged_attention(qblk, k_pages, v_pages, bias_ref, mask_of_page, k_new, v_new, new_add):
    qb = (qblk * QK_SCALE).astype(BF16)
    s_new = jnp.sum(qblk * k_new, axis=1, keepdims=True) * QK_SCALE + new_add
    s_pages = []
    for p in range(N_PAGES):
        s = lax.dot_general(qb, k_pages[p][0].astype(BF16), NT_DIMS, preferred_element_type=F32)
        s_pages.append(s + bias_ref[:, p * PAGE_SIZE:(p + 1) * PAGE_SIZE] + mask_of_page(p))
    m = s_new
    for s in s_pages:
        m = jnp.maximum(m, jnp.max(s, axis=1, keepdims=True))
    e_new = jnp.exp(s_new - m)
    l = e_new
    acc = e_new * v_new
    for p in range(N_PAGES):
        e = jnp.exp(s_pages[p] - m)
        l = l + jnp.sum(e, axis=1, keepdims=True)
        acc = acc + jnp.dot(e.astype(BF16), v_pages[p][0].astype(BF16), preferred_element_type=F32)
    return acc / jnp.maximum(l, TINY)


def _group_lanes(o_full, heads_per_group):
    hp, w = o_full.shape
    grp = lax.broadcasted_iota(jnp.int32, (hp, HEAD_DIM), 0) // heads_per_group
    out = o_full[:, :HEAD_DIM]
    for gg in range(1, w // HEAD_DIM):
        out = jnp.where(grp == gg, o_full[:, gg * HEAD_DIM:(gg + 1) * HEAD_DIM], out)
    return out


MOBA_HP = 8


def _moba_decode_kernel(pt_ref, q_ref, kn_ref, vn_ref, bias_ref, bnew_ref, *refs):
    o_ref = refs[2 * REQ * N_PAGES]
    for r in range(REQ):
        k_pages, v_pages = _request_pages(refs, 2, r)
        o_ref[r] = _moba_decode_one(q_ref[r], kn_ref[r], vn_ref[r], bias_ref, bnew_ref, k_pages, v_pages)


def _moba_decode_one(qblk, k_new, v_new, bias_ref, bnew_ref, k_pages, v_pages):
    pages_per_block = MOBA_BLOCK // PAGE_SIZE
    n_blk = N_PAGES // pages_per_block
    means = []
    for blk in range(n_blk):
        tot = jnp.sum(k_pages[blk * pages_per_block][0], axis=0, keepdims=True)
        for p in range(blk * pages_per_block + 1, (blk + 1) * pages_per_block):
            tot = tot + jnp.sum(k_pages[p][0], axis=0, keepdims=True)
        means.append(tot * (1.0 / MOBA_BLOCK))
    k_mean = jnp.concatenate(means, axis=0)
    gate = lax.dot_general(qblk, k_mean, NT_DIMS, preferred_element_type=F32, precision=HIGHEST)
    chosen = _top_k_rows(gate, jnp.ones(gate.shape, jnp.bool_), MOBA_TOPK, n_blk)
    blk_add = jnp.where(chosen, 0.0, NEG_INF)

    def mask_of_page(p):
        b0 = p // pages_per_block
        return blk_add[:, b0:b0 + 1]

    o_full = _paged_attention(qblk, k_pages, v_pages, bias_ref, mask_of_page, k_new, v_new, bnew_ref[:, 0:1])
    return _group_lanes(o_full, 1)


def moba_decode(page_table, qblk, k_new, v_new, bias, bnew, cache_k, cache_v, layer):
    n_req = qblk.shape[0]
    in_specs = ([_row_spec(MOBA_HP, MOBA_W), _row_spec(1, MOBA_W), _row_spec(1, MOBA_W),
                 _const_spec(MOBA_HP, PAST_LEN), _const_spec(MOBA_HP, LANES)]
                + _page_specs(layer, MOBA_W) * 2)
    args = [qblk, k_new, v_new, bias, bnew] + [cache_k] * (REQ * N_PAGES) + [cache_v] * (REQ * N_PAGES)
    return _decode_call(_moba_decode_kernel, n_req, page_table, in_specs, args,
                        jax.ShapeDtypeStruct((n_req, MOBA_HP, HEAD_DIM), F32), _row_spec(MOBA_HP, HEAD_DIM))


NSA_HP = 16
GRP_ROWS = 8


def _chunk_rows(pages):
    per_page = PAGE_SIZE // CMP_STRIDE
    return jnp.concatenate(
        [jnp.concatenate([pg[0, pl.ds(r, per_page, stride=CMP_STRIDE), :] for pg in pages], axis=0)
         for r in range(CMP_STRIDE)], axis=1)


def _nsa_decode_kernel(pt_ref, q_ref, gt_ref, skn_ref, svn_ref, wkn_ref, wvn_ref, wkb_ref, wvb_ref,
                       pe_ref, w1_ref, w2_ref, gain_ref, seg_ref, cb_ref, sb_ref, wb_ref, bnew_ref,
                       ov_ref, ex_ref, g2_ref, p16_ref, *refs):
    o_ref = refs[4 * REQ * N_PAGES]
    consts = (pe_ref, w1_ref, w2_ref, gain_ref, seg_ref, cb_ref, sb_ref, wb_ref, bnew_ref, ov_ref, ex_ref, g2_ref,
              p16_ref)
    for r in range(REQ):
        o_ref[r] = _nsa_decode_one(q_ref[r], gt_ref[r], skn_ref[r], svn_ref[r], wkn_ref[r], wvn_ref[r], wkb_ref[r],
                                   wvb_ref[r], consts, _request_pages(refs, 4, r))


def _nsa_decode_one(qblk, gt, sk_new, sv_new, wk_new, wv_new, wk_buf, wv_buf, consts, pages):
    pe_ref, w1_ref, w2_ref, gain_ref, seg_ref, cb_ref, sb_ref, wb_ref, bnew_ref, ov_ref, ex_ref, g2_ref, p16_ref = consts
    ck_pages, cv_pages, sk_pages, sv_pages = pages
    new_add = bnew_ref[:, 0:1]
    lane = lax.broadcasted_iota(jnp.int32, (GRP_ROWS, LANES), 1)

    ck = _head_rms(_compress_one(_chunk_rows(ck_pages), pe_ref, w1_ref, w2_ref, 0), seg_ref, gain_ref[...])
    cv = _compress_one(_chunk_rows(cv_pages), pe_ref, w1_ref, w2_ref, 1)
    cvalid = lax.broadcasted_iota(jnp.int32, (NSA_HP, LANES), 1) < N_CMP
    lc = lax.dot_general(qblk, ck, NT_DIMS, preferred_element_type=F32, precision=HIGHEST)
    lc = jnp.where(cvalid, lc * QK_SCALE + cb_ref[...], NEG_INF)
    e = jnp.where(cvalid, jnp.exp(lc - jnp.max(lc, axis=1, keepdims=True)), 0.0)
    pc = e / jnp.maximum(jnp.sum(e, axis=1, keepdims=True), TINY)
    o_cmp = jnp.dot(pc.astype(BF16), cv.astype(BF16), preferred_element_type=F32)

    psum = jnp.dot(g2_ref[...], pc, preferred_element_type=F32, precision=HIGHEST)
    imp = jnp.dot(psum, ov_ref[...], preferred_element_type=F32, precision=HIGHEST)
    cur = PAST_LEN // SEL_BLOCK
    forced = jnp.logical_or(lane == 0, jnp.logical_or(lane == cur, lane == cur - 1))
    picked = _top_k_rows(jnp.where(forced, SEL_FORCE, imp), lane <= cur, N_SEL, cur + 1)
    pick_h =jnp.dot(p16_ref[...], jnp.where(picked, 1.0, 0.0).astype(BF16), preferred_element_type=F32)
    key_hit = jnp.dot(pick_h.astype(BF16), ex_ref[...], preferred_element_type=F32)
    key_add = jnp.where(key_hit > 0.5, 0.0, NEG_INF)

    o_slc = _paged_attention(qblk, sk_pages, sv_pages, sb_ref, lambda p: key_add[:, p * PAGE_SIZE:(p + 1) * PAGE_SIZE],
                             sk_new, sv_new, new_add)

    qb = (qblk * QK_SCALE).astype(BF16)
    sw = lax.dot_general(qb, wk_buf.astype(BF16), NT_DIMS, preferred_element_type=F32) + wb_ref[...]
    s_new = jnp.sum(qblk * wk_new, axis=1, keepdims=True) * QK_SCALE + new_add
    m = jnp.maximum(s_new, jnp.max(sw, axis=1, keepdims=True))
    ew = jnp.exp(sw - m)
    e_new = jnp.exp(s_new - m)
    l = e_new + jnp.sum(ew, axis=1, keepdims=True)
    o_win = e_new * wv_new + jnp.dot(ew.astype(BF16), wv_buf.astype(BF16), preferred_element_type=F32)
    o_win = o_win / jnp.maximum(l, TINY)

    o_full = gt[:, 0:1] * o_cmp + gt[:, 1:2] * o_slc + gt[:, 2:3] * o_win
    return _group_lanes(o_full, NSA_GROUP)


def nsa_decode(page_table, qblk, gates, sk_new, sv_new, wk_new, wv_new, win_k, win_v, cmp_w, gain_row,
               cbias, sbias, wbias, bnew, cache_ck, cache_cv, cache_sk, cache_sv, layer):
    n_req = qblk.shape[0]
    pe_x, w1_x, w2_x = cmp_w
    hh = jnp.arange(NSA_HP)
    g2 = jnp.logical_and(hh[None, :] // NSA_GROUP == jnp.arange(GRP_ROWS)[:, None], hh[None, :] < NSA_HEADS)
    p16 = (hh[:, None] // NSA_GROUP == jnp.arange(GRP_ROWS)[None, :]).astype(BF16)
    win_spec = pl.BlockSpec((REQ, WINDOW, NSA_KW), lambda b, pt: (b, 0, layer))
    in_specs = ([_row_spec(NSA_HP, NSA_KW), _row_spec(NSA_HP, LANES)] + [_row_spec(1, NSA_KW)] * 4
                + [win_spec, win_spec,
                   _const_spec(2, 2, CHUNK_W), _const_spec(2, CHUNK_W, 2 * HALF_W), _const_spec(2, HALF_W, NSA_KW),
                   _const_spec(1, NSA_KW), _const_spec(LANES, LANES),
                   _const_spec(NSA_HP, LANES), _const_spec(NSA_HP, PAST_LEN), _const_spec(NSA_HP, WINDOW),
                   _const_spec(NSA_HP, LANES), _const_spec(LANES, LANES), _const_spec(LANES, PAST_LEN),
                   _const_spec(GRP_ROWS, NSA_HP), _const_spec(NSA_HP, GRP_ROWS)]
                + _page_specs(layer, NSA_KW) * 4)
    args = ([qblk, gates, sk_new, sv_new, wk_new, wv_new, win_k, win_v, pe_x, w1_x, w2_x, gain_row, _seg_ones(),
             cbias, sbias, wbias, bnew, _overlap_matrix(), _block_expand(PAST_LEN), g2.astype(F32), p16]
            + [c for c in (cache_ck, cache_cv, cache_sk, cache_sv) for _ in range(REQ * N_PAGES)])
    return _decode_call(_nsa_decode_kernel, n_req, page_table, in_specs, args,
                        jax.ShapeDtypeStruct((n_req, NSA_HP, HEAD_DIM), F32), _row_spec(NSA_HP, HEAD_DIM))


KEYS_PAD = PAST_LEN + LANES


def _dsa_score_kernel(pt_ref, qi_ref, wi_ref, kin_ref, *refs):
    o_ref = refs[REQ * N_PAGES]
    lane = lax.broadcasted_iota(jnp.int32, (1, LANES), 1)
    for r in range(REQ):
        pages, = _request_pages(refs, 1, r)
        qi = qi_ref[r]
        wi = wi_ref[r][:, 0:1]
        parts = []
        for p in range(N_PAGES):
            s = lax.dot_general(qi, pages[p][0], NT_DIMS, preferred_element_type=F32, precision=HIGHEST)
            parts.append(jnp.sum(wi * jnp.maximum(s * IDX_DIM ** -0.5, 0.0), axis=0, keepdims=True))
        s_new = jnp.sum(qi * kin_ref[r], axis=1, keepdims=True) * IDX_DIM ** -0.5
        new = jnp.sum(wi * jnp.maximum(s_new, 0.0), axis=0, keepdims=True)
        parts.append(jnp.where(lane == 0, new, 0.0))
        o_ref[r] = jnp.concatenate(parts, axis=1) * IDX_HEADS ** -0.5


def dsa_score_decode(page_table, qi, wi, ki_new, cache_idx):
    n_req = qi.shape[0]
    in_specs = ([_row_spec(IDX_HEADS, LANES), _row_spec(IDX_HEADS, LANES), _row_spec(1, LANES)]
                + _page_specs(0, LANES))
    args = [qi, wi, ki_new] + [cache_idx] * (REQ * N_PAGES)
    return _decode_call(_dsa_score_kernel, n_req, page_table, in_specs, args,
                        jax.ShapeDtypeStruct((n_req, 1, KEYS_PAD), F32), _row_spec(1, KEYS_PAD))


def _topk_rows_kernel(k, s_ref, o_ref):
    score = s_ref[...]
    pos = lax.broadcasted_iota(jnp.int32, score.shape, 1)
    keep = _top_k_mask(score, pos <= PAST_LEN, pos, k)
    o_ref[...] = jnp.where(keep, 0.0, NEG_INF)


def dsa_topk_decode(score):
    k = min(DSA_TOPK, (PAST_LEN + 1) // 4)
    return pl.pallas_call(
        functools.partial(_topk_rows_kernel, k),
        out_shape=jax.ShapeDtypeStruct(score.shape, F32),
        compiler_params=pltpu.CompilerParams(vmem_limit_bytes=VMEM_LIMIT),
    )(score)


def _dsa_decode_kernel(pt_ref, q_ref, kn_ref, vn_ref, m_ref, bias_ref, bnew_ref, *refs):
    o_ref = refs[2 * REQ * N_PAGES]
    for r in range(REQ):
        k_pages, v_pages = _request_pages(refs, 2, r)
        mask = m_ref[r]
        new_add = bnew_ref[:, 0:1] + mask[:, PAST_LEN:PAST_LEN + 1]
        o_full = _paged_attention(q_ref[r], k_pages, v_pages, bias_ref,
                                  functools.partial(lambda m, p: m[:, p * PAGE_SIZE:(p + 1) * PAGE_SIZE], mask),
                                  kn_ref[r], vn_ref[r], new_add)
        o_ref[r] = _group_lanes(o_full, DSA_GROUP)


def dsa_attn_decode(page_table, qblk, k_new, v_new, mask, bias, bnew, cache_k, cache_v, layer):
    n_req = qblk.shape[0]
    in_specs = ([_row_spec(DSA_HEADS, DSA_KW), _row_spec(1, DSA_KW), _row_spec(1, DSA_KW), _row_spec(1, KEYS_PAD),
                 _const_spec(DSA_HEADS, PAST_LEN), _const_spec(DSA_HEADS, LANES)]
                + _page_specs(layer, DSA_KW) * 2)
    args = [qblk, k_new, v_new, mask, bias, bnew] + [cache_k] * (REQ * N_PAGES) + [cache_v] * (REQ * N_PAGES)
    return _decode_call(_dsa_decode_kernel, n_req, page_table, in_specs, args,
                        jax.ShapeDtypeStruct((n_req, DSA_HEADS, HEAD_DIM), F32), _row_spec(DSA_HEADS, HEAD_DIM))


EVEN_WP = 2432
EVEN_MODES = ("norm",) * 4 + ("raw",) * 2 + ("norm",) * 6 + ("raw", "raw", "norm", "raw", "norm", "raw", "sigmoid")
EVEN_CACHE_OUTS = (("cols", 256, 256), ("cols", 512, 256), ("cols", 1536, 128), ("cols", 1664, 128),
                   ("cols", 1792, 128), ("cols", 1920, 128), ("cols", 2048, 128), ("cols", 2176, 128))
EVEN_OUTS_PROMPT = EVEN_CACHE_OUTS + (
    ("cols", 2304, 128), ("heads", 0, 4, F32, 1.0), ("heads", 256, 4, F32, 1.0), ("heads", 512, 4, BF16, 1.0),
    ("heads", 768, 12, F32, 1.0), ("heads", 1792, 2, BF16, 1.0), ("heads", 1920, 2, BF16, 1.0),
    ("heads", 2048, 2, BF16, 1.0), ("heads", 2176, 2, BF16, 1.0))
EVEN_OUTS_DECODE = EVEN_CACHE_OUTS + (("cols", 2304, 128), ("cols", 0, 256), ("cols", 768, 768))
ODD_WP = 2304
ODD_MODES = ("norm",) * 10 + ("raw",) * 8
ODD_CACHE_OUTS = (("cols", 1024, 256), ("cols", 1280, 256), ("cols", 2048, 128))
ODD_OUTS_PROMPT = ODD_CACHE_OUTS + (
    ("cols", 2176, 128), ("heads", 0, 16, BF16, QK_SCALE), ("heads", 1024, 4, BF16, 1.0),
    ("heads", 1280, 4, BF16, 1.0), ("heads", 1536, 8, F32, 1.0))
ODD_OUTS_DECODE = ODD_CACHE_OUTS + (("cols", 2176, 128), ("cols", 0, 1024), ("cols", 1536, 512))


def _even_weights(w_in, moba_g, nsa_g):
    w = jnp.pad(w_in, ((0, 0), (0, EVEN_WP - w_in.shape[1]))).astype(BF16)
    z = lambda n: jnp.zeros((n,), F32)
    gain = jnp.concatenate([jnp.tile(moba_g[0], 4), jnp.tile(moba_g[1], 4), z(256), jnp.tile(nsa_g[0], 12), z(256),
                            jnp.tile(nsa_g[2], 2), z(128), jnp.tile(nsa_g[3], 2), z(256)])
    return w, gain.reshape(1, EVEN_WP)


def _odd_weights(w_in, qk_g):
    zc = lambda n: jnp.zeros((D_MODEL, n), F32)
    w = jnp.concatenate([w_in[:, :2112], zc(64), w_in[:, 2112:], zc(ODD_WP - 2176 - IDX_HEADS)], axis=1).astype(BF16)
    gain = jnp.concatenate([jnp.tile(qk_g[0], 16), jnp.tile(qk_g[1], 4), jnp.zeros((ODD_WP - 1280,), F32)])
    return w, gain.reshape(1, ODD_WP)


def _block_queries(q, n_heads, heads_per_group, rows):
    b = q.shape[0]
    n_groups = n_heads // heads_per_group
    qh = q.reshape(b, n_heads, 1, HEAD_DIM)
    grp = (jnp.arange(n_heads)[:, None] // heads_per_group == jnp.arange(n_groups)[None, :]).astype(F32)
    blk = (qh * grp[None, :, :, None]).reshape(b, n_heads, n_groups * HEAD_DIM)
    return jnp.pad(blk, ((0, 0), (0, rows - n_heads), (0, 0)))


def _decode_heads(o, n_heads):
    return o[:, :n_heads].transpose(1, 0, 2)[None]


def kernel(x_prompt, x_sample, cache_moba_k, cache_moba_v, cache_nsa_cmp_k, cache_nsa_cmp_v, cache_nsa_slc_k,
           cache_nsa_slc_v, state_nsa_win_k, state_nsa_win_v, cache_dsa_k, cache_dsa_v, cache_dsa_idx_k, page_table,
           c_prompt, c_sample, bias_table, norm_gain, ada_w, ada_b, ffn_w_in, ffn_w_out, even_w_in, even_w_out,
           moba_qk_gain, nsa_qk_gain, nsa_cmp_pe, nsa_cmp_w1, nsa_cmp_w2, odd_w_in, odd_w_out, dsa_qk_gain):
    bp, t, _ = x_prompt.shape
    bs = x_sample.shape[0]

    tab_m, tab_n = bias_table[:, :MOBA_HEADS], bias_table[:, MOBA_HEADS:]
    bias_m = _bias_tiles(tab_m, MOBA_BLOCK)[:, :3]
    bias_n = _bias_tiles(tab_n, TQ)
    bias_d = _bias_tiles(bias_table, TQ)
    cmp_end = jnp.arange(LANES) * CMP_STRIDE + CMP_LEN - 1
    cbias = _rel_bias(jnp.arange(t)[:, None] - cmp_end[None, :], tab_n)
    pad_rows = lambda a, rows: jnp.pad(a, ((0, rows - a.shape[0]), (0, 0)))
    past_dist = PAST_LEN - jnp.arange(PAST_LEN)
    dec_bias_m = pad_rows(_rel_bias(past_dist, tab_m), MOBA_HP)
    dec_bias_n = pad_rows(_rel_bias(past_dist, tab_n), NSA_HP)
    dec_bias_d = _rel_bias(past_dist, bias_table)
    dec_cbias = pad_rows(_rel_bias(PAST_LEN - cmp_end, tab_n), NSA_HP)
    win_dist = WINDOW - jnp.arange(WINDOW)
    dec_wbias = pad_rows(jnp.where(win_dist < WINDOW, _rel_bias(win_dist, tab_n), NEG_INF), NSA_HP)
    new_bias = jnp.broadcast_to(bias_table[0][:, None], (DSA_HEADS, LANES))
    bnew_m = pad_rows(new_bias[:MOBA_HEADS], MOBA_HP)
    bnew_n = pad_rows(new_bias[MOBA_HEADS:], NSA_HP)

    flat = lambda c: c.reshape(c.shape[0], c.shape[1], -1)
    pool_mk, pool_mv = flat(cache_moba_k), flat(cache_moba_v)
    pool_ck, pool_cv = flat(cache_nsa_cmp_k), flat(cache_nsa_cmp_v)
    pool_sk, pool_sv = flat(cache_nsa_slc_k), flat(cache_nsa_slc_v)
    pool_dk, pool_dv, pool_di = flat(cache_dsa_k), flat(cache_dsa_v), flat(cache_dsa_idx_k)
    win_k_all, win_v_all = flat(state_nsa_win_k), flat(state_nsa_win_v)

    mods = adaln_all(jnp.concatenate([c_prompt, c_sample], axis=0), ada_w, ada_b)
    mods = mods.reshape(DEPTH, bp + bs, 3, 3, D_MODEL)
    ffn_in = ffn_w_in.astype(BF16)
    ffn_out = ffn_w_out.astype(BF16)

    yp = x_prompt
    ys = x_sample.reshape(1, bs, D_MODEL)
    rows_ep = [[] for _ in range(8)]
    rows_es = [[] for _ in range(8)]
    rows_op = [[] for _ in range(3)]
    rows_os = [[] for _ in range(3)]
    for li in range(DEPTH):
        mp = mods[li, :bp][:, None]
        ms = mods[li, bp:][None]
        mod = lambda m, s, k: m[:, :, s, k]

        def ffn(y, m, s, w_idx):
            return ffn_half(y, mod(m, s, 0), mod(m, s, 1), mod(m, s, 2), norm_gain[li, s], ffn_in[li, w_idx],
                            ffn_out[li, w_idx])

        yp, ys = ffn(yp, mp, 0, 0), ffn(ys, ms, 0, 0)
        if li % 2 == 0:
            e = li // 2
            w_p, gain_cols = _even_weights(even_w_in[e], moba_qk_gain[e], nsa_qk_gain[e])
            w_o = even_w_out[e].astype(BF16)
            cmp_w = _compress_weights(nsa_cmp_pe[e], nsa_cmp_w1[e], nsa_cmp_w2[e])
            cmp_gain = jnp.tile(nsa_qk_gain[e, 1], NSA_KV).reshape(1, NSA_KW)
            project = lambda y, m, outs: mixer_project(y, mod(m, 1, 0), mod(m, 1, 1), norm_gain[li, 1], w_p,
                                                       gain_cols, EVEN_MODES, outs)
            pp = project(yp, mp, EVEN_OUTS_PROMPT)
            gl, mq_h, mk_h, mv_h, nq_h, sk_h, sv_h, wk_h, wv_h = pp[8:]
            o_m = moba_prompt(mq_h, mk_h, mv_h, bias_m)
            ck, cv = nsa_compress_prompt(pp[2], pp[3], *cmp_w, cmp_gain)
            o_n = nsa_prompt(nq_h, gl, ck, cv, sk_h, sv_h, wk_h, wv_h, bias_n, cbias)
            pieces_p = [o_m, o_n]
            wb = min(WINDOW, t)
            new_p = list(pp[:6]) + [pp[6][:, t - wb:], pp[7][:, t - wb:]]
            ps = [a.reshape(bs, 1, a.shape[-1]) for a in project(ys, ms, EVEN_OUTS_DECODE)]
            mk_s, mv_s, ckr_s, cvr_s, sk_s, sv_s, wk_s, wv_s, gl_s, mq_s, nq_s = ps
            o_ms = moba_decode(page_table, _block_queries(mq_s[:, 0], MOBA_HEADS, 1, MOBA_HP), mk_s, mv_s,
                               dec_bias_m, bnew_m, pool_mk, pool_mv, e)
            gates_s = jnp.pad(gl_s[:, 0, :3 * NSA_HEADS].reshape(bs, NSA_HEADS, 3),
                              ((0, 0), (0, NSA_HP - NSA_HEADS), (0, LANES - 3)))
            o_ns = nsa_decode(page_table, _block_queries(nq_s[:, 0], NSA_HEADS, NSA_GROUP, NSA_HP), gates_s,
                              sk_s, sv_s, wk_s, wv_s, win_k_all, win_v_all, cmp_w, cmp_gain,
                              dec_cbias, dec_bias_n, dec_wbias, bnew_n, pool_ck, pool_cv, pool_sk, pool_sv, e)
            pieces_s = [_decode_heads(o_ms, MOBA_HEADS), _decode_heads(o_ns, NSA_HEADS)]
            keep = min(WINDOW, PAST_LEN + 1)
            lanes_e = slice(e * NSA_KW, (e + 1) * NSA_KW)
            new_s = [mk_s, mv_s, ckr_s, cvr_s, sk_s, sv_s,
                     jnp.concatenate([win_k_all[:, :, lanes_e], wk_s], axis=1)[:, -keep:],
                     jnp.concatenate([win_v_all[:, :, lanes_e], wv_s], axis=1)[:, -keep:]]
            for acc, r in zip(rows_ep, new_p):
                acc.append(r)
            for acc, r in zip(rows_es, new_s):
                acc.append(r)
        else:
            o = li // 2
            w_p, gain_cols = _odd_weights(odd_w_in[o], dsa_qk_gain[o])
            w_o = odd_w_out[o].astype(BF16)
            project = lambda y, m, outs: mixer_project(y, mod(m, 1, 0), mod(m, 1, 1), norm_gain[li, 1], w_p,
                                                       gain_cols, ODD_MODES, outs)
            pp = project(yp, mp, ODD_OUTS_PROMPT)
            wi, q_h, k_h, v_h, qi_h = pp[3:]
            keep_mask = dsa_index_prompt(qi_h, wi, pp[2])
            pieces_p = [dsa_attn_prompt(q_h, k_h, v_h, bias_d, keep_mask)]
            ps = [a.reshape(bs, 1, a.shape[-1]) for a in project(ys, ms, ODD_OUTS_DECODE)]
            k_s, v_s, ki_s, wi_s, q_s, qi_s = ps
            idx_lanes = lambda a: jnp.pad(a, ((0, 0), (0, 0), (o * IDX_DIM, LANES - (o + 1) * IDX_DIM)))
            wi_rows = jnp.broadcast_to(wi_s[:, 0, :IDX_HEADS, None], (bs, IDX_HEADS, LANES))
            score = dsa_score_decode(page_table, idx_lanes(qi_s[:, 0].reshape(bs, IDX_HEADS, IDX_DIM)), wi_rows,
                                     idx_lanes(ki_s[:, :, :IDX_DIM]), pool_di)
            keep_s = dsa_topk_decode(score.reshape(bs, KEYS_PAD)).reshape(bs, 1, KEYS_PAD)
            o_ds = dsa_attn_decode(page_table, _block_queries(q_s[:, 0], DSA_HEADS, DSA_GROUP, DSA_HEADS), k_s, v_s,
                                   keep_s, dec_bias_d, new_bias, pool_dk, pool_dv, o)
            pieces_s = [_decode_heads(o_ds, DSA_HEADS)]
            for acc, r in zip(rows_op, (pp[0], pp[1], pp[2][..., :IDX_DIM])):
                acc.append(r)
            for acc, r in zip(rows_os, (k_s, v_s, ki_s[..., :IDX_DIM])):
                acc.append(r)
        yp = mixer_merge(yp, pieces_p, mod(mp, 1, 2), w_o)
        ys = mixer_merge(ys, pieces_s, mod(ms, 1, 2), w_o)
        yp, ys = ffn(yp, mp, 2, 1), ffn(ys, ms, 2, 1)

    def stack_layers(rows, n_heads):
        return jnp.stack([r.reshape(r.shape[0], r.shape[1], n_heads, HEAD_DIM) for r in rows], axis=2)

    even_heads = (MOBA_HEADS, MOBA_HEADS) + (NSA_KV,) * 6
    moba_k_p, moba_v_p, cmp_k_p, cmp_v_p, slc_k_p, slc_v_p, win_k_p, win_v_p = [
        stack_layers(r, n) for r, n in zip(rows_ep, even_heads)]
    moba_k_s, moba_v_s, cmp_k_s, cmp_v_s, slc_k_s, slc_v_s, win_k_s, win_v_s = [
        stack_layers(r, n) for r, n in zip(rows_es, even_heads)]
    dsa_k_p, dsa_v_p = stack_layers(rows_op[0], DSA_KV), stack_layers(rows_op[1], DSA_KV)
    dsa_k_s, dsa_v_s = stack_layers(rows_os[0], DSA_KV), stack_layers(rows_os[1], DSA_KV)
    dsa_idx_k_p, dsa_idx_k_s = jnp.stack(rows_op[2], axis=2), jnp.stack(rows_os[2], axis=2)
    return (yp, ys.reshape(bs, 1, D_MODEL), moba_k_p, moba_k_s, moba_v_p, moba_v_s, cmp_k_p, cmp_k_s, cmp_v_p,
            cmp_v_s, slc_k_p, slc_k_s, slc_v_p, slc_v_s, win_k_p, win_k_s, win_v_p, win_v_s,
            dsa_k_p, dsa_k_s, dsa_v_p, dsa_v_s, dsa_idx_k_p, dsa_idx_k_s)
```

```python
import functools
import math

import jax
import jax.numpy as jnp
from jax import lax
from jax.experimental import pallas as pl
from jax.experimental.pallas import tpu as pltpu

D_MODEL = 1024
DEPTH = 4
PAST_LEN = 2048
PAGE_SIZE = 128
N_PAGES = PAST_LEN // PAGE_SIZE
HEAD_DIM = 64
MOBA_HEADS = 4
NSA_HEADS = 12
NSA_KV = 2
NSA_GROUP = 6
DSA_HEADS = 16
DSA_KV = 4
DSA_GROUP = 4
IDX_HEADS = 8
IDX_DIM = 64
MOBA_BLOCK = 256
MOBA_TOPK = 3
CMP_LEN = 32
CMP_STRIDE = 16
CMP_HIDDEN = 128
SEL_BLOCK = 64
N_SEL = 8
WINDOW = 512
DSA_TOPK = 256
N_BUCKETS = 32
MAX_DISTANCE = 128
D_FF = 2816
MOBA_W = MOBA_HEADS * HEAD_DIM
NSA_W = NSA_HEADS * HEAD_DIM
NSA_KW = NSA_KV * HEAD_DIM
DSA_W = DSA_HEADS * HEAD_DIM
DSA_KW = DSA_KV * HEAD_DIM
RMS_EPS = 1e-6
NEG_INF = -1e30
TINY = 1e-30
SEL_FORCE = 1e4
QK_SCALE = HEAD_DIM ** -0.5

LANES = 128
VMEM_LIMIT = 56 * 1024 * 1024
BF16 = jnp.bfloat16
F32 = jnp.float32
HIGHEST = lax.Precision.HIGHEST
NT_DIMS = (((1,), (1,)), ((), ()))
M_INIT = -3e38
INT_MIN = -2 ** 31


def _cparams(*sem):
    return pltpu.CompilerParams(dimension_semantics=sem, vmem_limit_bytes=VMEM_LIMIT)


def _t5_bucket(dist):
    n = jnp.maximum(dist, 0)
    exact = N_BUCKETS // 2
    nf = jnp.maximum(n, 1).astype(F32)
    large = exact + (jnp.log(nf / exact) / math.log(MAX_DISTANCE / exact) * (N_BUCKETS - exact)).astype(jnp.int32)
    return jnp.where(n < exact, n, jnp.minimum(large, N_BUCKETS - 1))


def _rel_bias(dist, table):
    onehot = jax.nn.one_hot(_t5_bucket(dist), N_BUCKETS, dtype=F32)
    return jnp.einsum("...k,kh->h...", onehot, table, precision=HIGHEST)


def _adaln_kernel(c_ref, w_ref, b_ref, o_ref):
    c = c_ref[...]
    s = (c * jax.nn.sigmoid(c)).astype(BF16)
    o_ref[0] = jnp.dot(s, w_ref[0].astype(BF16), preferred_element_type=F32) + b_ref[0]


def adaln_all(c_all, ada_w, ada_b):
    r = c_all.shape[0]
    n_out = ada_w.shape[2]
    tn = 1024
    return pl.pallas_call(
        _adaln_kernel,
        grid=(DEPTH, n_out // tn),
        in_specs=[
            pl.BlockSpec((r, D_MODEL), lambda l, j: (0, 0)),
            pl.BlockSpec((1, D_MODEL, tn), lambda l, j: (l, 0, j)),
            pl.BlockSpec((1, 1, tn), lambda l, j: (l, 0, j)),
        ],
        out_specs=pl.BlockSpec((1, r, tn), lambda l, j: (l, 0, j)),
        out_shape=jax.ShapeDtypeStruct((DEPTH, r, n_out), F32),
        compiler_params=_cparams("parallel", "parallel"),
    )(c_all, ada_w, ada_b.reshape(DEPTH, 1, n_out))


def _modnorm(x, g, scale, shift):
    y = x * lax.rsqrt(jnp.mean(x * x, axis=-1, keepdims=True) + RMS_EPS)
    return (y * g) * (1.0 + scale) + shift


def _mod_spec(tm_rows, t_mod):
    if t_mod == 1:
        return pl.BlockSpec((1, 1, D_MODEL), lambda b, i, *_: (b, 0, 0))
    return pl.BlockSpec((1, tm_rows, D_MODEL), lambda b, i, *_: (b, i, 0))


def _seg_ones():
    return jnp.kron(jnp.eye(LANES // HEAD_DIM, dtype=F32), jnp.ones((HEAD_DIM, HEAD_DIM), F32))


def _head_rms(z, seg_ref, gain):
    ss = jnp.dot(z * z, seg_ref[...], preferred_element_type=F32, precision=HIGHEST)
    return z * lax.rsqrt(ss * (1.0 / HEAD_DIM) + RMS_EPS) * gain


def _ffn_kernel(x_ref, sh_ref, sc_ref, gt_ref, g_ref, wa_ref, wg_ref, wo_ref, o_ref, xn_ref, acc_ref):
    j = pl.program_id(2)

    @pl.when(j == 0)
    def _():
        xn_ref[...] = _modnorm(x_ref[0], g_ref[...], sc_ref[0], sh_ref[0]).astype(BF16)
        acc_ref[...] = jnp.zeros_like(acc_ref)

    xn = xn_ref[...]
    a = jnp.dot(xn, wa_ref[...], preferred_element_type=F32)
    g = jnp.dot(xn, wg_ref[...], preferred_element_type=F32)
    h = ((g * jax.nn.sigmoid(g)) * a).astype(BF16)
    acc_ref[...] += jnp.dot(h, wo_ref[...], preferred_element_type=F32)

    @pl.when(j == pl.num_programs(2) - 1)
    def _():
        o_ref[0] = x_ref[0] + (0.5 * gt_ref[0]) * acc_ref[...]


def ffn_half(x, shift, scale, gate, g, w_in, w_out):
    b, t, _ = x.shape
    tm = min(t, 512)
    fc = D_FF // 2
    nf = D_FF // fc
    mspec = _mod_spec(tm, shift.shape[1])
    return pl.pallas_call(
        _ffn_kernel,
        grid=(b, t // tm, nf),
        in_specs=[
            pl.BlockSpec((1, tm, D_MODEL), lambda bb, i, j: (bb, i, 0)),
            mspec, mspec, mspec,
            pl.BlockSpec((1, D_MODEL), lambda bb, i, j: (0, 0)),
            pl.BlockSpec((D_MODEL, fc), lambda bb, i, j: (0, j)),
            pl.BlockSpec((D_MODEL, fc), lambda bb, i, j: (0, j + nf)),
            pl.BlockSpec((fc, D_MODEL), lambda bb, i, j: (j, 0)),
        ],
        out_specs=pl.BlockSpec((1, tm, D_MODEL), lambda bb, i, j: (bb, i, 0)),
        out_shape=jax.ShapeDtypeStruct(x.shape, F32),
        scratch_shapes=[pltpu.VMEM((tm, D_MODEL), BF16), pltpu.VMEM((tm, D_MODEL), F32)],
        compiler_params=_cparams("parallel", "parallel", "arbitrary"),
    )(x, shift, scale, gate, g.reshape(1, D_MODEL), w_in, w_in, w_out)


def _proj_kernel(modes, outs, x_ref, sh_ref, sc_ref, g_ref, w_ref, gain_ref, seg_ref, *o_refs):
    h = _modnorm(x_ref[0], g_ref[...], sc_ref[0], sh_ref[0])
    z = jnp.dot(h.astype(BF16), w_ref[...], preferred_element_type=F32)
    chunks = []
    for c, mode in enumerate(modes):
        zc = z[:, c * LANES:(c + 1) * LANES]
        if mode == "norm":
            zc = _head_rms(zc, seg_ref, gain_ref[:, c * LANES:(c + 1) * LANES])
        elif mode == "sigmoid":
            zc = jax.nn.sigmoid(zc)
        chunks.append(zc)
    for o_ref, out in zip(o_refs, outs):
        if out[0] == "cols":
            _, start, width = out
            for c in range(width // LANES):
                o_ref[0, :, c * LANES:(c + 1) * LANES] = chunks[start // LANES + c]
        else:
            _, start, n_heads, dtype, scale = out
            for hh in range(n_heads):
                lo = start + hh * HEAD_DIM
                piece = chunks[lo // LANES][:, lo % LANES:lo % LANES + HEAD_DIM]
                o_ref[0, hh] = (piece * scale).astype(dtype)


def mixer_project(x, shift, scale, g, w, gain_cols, modes, outs):
    b, t, _ = x.shape
    tm = min(t, 512)
    wp = w.shape[1]
    mspec = _mod_spec(tm, shift.shape[1])
    out_shapes, out_specs = [], []
    for out in outs:
        if out[0] == "cols":
            out_shapes.append(jax.ShapeDtypeStruct((b, t, out[2]), F32))
            out_specs.append(pl.BlockSpec((1, tm, out[2]), lambda bb, i: (bb, i, 0)))
        else:
            out_shapes.append(jax.ShapeDtypeStruct((b, out[2], t, HEAD_DIM), out[3]))
            out_specs.append(pl.BlockSpec((1, out[2], tm, HEAD_DIM), lambda bb, i: (bb, 0, i, 0)))
    return pl.pallas_call(
        functools.partial(_proj_kernel, modes, outs),
        grid=(b, t // tm),
        in_specs=[
            pl.BlockSpec((1, tm, D_MODEL), lambda bb, i: (bb, i, 0)),
            mspec, mspec,
            pl.BlockSpec((1, D_MODEL), lambda bb, i: (0, 0)),
            pl.BlockSpec((D_MODEL, wp), lambda bb, i: (0, 0)),
            pl.BlockSpec((1, wp), lambda bb, i: (0, 0)),
            pl.BlockSpec((LANES, LANES), lambda bb, i: (0, 0)),
        ],
        out_specs=out_specs,
        out_shape=out_shapes,
        compiler_params=_cparams("parallel", "parallel"),
    )(x, shift, scale, g.reshape(1, D_MODEL), w, gain_cols, _seg_ones())


def _merge_kernel(n_pieces, y_ref, gt_ref, w_ref, *refs):
    out_ref = refs[n_pieces]
    heads = [refs[k][0, hh] for k in range(n_pieces) for hh in range(refs[k].shape[1])]
    o = jnp.concatenate(heads, axis=1).astype(BF16)
    out_ref[0] = y_ref[0] + gt_ref[0] * jnp.dot(o, w_ref[...], preferred_element_type=F32)


def mixer_merge(y, pieces, gate, w_out):
    b, t, _ = y.shape
    tm = min(t, 512)
    piece_specs = [pl.BlockSpec((1, p.shape[1], tm, HEAD_DIM), lambda bb, i: (bb, 0, i, 0)) for p in pieces]
    return pl.pallas_call(
        functools.partial(_merge_kernel, len(pieces)),
        grid=(b, t // tm),
        in_specs=[
            pl.BlockSpec((1, tm, D_MODEL), lambda bb, i: (bb, i, 0)),
            _mod_spec(tm, gate.shape[1]),
            pl.BlockSpec((D_MODEL, D_MODEL), lambda bb, i: (0, 0)),
        ] + piece_specs,
        out_specs=pl.BlockSpec((1, tm, D_MODEL), lambda bb, i: (bb, i, 0)),
        out_shape=jax.ShapeDtypeStruct(y.shape, F32),
        compiler_params=_cparams("parallel", "parallel"),
    )(y, gate, w_out, *pieces)


TQ = 128
CHUNK_TILES = 4
CK = CHUNK_TILES * TQ


def _bias_tiles(table, t):
    d = jnp.arange(t)[:, None] - jnp.arange(t)[None, :]
    diag = jnp.where(d >= 0, _rel_bias(d, table), NEG_INF)
    adj = _rel_bias(t + d, table)
    far = _rel_bias(2 * t + d, table)
    masked = jnp.full_like(far, NEG_INF)
    edge = jnp.where(d < 0, far, NEG_INF)
    return jnp.stack([diag, adj, far, masked, edge], axis=1)


def _tile_kind(d):
    return jnp.where(d < 0, 3, jnp.minimum(d, 2))


def _loop(n, body, init, static):
    if static:
        for c in range(n):
            init = body(c, init)
        return init
    return lax.fori_loop(0, n, body, init)


def _attend(qb, n_chunks, k_of, v_of, add_of, s_ref, static=False):
    rows = qb.shape[0]

    def cols(c, w):
        return pl.ds(c * w if static else pl.multiple_of(c * w, w), w)

    def score(c, mx):
        s = lax.dot_general(qb, k_of(c), NT_DIMS, preferred_element_type=F32) + add_of(c)
        w = s.shape[1]
        s_ref[:, cols(c, w)] = s
        for part in range(w // LANES):
            mx = jnp.maximum(mx, s[:, part * LANES:(part + 1) * LANES])
        return mx

    mx = _loop(n_chunks, score, jnp.full((rows, LANES), M_INIT, F32), static)
    m = jnp.max(mx, axis=1, keepdims=True)

    def accumulate(c, acc):
        v = v_of(c)
        w = v.shape[0]
        p = jnp.exp(s_ref[:, cols(c, w)] - m).astype(BF16)
        return acc + jnp.dot(p, jnp.concatenate([v, jnp.ones_like(v)], axis=1), preferred_element_type=F32)

    acc = _loop(n_chunks, accumulate, jnp.zeros((rows, 2 * HEAD_DIM), F32), static)
    return acc[:, :HEAD_DIM] / jnp.maximum(acc[:, HEAD_DIM:HEAD_DIM + 1], TINY)


def _top_k_rows(score, allowed, k, n_cand):
    lane = lax.broadcasted_iota(jnp.int32, score.shape, 1)
    s = jnp.where(allowed, score, -jnp.inf)
    ahead = jnp.zeros(score.shape, F32)
    for j in range(n_cand):
        col = s[:, j:j + 1]
        ahead = ahead + jnp.where(col > s, 1.0, jnp.where(col == s, jnp.where(lane > j, 1.0, 0.0), 0.0))
    return jnp.logical_and(allowed, ahead < float(k))


def _count(mask):
    return jnp.sum(jnp.where(mask, 1.0, 0.0), axis=1, keepdims=True)


def _top_k_mask(score, valid, pos, k):
    bits = pltpu.bitcast(score, jnp.int32)
    key = jnp.where(bits < 0, bits ^ jnp.int32(0x7FFFFFFF), bits)
    key = jnp.where(valid, key, jnp.int32(INT_MIN))
    kf = float(k)

    thr = jnp.full((score.shape[0], 1), INT_MIN, jnp.int32)
    cand = thr ^ jnp.int32(INT_MIN)
    thr = jnp.where(_count(key >= cand) >= kf, cand, thr)

    def step(b, thr):
        cand = thr | lax.shift_left(jnp.int32(1), 30 - b)
        return jnp.where(_count(key >= cand) >= kf, cand, thr)

    thr = lax.fori_loop(0, 31, step, thr)
    above = key > thr
    tie = jnp.logical_and(key == thr, valid)
    need = kf - _count(above)

    n_bits = max(int(score.shape[1] - 1).bit_length(), 1)
    all_pos = jnp.full((score.shape[0], 1), 2 ** n_bits - 1, jnp.int32)

    def tie_search():
        def tie_step(b, cut):
            cand = cut | lax.shift_left(jnp.int32(1), n_bits - 1 - b)
            return jnp.where(_count(jnp.logical_and(tie, pos < cand)) < need, cand, cut)

        return lax.fori_loop(0, n_bits, tie_step, jnp.zeros_like(all_pos))

    surplus = jnp.max(jnp.where(_count(tie) > need, 1.0, 0.0))
    cut = lax.cond(surplus > 0.0, tie_search, lambda: all_pos)
    keep = jnp.logical_or(above, jnp.logical_and(tie, pos <= cut))
    return jnp.logical_and(keep, valid)


def _moba_kernel(q_ref, k_ref, v_ref, bias_ref, o_ref, km_ref, s_ref):
    i = pl.program_id(2)
    nb = km_ref.shape[0]
    t = MOBA_BLOCK

    @pl.when(i == 0)
    def _():
        km_ref[...] = jnp.mean(k_ref[0, 0].reshape(nb, t, HEAD_DIM), axis=1)

    q = q_ref[0, 0]
    gate = lax.dot_general(q.astype(BF16), km_ref[...].astype(BF16), NT_DIMS, preferred_element_type=F32)
    blk = lax.broadcasted_iota(jnp.int32, gate.shape, 1)
    chosen = jnp.logical_or(_top_k_rows(gate, blk < i, MOBA_TOPK, nb), blk == i)
    blk_add = jnp.where(chosen, 0.0, NEG_INF)

    def rows_of(j):
        return pl.ds(pl.multiple_of(j * t, t), t)

    def add_of(j):
        col = jnp.sum(jnp.where(blk == j, blk_add, 0.0), axis=1, keepdims=True)
        return bias_ref[0, jnp.minimum(i - j, 2)] + col

    o_ref[0, 0] = _attend((q * QK_SCALE).astype(BF16), i + 1, lambda j: k_ref[0, 0, rows_of(j), :].astype(BF16),
                          lambda j: v_ref[0, 0, rows_of(j), :], add_of, s_ref)


def moba_prompt(q, k, v, bias):
    b, h, t, _ = q.shape
    tb = MOBA_BLOCK
    nb = t // tb
    full = pl.BlockSpec((1, 1, t, HEAD_DIM), lambda bb, hh, i: (bb, hh, 0, 0))
    return pl.pallas_call(
        _moba_kernel,
        grid=(b, h, nb),
        in_specs=[
            pl.BlockSpec((1, 1, tb, HEAD_DIM), lambda bb, hh, i: (bb, hh, i, 0)),
            full, full,
            pl.BlockSpec((1, 3, tb, tb), lambda bb, hh, i: (hh, 0, 0, 0)),
        ],
        out_specs=pl.BlockSpec((1, 1, tb, HEAD_DIM), lambda bb, hh, i: (bb, hh, i, 0)),
        out_shape=jax.ShapeDtypeStruct(q.shape, F32),
        scratch_shapes=[pltpu.VMEM((nb, HEAD_DIM), F32), pltpu.VMEM((tb, t), F32)],
        compiler_params=_cparams("parallel", "parallel", "arbitrary"),
    )(q, k, v, bias)


N_CHUNK = PAST_LEN // CMP_STRIDE
CHUNK_W = CMP_STRIDE * NSA_KW
N_CMP = (PAST_LEN - CMP_LEN) // CMP_STRIDE + 1
HALF_W = NSA_KV * CMP_HIDDEN


def _compress_weights(pe, w1, w2):
    eye = jnp.eye(NSA_KV, dtype=F32)
    pe_x = jnp.broadcast_to(pe.reshape(2, 2, CMP_STRIDE, 1, HEAD_DIM),
                            (2, 2, CMP_STRIDE, NSA_KV, HEAD_DIM)).reshape(2, 2, CHUNK_W)
    w1_h = w1.reshape(2, 2, CMP_STRIDE, HEAD_DIM, CMP_HIDDEN)
    w1_x = jnp.einsum("thrdj,gq->trgdhqj", w1_h, eye).reshape(2, CHUNK_W, 2 * HALF_W)
    w2_x = jnp.einsum("tjd,gq->tgjqd", w2, eye).reshape(2, HALF_W, NSA_KW)
    return pe_x, w1_x.astype(BF16), w2_x.astype(BF16)


def _compress_tail(ha, hb, w2):
    hid = jax.nn.gelu(ha + pltpu.roll(hb, N_CHUNK - 1, 0))
    return jnp.dot(hid.astype(BF16), w2, preferred_element_type=F32)


def _compress_one(r, pe_ref, w1_ref, w2_ref, t):
    ha = jnp.dot((r + pe_ref[t, 0:1]).astype(BF16), w1_ref[t, :, :HALF_W], preferred_element_type=F32)
    hb = jnp.dot((r + pe_ref[t, 1:2]).astype(BF16), w1_ref[t, :, HALF_W:], preferred_element_type=F32)
    return _compress_tail(ha, hb, w2_ref[t])


def _compress_kernel(rk_ref, rv_ref, pe_ref, w1_ref, w2_ref, gain_ref, seg_ref, ck_ref, cv_ref):
    ck_ref[0] = _head_rms(_compress_one(rk_ref[0], pe_ref, w1_ref, w2_ref, 0), seg_ref, gain_ref[...])
    cv_ref[0] = _compress_one(rv_ref[0], pe_ref, w1_ref, w2_ref, 1)


def nsa_compress_prompt(ck_raw, cv_raw, pe_x, w1_x, w2_x, gain_row):
    b = ck_raw.shape[0]
    rows = pl.BlockSpec((1, N_CHUNK, CHUNK_W), lambda bb: (bb, 0, 0))
    out = pl.BlockSpec((1, N_CHUNK, NSA_KW), lambda bb: (bb, 0, 0))
    const = lambda *shape: pl.BlockSpec(shape, lambda bb: (0,) * len(shape))
    return pl.pallas_call(
        _compress_kernel,
        grid=(b,),
        in_specs=[rows, rows, const(2, 2, CHUNK_W), const(2, CHUNK_W, 2 * HALF_W),
                  const(2, HALF_W, NSA_KW), const(1, NSA_KW), const(LANES, LANES)],
        out_specs=[out, out],
        out_shape=[jax.ShapeDtypeStruct((b, N_CHUNK, NSA_KW), F32)] * 2,
        compiler_params=_cparams("parallel"),
    )(ck_raw.reshape(b, N_CHUNK, CHUNK_W), cv_raw.reshape(b, N_CHUNK, CHUNK_W), pe_x, w1_x, w2_x, gain_row,
      _seg_ones())


def _overlap_matrix():
    c = jnp.arange(LANES)[:, None]
    n = jnp.arange(LANES)[None, :]
    return jnp.logical_and(c * CMP_STRIDE < n * SEL_BLOCK + SEL_BLOCK,
                           c * CMP_STRIDE + CMP_LEN > n * SEL_BLOCK).astype(F32)


def _block_expand(n_keys):
    return (jnp.arange(n_keys)[None, :] // SEL_BLOCK == jnp.arange(LANES)[:, None]).astype(BF16)


def _nsa_kernel(q_ref, gt_ref, ck_ref, cv_ref, sk_ref, sv_ref, wk_ref, wv_ref, bias_ref, cb_ref, ov_ref, ex_ref,
                o_ref, madd_ref, s_ref):
    g = pl.program_id(1)
    i = pl.program_id(2)
    hg = NSA_GROUP
    rows = hg * TQ
    nq = madd_ref.shape[0]
    row = lax.broadcasted_iota(jnp.int32, (TQ, LANES), 0)
    lane = lax.broadcasted_iota(jnp.int32, (TQ, LANES), 1)
    qpos = i * TQ + row
    qf = q_ref[0].reshape(rows, HEAD_DIM)

    def group_half(x):
        return jnp.where(g == 0, x[:, :HEAD_DIM], x[:, HEAD_DIM:])

    cmask = jnp.logical_and(qpos - (lane * CMP_STRIDE + CMP_LEN - 1) >= 0, lane < N_CMP)
    cmask_all = jnp.concatenate([cmask] * hg, axis=0)
    lc = lax.dot_general(qf.astype(BF16), group_half(ck_ref[0]).astype(BF16), NT_DIMS, preferred_element_type=F32)
    lc = jnp.where(cmask_all, lc * QK_SCALE + cb_ref[...].reshape(rows, LANES), NEG_INF)
    e = jnp.where(cmask_all, jnp.exp(lc - jnp.max(lc, axis=1, keepdims=True)), 0.0)
    pc = e / jnp.maximum(jnp.sum(e, axis=1, keepdims=True), TINY)
    o_cmp = jnp.dot(pc.astype(BF16), group_half(cv_ref[0]).astype(BF16), preferred_element_type=F32)
    psum = jnp.sum(pc.reshape(hg, TQ, LANES), axis=0)

    imp = jnp.dot(psum.astype(BF16), ov_ref[...], preferred_element_type=F32)
    cur = qpos // SEL_BLOCK
    forced = jnp.logical_or(lane == 0, jnp.logical_or(lane == cur, lane == cur - 1))
    picked = _top_k_rows(jnp.where(forced, SEL_FORCE, imp), lane <= cur, N_SEL, nq * TQ // SEL_BLOCK)
    pick_b = jnp.where(picked, 1.0, 0.0).astype(BF16)

    key_add = jnp.where(jnp.dot(pick_b, ex_ref[...], preferred_element_type=F32) > 0.5, 0.0, NEG_INF)
    for j in range(nq):
        madd_ref[j] = key_add[:, j * TQ:(j + 1) * TQ]

    qb = (qf * QK_SCALE).astype(BF16)

    def bias_rows(kind):
        return bias_ref[:, kind].reshape(rows, TQ)

    def key_tile(ref, j):
        return ref[0, 0, pl.ds(pl.multiple_of(j * TQ, TQ), TQ), :]

    def keys_of(c):
        return pl.ds(pl.multiple_of(c * CK, CK), CK)

    def slc_add(c):
        tiles = [CHUNK_TILES * c + k for k in range(CHUNK_TILES)]
        mask = jnp.concatenate([madd_ref[j] for j in tiles], axis=1)
        add = jnp.concatenate([bias_rows(_tile_kind(i - j)) for j in tiles], axis=1)
        return add + jnp.concatenate([mask] * hg, axis=0)

    o_slc = _attend(qb, i // CHUNK_TILES + 1, lambda c: sk_ref[0, 0, keys_of(c), :],
                    lambda c: sv_ref[0, 0, keys_of(c), :], slc_add, s_ref)

    n_win = WINDOW // TQ
    win_chunk = (n_win + 2) // 2

    def win_tiles(w):
        first = i - 2 * win_chunk + 1 + w * win_chunk
        return [first + k for k in range(win_chunk)]

    def win_keys(ref):
        return lambda w: jnp.concatenate([key_tile(ref, jnp.maximum(j, 0)) for j in win_tiles(w)], axis=0)

    def win_add(w):
        kinds = []
        for j in win_tiles(w):
            d = i - j
            kind = jnp.where(d == n_win, 4, jnp.where(d > n_win, 3, jnp.minimum(d, 2)))
            kinds.append(jnp.where(j < 0, 3, kind))
        return jnp.concatenate([bias_rows(kind) for kind in kinds], axis=1)

    o_win = _attend(qb, 2, win_keys(wk_ref), win_keys(wv_ref), win_add, s_ref, static=True)

    gt = gt_ref[0]

    def gate_col(branch):
        cols = []
        for h in range(hg):
            c0 = 3 * h + branch
            cols.append(jnp.where(g == 0, gt[:, c0:c0 + 1], gt[:, 3 * hg + c0:3 * hg + c0 + 1]))
        return jnp.concatenate(cols, axis=0)

    o = gate_col(0) * o_cmp + gate_col(1) * o_slc + gate_col(2) * o_win
    o_ref[0] = o.reshape(hg, TQ, HEAD_DIM)


def nsa_prompt(q, gates, ck, cv, sk, sv, wk, wv, bias, cbias):
    b, _, t, _ = q.shape
    hg = NSA_GROUP
    nq = t // TQ
    kv = pl.BlockSpec((1, 1, t, HEAD_DIM), lambda bb, gg, i: (bb, gg, 0, 0))
    cmp_spec = pl.BlockSpec((1, LANES, NSA_KW), lambda bb, gg, i: (bb, 0, 0))
    return pl.pallas_call(
        _nsa_kernel,
        grid=(b, NSA_KV, nq),
        in_specs=[
            pl.BlockSpec((1, hg, TQ, HEAD_DIM), lambda bb, gg, i: (bb, gg, i, 0)),
            pl.BlockSpec((1, TQ, LANES), lambda bb, gg, i: (bb, i, 0)),
            cmp_spec, cmp_spec, kv, kv, kv, kv,
            pl.BlockSpec((hg, 5, TQ, TQ), lambda bb, gg, i: (gg, 0, 0, 0)),
            pl.BlockSpec((hg, TQ, LANES), lambda bb, gg, i: (gg, i, 0)),
            pl.BlockSpec((LANES, LANES), lambda bb, gg, i: (0, 0)),
            pl.BlockSpec((LANES, t), lambda bb, gg, i: (0, 0)),
        ],
        out_specs=pl.BlockSpec((1, hg, TQ, HEAD_DIM), lambda bb, gg, i: (bb, gg, i, 0)),
        out_shape=jax.ShapeDtypeStruct(q.shape, F32),
        scratch_shapes=[pltpu.VMEM((nq, TQ, TQ), F32), pltpu.VMEM((hg * TQ, t), F32)],
        compiler_params=_cparams("parallel", "parallel", "arbitrary"),
    )(q, gates, ck, cv, sk, sv, wk, wv, bias, cbias, _overlap_matrix().astype(BF16), _block_expand(t))


def _dsa_index_kernel(qi_ref, wi_ref, ki_ref, o_ref):
    i = pl.program_id(1)
    t = ki_ref.shape[1]
    nq = t // TQ
    wi = wi_ref[0]

    def select(width):
        ki = ki_ref[0, :width, :IDX_DIM].astype(BF16)
        score = jnp.zeros((TQ, width), F32)
        for h in range(IDX_HEADS):
            s = lax.dot_general(qi_ref[0, h].astype(BF16), ki, NT_DIMS, preferred_element_type=F32)
            score = score + wi[:, h:h + 1] * jnp.maximum(s * IDX_DIM ** -0.5, 0.0)
        score = score * IDX_HEADS ** -0.5
        kpos = lax.broadcasted_iota(jnp.int32, (TQ, width), 1)
        qpos = i * TQ + lax.broadcasted_iota(jnp.int32, (TQ, width), 0)
        keep = _top_k_mask(score, kpos <= qpos, kpos, min(DSA_TOPK, t // 4))
        add = jnp.where(keep, 0.0, NEG_INF).astype(BF16)
        for j in range(nq):
            lo = j * TQ
            o_ref[0, 0, j] = add[:, lo:lo + TQ] if lo < width else jnp.full((TQ, TQ), NEG_INF, BF16)

    widths = [w for w in (t // 4, t // 2, t) if w % TQ == 0 and w >= TQ]
    lo_tile = 0
    for w in widths:
        hi_tile = w // TQ
        pl.when(jnp.logical_and(i >= lo_tile, i < hi_tile))(functools.partial(select, w))
        lo_tile = hi_tile


def dsa_index_prompt(qi, wi, ki):
    b, ih, t, _ = qi.shape
    nq = t // TQ
    return pl.pallas_call(
        _dsa_index_kernel,
        grid=(b, nq),
        in_specs=[
            pl.BlockSpec((1, ih, TQ, IDX_DIM), lambda bb, i: (bb, 0, i, 0)),
            pl.BlockSpec((1, TQ, LANES), lambda bb, i: (bb, i, 0)),
            pl.BlockSpec((1, t, LANES), lambda bb, i: (bb, 0, 0)),
        ],
        out_specs=pl.BlockSpec((1, 1, nq, TQ, TQ), lambda bb, i: (bb, i, 0, 0, 0)),
        out_shape=jax.ShapeDtypeStruct((b, nq, nq, TQ, TQ), BF16),
        compiler_params=_cparams("parallel", "parallel"),
    )(qi, wi, ki)


def _dsa_attn_kernel(q_ref, k_ref, v_ref, bias_ref, m_ref, o_ref, s_ref):
    i = pl.program_id(2)
    hg = DSA_GROUP
    rows = hg * TQ

    def keys_of(c):
        return pl.ds(pl.multiple_of(c * CK, CK), CK)

    def add_of(c):
        tiles = [CHUNK_TILES * c + k for k in range(CHUNK_TILES)]
        mask = jnp.concatenate([m_ref[0, 0, j] for j in tiles], axis=1).astype(F32)
        add = jnp.concatenate([bias_ref[:, _tile_kind(i - j)].reshape(rows, TQ) for j in tiles], axis=1)
        return add + jnp.concatenate([mask] * hg, axis=0)

    o = _attend(q_ref[0].reshape(rows, HEAD_DIM), i // CHUNK_TILES + 1, lambda c: k_ref[0, 0, keys_of(c), :],
                lambda c: v_ref[0, 0, keys_of(c), :], add_of, s_ref)
    o_ref[0] = o.reshape(hg, TQ, HEAD_DIM)


def dsa_attn_prompt(q, k, v, bias, mask):
    b, _, t, _ = q.shape
    hg = DSA_GROUP
    nq = t // TQ
    kv = pl.BlockSpec((1, 1, t, HEAD_DIM), lambda bb, gg, i: (bb, gg, 0, 0))
    return pl.pallas_call(
        _dsa_attn_kernel,
        grid=(b, DSA_KV, nq),
        in_specs=[
            pl.BlockSpec((1, hg, TQ, HEAD_DIM), lambda bb, gg, i: (bb, gg, i, 0)),
            kv, kv,
            pl.BlockSpec((hg, 5, TQ, TQ), lambda bb, gg, i: (gg, 0, 0, 0)),
            pl.BlockSpec((1, 1, nq, TQ, TQ), lambda bb, gg, i: (bb, i, 0, 0, 0)),
        ],
        out_specs=pl.BlockSpec((1, hg, TQ, HEAD_DIM), lambda bb, gg, i: (bb, gg, i, 0)),
        out_shape=jax.ShapeDtypeStruct(q.shape, F32),
        scratch_shapes=[pltpu.VMEM((hg * TQ, t), F32)],
        compiler_params=_cparams("parallel", "parallel", "arbitrary"),
    )(q, k, v, bias, mask)


REQ = 2


def _row_spec(*shape):
    return pl.BlockSpec((REQ,) + shape, lambda b, pt: (b,) + (0,) * len(shape))


def _const_spec(*shape):
    return pl.BlockSpec(shape, lambda b, pt: (0,) * len(shape))


def _decode_call(kernel, n_req, page_table, in_specs, args, out_shapes, out_specs):
    return pl.pallas_call(
        kernel,
        grid_spec=pltpu.PrefetchScalarGridSpec(num_scalar_prefetch=1, grid=(n_req // REQ,), in_specs=in_specs,
                                               out_specs=out_specs),
        out_shape=out_shapes,
        compiler_params=_cparams("parallel"),
    )(page_table, *args)


def _page_specs(lane_block, width):
    return [pl.BlockSpec((1, PAGE_SIZE, width),
                         functools.partial(lambda r, p, b, pt: (pt[b * REQ + r, p], 0, lane_block), r, p))
            for r in range(REQ) for p in range(N_PAGES)]


def _request_pages(refs, n_tensors, r):
    per = REQ * N_PAGES
    return [refs[t * per + r * N_PAGES:t * per + (r + 1) * N_PAGES] for t in range(n_tensors)]


def _paged_attention(qblk, k_pages, v_pages, bias_ref, mask_of_page, k_new, v_new, new_add):
    qb = (qblk * QK_SCALE).astype(BF16)
    s_new = jnp.sum(qblk * k_new, axis=1, keepdims=True) * QK_SCALE + new_add
    s_pages = []
    for p in range(N_PAGES):
        s = lax.dot_general(qb, k_pages[p][0].astype(BF16), NT_DIMS, preferred_element_type=F32)
        s_pages.append(s + bias_ref[:, p * PAGE_SIZE:(p + 1) * PAGE_SIZE] + mask_of_page(p))
    m = s_new
    for s in s_pages:
        m = jnp.maximum(m, jnp.max(s, axis=1, keepdims=True))
    e_new = jnp.exp(s_new - m)
    l = e_new
    acc = e_new * v_new
    for p in range(N_PAGES):
        e = jnp.exp(s_pages[p] - m)
        l = l + jnp.sum(e, axis=1, keepdims=True)
        acc = acc + jnp.dot(e.astype(BF16), v_pages[p][0].astype(BF16), preferred_element_type=F32)
    return acc / jnp.maximum(l, TINY)


def _group_lanes(o_full, heads_per_group):
    hp, w = o_full.shape
    grp = lax.broadcasted_iota(jnp.int32, (hp, HEAD_DIM), 0) // heads_per_group
    out = o_full[:, :HEAD_DIM]
    for gg in range(1, w // HEAD_DIM):
        out = jnp.where(grp == gg, o_full[:, gg * HEAD_DIM:(gg + 1) * HEAD_DIM], out)
    return out


MOBA_HP = 8


def _moba_decode_kernel(pt_ref, q_ref, kn_ref, vn_ref, bias_ref, bnew_ref, *refs):
    o_ref = refs[2 * REQ * N_PAGES]
    for r in range(REQ):
        k_pages, v_pages = _request_pages(refs, 2, r)
        o_ref[r] = _moba_decode_one(q_ref[r], kn_ref[r], vn_ref[r], bias_ref, bnew_ref, k_pages, v_pages)


def _moba_decode_one(qblk, k_new, v_new, bias_ref, bnew_ref, k_pages, v_pages):
    pages_per_block = MOBA_BLOCK // PAGE_SIZE
    n_blk = N_PAGES // pages_per_block
    means = []
    for blk in range(n_blk):
        tot = jnp.sum(k_pages[blk * pages_per_block][0], axis=0, keepdims=True)
        for p in range(blk * pages_per_block + 1, (blk + 1) * pages_per_block):
            tot = tot + jnp.sum(k_pages[p][0], axis=0, keepdims=True)
        means.append(tot * (1.0 / MOBA_BLOCK))
    k_mean = jnp.concatenate(means, axis=0)
    gate = lax.dot_general(qblk, k_mean, NT_DIMS, preferred_element_type=F32, precision=HIGHEST)
    chosen = _top_k_rows(gate, jnp.ones(gate.shape, jnp.bool_), MOBA_TOPK, n_blk)
    blk_add = jnp.where(chosen, 0.0, NEG_INF)

    def mask_of_page(p):
        b0 = p // pages_per_block
        return blk_add[:, b0:b0 + 1]

    o_full = _paged_attention(qblk, k_pages, v_pages, bias_ref, mask_of_page, k_new, v_new, bnew_ref[:, 0:1])
    return _group_lanes(o_full, 1)


def moba_decode(page_table, qblk, k_new, v_new, bias, bnew, cache_k, cache_v, layer):
    n_req = qblk.shape[0]
    in_specs = ([_row_spec(MOBA_HP, MOBA_W), _row_spec(1, MOBA_W), _row_spec(1, MOBA_W),
                 _const_spec(MOBA_HP, PAST_LEN), _const_spec(MOBA_HP, LANES)]
                + _page_specs(layer, MOBA_W) * 2)
    args = [qblk, k_new, v_new, bias, bnew] + [cache_k] * (REQ * N_PAGES) + [cache_v] * (REQ * N_PAGES)
    return _decode_call(_moba_decode_kernel, n_req, page_table, in_specs, args,
                        jax.ShapeDtypeStruct((n_req, MOBA_HP, HEAD_DIM), F32), _row_spec(MOBA_HP, HEAD_DIM))


NSA_HP = 16
GRP_ROWS = 8


def _chunk_rows(pages):
    per_page = PAGE_SIZE // CMP_STRIDE
    return jnp.concatenate(
        [jnp.concatenate([pg[0, pl.ds(r, per_page, stride=CMP_STRIDE), :] for pg in pages], axis=0)
         for r in range(CMP_STRIDE)], axis=1)


def _nsa_decode_kernel(pt_ref, q_ref, gt_ref, skn_ref, svn_ref, wkn_ref, wvn_ref, wkb_ref, wvb_ref,
                       pe_ref, w1_ref, w2_ref, gain_ref, seg_ref, cb_ref, sb_ref, wb_ref, bnew_ref,
                       ov_ref, ex_ref, g2_ref, p16_ref, *refs):
    o_ref = refs[4 * REQ * N_PAGES]
    consts = (pe_ref, w1_ref, w2_ref, gain_ref, seg_ref, cb_ref, sb_ref, wb_ref, bnew_ref, ov_ref, ex_ref, g2_ref,
              p16_ref)
    for r in range(REQ):
        o_ref[r] = _nsa_decode_one(q_ref[r], gt_ref[r], skn_ref[r], svn_ref[r], wkn_ref[r], wvn_ref[r], wkb_ref[r],
                                   wvb_ref[r], consts, _request_pages(refs, 4, r))


def _nsa_decode_one(qblk, gt, sk_new, sv_new, wk_new, wv_new, wk_buf, wv_buf, consts, pages):
    pe_ref, w1_ref, w2_ref, gain_ref, seg_ref, cb_ref, sb_ref, wb_ref, bnew_ref, ov_ref, ex_ref, g2_ref, p16_ref = consts
    ck_pages, cv_pages, sk_pages, sv_pages = pages
    new_add = bnew_ref[:, 0:1]
    lane = lax.broadcasted_iota(jnp.int32, (GRP_ROWS, LANES), 1)

    ck = _head_rms(_compress_one(_chunk_rows(ck_pages), pe_ref, w1_ref, w2_ref, 0), seg_ref, gain_ref[...])
    cv = _compress_one(_chunk_rows(cv_pages), pe_ref, w1_ref, w2_ref, 1)
    cvalid = lax.broadcasted_iota(jnp.int32, (NSA_HP, LANES), 1) < N_CMP
    lc = lax.dot_general(qblk, ck, NT_DIMS, preferred_element_type=F32, precision=HIGHEST)
    lc = jnp.where(cvalid, lc * QK_SCALE + cb_ref[...], NEG_INF)
    e = jnp.where(cvalid, jnp.exp(lc - jnp.max(lc, axis=1, keepdims=True)), 0.0)
    pc = e / jnp.maximum(jnp.sum(e, axis=1, keepdims=True), TINY)
    o_cmp = jnp.dot(pc.astype(BF16), cv.astype(BF16), preferred_element_type=F32)

    psum = jnp.dot(g2_ref[...], pc, preferred_element_type=F32, precision=HIGHEST)
    imp = jnp.dot(psum, ov_ref[...], preferred_element_type=F32, precision=HIGHEST)
    cur = PAST_LEN // SEL_BLOCK
    forced = jnp.logical_or(lane == 0, jnp.logical_or(lane == cur, lane == cur - 1))
    picked = _top_k_rows(jnp.where(forced, SEL_FORCE, imp), lane <= cur, N_SEL, cur + 1)
    pick_h =jnp.dot(p16_ref[...], jnp.where(picked, 1.0, 0.0).astype(BF16), preferred_element_type=F32)
    key_hit = jnp.dot(pick_h.astype(BF16), ex_ref[...], preferred_element_type=F32)
    key_add = jnp.where(key_hit > 0.5, 0.0, NEG_INF)

    o_slc = _paged_attention(qblk, sk_pages, sv_pages, sb_ref, lambda p: key_add[:, p * PAGE_SIZE:(p + 1) * PAGE_SIZE],
                             sk_new, sv_new, new_add)

    qb = (qblk * QK_SCALE).astype(BF16)
    sw = lax.dot_general(qb, wk_buf.astype(BF16), NT_DIMS, preferred_element_type=F32) + wb_ref[...]
    s_new = jnp.sum(qblk * wk_new, axis=1, keepdims=True) * QK_SCALE + new_add
    m = jnp.maximum(s_new, jnp.max(sw, axis=1, keepdims=True))
    ew = jnp.exp(sw - m)
    e_new = jnp.exp(s_new - m)
    l = e_new + jnp.sum(ew, axis=1, keepdims=True)
    o_win = e_new * wv_new + jnp.dot(ew.astype(BF16), wv_buf.astype(BF16), preferred_element_type=F32)
    o_win = o_win / jnp.maximum(l, TINY)

    o_full = gt[:, 0:1] * o_cmp + gt[:, 1:2] * o_slc + gt[:, 2:3] * o_win
    return _group_lanes(o_full, NSA_GROUP)


def nsa_decode(page_table, qblk, gates, sk_new, sv_new, wk_new, wv_new, win_k, win_v, cmp_w, gain_row,
               cbias, sbias, wbias, bnew, cache_ck, cache_cv, cache_sk, cache_sv, layer):
    n_req = qblk.shape[0]
    pe_x, w1_x, w2_x = cmp_w
    hh = jnp.arange(NSA_HP)
    g2 = jnp.logical_and(hh[None, :] // NSA_GROUP == jnp.arange(GRP_ROWS)[:, None], hh[None, :] < NSA_HEADS)
    p16 = (hh[:, None] // NSA_GROUP == jnp.arange(GRP_ROWS)[None, :]).astype(BF16)
    win_spec = pl.BlockSpec((REQ, WINDOW, NSA_KW), lambda b, pt: (b, 0, layer))
    in_specs = ([_row_spec(NSA_HP, NSA_KW), _row_spec(NSA_HP, LANES)] + [_row_spec(1, NSA_KW)] * 4
                + [win_spec, win_spec,
                   _const_spec(2, 2, CHUNK_W), _const_spec(2, CHUNK_W, 2 * HALF_W), _const_spec(2, HALF_W, NSA_KW),
                   _const_spec(1, NSA_KW), _const_spec(LANES, LANES),
                   _const_spec(NSA_HP, LANES), _const_spec(NSA_HP, PAST_LEN), _const_spec(NSA_HP, WINDOW),
                   _const_spec(NSA_HP, LANES), _const_spec(LANES, LANES), _const_spec(LANES, PAST_LEN),
                   _const_spec(GRP_ROWS, NSA_HP), _const_spec(NSA_HP, GRP_ROWS)]
                + _page_specs(layer, NSA_KW) * 4)
    args = ([qblk, gates, sk_new, sv_new, wk_new, wv_new, win_k, win_v, pe_x, w1_x, w2_x, gain_row, _seg_ones(),
             cbias, sbias, wbias, bnew, _overlap_matrix(), _block_expand(PAST_LEN), g2.astype(F32), p16]
            + [c for c in (cache_ck, cache_cv, cache_sk, cache_sv) for _ in range(REQ * N_PAGES)])
    return _decode_call(_nsa_decode_kernel, n_req, page_table, in_specs, args,
                        jax.ShapeDtypeStruct((n_req, NSA_HP, HEAD_DIM), F32), _row_spec(NSA_HP, HEAD_DIM))


KEYS_PAD = PAST_LEN + LANES


def _dsa_score_kernel(pt_ref, qi_ref, wi_ref, kin_ref, *refs):
    o_ref = refs[REQ * N_PAGES]
    lane = lax.broadcasted_iota(jnp.int32, (1, LANES), 1)
    for r in range(REQ):
        pages, = _request_pages(refs, 1, r)
        qi = qi_ref[r]
        wi = wi_ref[r][:, 0:1]
        parts = []
        for p in range(N_PAGES):
            s = lax.dot_general(qi, pages[p][0], NT_DIMS, preferred_element_type=F32, precision=HIGHEST)
            parts.append(jnp.sum(wi * jnp.maximum(s * IDX_DIM ** -0.5, 0.0), axis=0, keepdims=True))
        s_new = jnp.sum(qi * kin_ref[r], axis=1, keepdims=True) * IDX_DIM ** -0.5
        new = jnp.sum(wi * jnp.maximum(s_new, 0.0), axis=0, keepdims=True)
        parts.append(jnp.where(lane == 0, new, 0.0))
        o_ref[r] = jnp.concatenate(parts, axis=1) * IDX_HEADS ** -0.5


def dsa_score_decode(page_table, qi, wi, ki_new, cache_idx):
    n_req = qi.shape[0]
    in_specs = ([_row_spec(IDX_HEADS, LANES), _row_spec(IDX_HEADS, LANES), _row_spec(1, LANES)]
                + _page_specs(0, LANES))
    args = [qi, wi, ki_new] + [cache_idx] * (REQ * N_PAGES)
    return _decode_call(_dsa_score_kernel, n_req, page_table, in_specs, args,
                        jax.ShapeDtypeStruct((n_req, 1, KEYS_PAD), F32), _row_spec(1, KEYS_PAD))


def _topk_rows_kernel(k, s_ref, o_ref):
    score = s_ref[...]
    pos = lax.broadcasted_iota(jnp.int32, score.shape, 1)
    keep = _top_k_mask(score, pos <= PAST_LEN, pos, k)
    o_ref[...] = jnp.where(keep, 0.0, NEG_INF)


def dsa_topk_decode(score):
    k = min(DSA_TOPK, (PAST_LEN + 1) // 4)
    return pl.pallas_call(
        functools.partial(_topk_rows_kernel, k),
        out_shape=jax.ShapeDtypeStruct(score.shape, F32),
        compiler_params=pltpu.CompilerParams(vmem_limit_bytes=VMEM_LIMIT),
    )(score)


def _dsa_decode_kernel(pt_ref, q_ref, kn_ref, vn_ref, m_ref, bias_ref, bnew_ref, *refs):
    o_ref = refs[2 * REQ * N_PAGES]
    for r in range(REQ):
        k_pages, v_pages = _request_pages(refs, 2, r)
        mask = m_ref[r]
        new_add = bnew_ref[:, 0:1] + mask[:, PAST_LEN:PAST_LEN + 1]
        o_full = _paged_attention(q_ref[r], k_pages, v_pages, bias_ref,
                                  functools.partial(lambda m, p: m[:, p * PAGE_SIZE:(p + 1) * PAGE_SIZE], mask),
                                  kn_ref[r], vn_ref[r], new_add)
        o_ref[r] = _group_lanes(o_full, DSA_GROUP)


def dsa_attn_decode(page_table, qblk, k_new, v_new, mask, bias, bnew, cache_k, cache_v, layer):
    n_req = qblk.shape[0]
    in_specs = ([_row_spec(DSA_HEADS, DSA_KW), _row_spec(1, DSA_KW), _row_spec(1, DSA_KW), _row_spec(1, KEYS_PAD),
                 _const_spec(DSA_HEADS, PAST_LEN), _const_spec(DSA_HEADS, LANES)]
                + _page_specs(layer, DSA_KW) * 2)
    args = [qblk, k_new, v_new, mask, bias, bnew] + [cache_k] * (REQ * N_PAGES) + [cache_v] * (REQ * N_PAGES)
    return _decode_call(_dsa_decode_kernel, n_req, page_table, in_specs, args,
                        jax.ShapeDtypeStruct((n_req, DSA_HEADS, HEAD_DIM), F32), _row_spec(DSA_HEADS, HEAD_DIM))


EVEN_WP = 2432
EVEN_MODES = ("norm",) * 4 + ("raw",) * 2 + ("norm",) * 6 + ("raw", "raw", "norm", "raw", "norm", "raw", "sigmoid")
EVEN_CACHE_OUTS = (("cols", 256, 256), ("cols", 512, 256), ("cols", 1536, 128), ("cols", 1664, 128),
                   ("cols", 1792, 128), ("cols", 1920, 128), ("cols", 2048, 128), ("cols", 2176, 128))
EVEN_OUTS_PROMPT = EVEN_CACHE_OUTS + (
    ("cols", 2304, 128), ("heads", 0, 4, F32, 1.0), ("heads", 256, 4, F32, 1.0), ("heads", 512, 4, BF16, 1.0),
    ("heads", 768, 12, F32, 1.0), ("heads", 1792, 2, BF16, 1.0), ("heads", 1920, 2, BF16, 1.0),
    ("heads", 2048, 2, BF16, 1.0), ("heads", 2176, 2, BF16, 1.0))
EVEN_OUTS_DECODE = EVEN_CACHE_OUTS + (("cols", 2304, 128), ("cols", 0, 256), ("cols", 768, 768))
ODD_WP = 2304
ODD_MODES = ("norm",) * 10 + ("raw",) * 8
ODD_CACHE_OUTS = (("cols", 1024, 256), ("cols", 1280, 256), ("cols", 2048, 128))
ODD_OUTS_PROMPT = ODD_CACHE_OUTS + (
    ("cols", 2176, 128), ("heads", 0, 16, BF16, QK_SCALE), ("heads", 1024, 4, BF16, 1.0),
    ("heads", 1280, 4, BF16, 1.0), ("heads", 1536, 8, F32, 1.0))
ODD_OUTS_DECODE = ODD_CACHE_OUTS + (("cols", 2176, 128), ("cols", 0, 1024), ("cols", 1536, 512))


def _even_weights(w_in, moba_g, nsa_g):
    w = jnp.pad(w_in, ((0, 0), (0, EVEN_WP - w_in.shape[1]))).astype(BF16)
    z = lambda n: jnp.zeros((n,), F32)
    gain = jnp.concatenate([jnp.tile(moba_g[0], 4), jnp.tile(moba_g[1], 4), z(256), jnp.tile(nsa_g[0], 12), z(256),
                            jnp.tile(nsa_g[2], 2), z(128), jnp.tile(nsa_g[3], 2), z(256)])
    return w, gain.reshape(1, EVEN_WP)


def _odd_weights(w_in, qk_g):
    zc = lambda n: jnp.zeros((D_MODEL, n), F32)
    w = jnp.concatenate([w_in[:, :2112], zc(64), w_in[:, 2112:], zc(ODD_WP - 2176 - IDX_HEADS)], axis=1).astype(BF16)
    gain = jnp.concatenate([jnp.tile(qk_g[0], 16), jnp.tile(qk_g[1], 4), jnp.zeros((ODD_WP - 1280,), F32)])
    return w, gain.reshape(1, ODD_WP)


def _block_queries(q, n_heads, heads_per_group, rows):
    b = q.shape[0]
    n_groups = n_heads // heads_per_group
    qh = q.reshape(b, n_heads, 1, HEAD_DIM)
    grp = (jnp.arange(n_heads)[:, None] // heads_per_group == jnp.arange(n_groups)[None, :]).astype(F32)
    blk = (qh * grp[None, :, :, None]).reshape(b, n_heads, n_groups * HEAD_DIM)
    return jnp.pad(blk, ((0, 0), (0, rows - n_heads), (0, 0)))


def _decode_heads(o, n_heads):
    return o[:, :n_heads].transpose(1, 0, 2)[None]


def kernel(x_prompt, x_sample, cache_moba_k, cache_moba_v, cache_nsa_cmp_k, cache_nsa_cmp_v, cache_nsa_slc_k,
           cache_nsa_slc_v, state_nsa_win_k, state_nsa_win_v, cache_dsa_k, cache_dsa_v, cache_dsa_idx_k, page_table,
           c_prompt, c_sample, bias_table, norm_gain, ada_w, ada_b, ffn_w_in, ffn_w_out, even_w_in, even_w_out,
           moba_qk_gain, nsa_qk_gain, nsa_cmp_pe, nsa_cmp_w1, nsa_cmp_w2, odd_w_in, odd_w_out, dsa_qk_gain):
    bp, t, _ = x_prompt.shape
    bs = x_sample.shape[0]

    tab_m, tab_n = bias_table[:, :MOBA_HEADS], bias_table[:, MOBA_HEADS:]
    bias_m = _bias_tiles(tab_m, MOBA_BLOCK)[:, :3]
    bias_d = _bias_tiles(bias_table, TQ)
    bias_n = bias_d[MOBA_HEADS:]
    cmp_end = jnp.arange(LANES) * CMP_STRIDE + CMP_LEN - 1
    cbias = _rel_bias(jnp.arange(t)[:, None] - cmp_end[None, :], tab_n)
    pad_rows = lambda a, rows: jnp.pad(a, ((0, rows - a.shape[0]), (0, 0)))
    dec_bias_d = _rel_bias(PAST_LEN - jnp.arange(PAST_LEN), bias_table)
    dec_bias_m = pad_rows(dec_bias_d[:MOBA_HEADS], MOBA_HP)
    dec_bias_n = pad_rows(dec_bias_d[MOBA_HEADS:], NSA_HP)
    dec_cbias = pad_rows(_rel_bias(PAST_LEN - cmp_end, tab_n), NSA_HP)
    win_dist = WINDOW - jnp.arange(WINDOW)
    dec_wbias = pad_rows(jnp.where(win_dist < WINDOW, _rel_bias(win_dist, tab_n), NEG_INF), NSA_HP)
    new_bias = jnp.broadcast_to(bias_table[0][:, None], (DSA_HEADS, LANES))
    bnew_m = pad_rows(new_bias[:MOBA_HEADS], MOBA_HP)
    bnew_n = pad_rows(new_bias[MOBA_HEADS:], NSA_HP)

    flat = lambda c: c.reshape(c.shape[0], c.shape[1], -1)
    flat_bf = lambda c: flat(c).astype(BF16)
    pool_mk, pool_mv = flat(cache_moba_k), flat_bf(cache_moba_v)
    pool_ck, pool_cv = flat(cache_nsa_cmp_k), flat(cache_nsa_cmp_v)
    pool_sk, pool_sv = flat_bf(cache_nsa_slc_k), flat_bf(cache_nsa_slc_v)
    pool_dk, pool_dv, pool_di = flat_bf(cache_dsa_k), flat_bf(cache_dsa_v), flat(cache_dsa_idx_k)
    win_k_all, win_v_all = flat(state_nsa_win_k), flat(state_nsa_win_v)

    mods = adaln_all(jnp.concatenate([c_prompt, c_sample], axis=0), ada_w, ada_b)
    mods = mods.reshape(DEPTH, bp + bs, 3, 3, D_MODEL)
    ffn_in = ffn_w_in.astype(BF16)
    ffn_out = ffn_w_out.astype(BF16)

    yp = x_prompt
    ys = x_sample.reshape(1, bs, D_MODEL)
    rows_ep = [[] for _ in range(8)]
    rows_es = [[] for _ in range(8)]
    rows_op = [[] for _ in range(3)]
    rows_os = [[] for _ in range(3)]
    for li in range(DEPTH):
        mp = mods[li, :bp][:, None]
        ms = mods[li, bp:][None]
        mod = lambda m, s, k: m[:, :, s, k]

        def ffn(y, m, s, w_idx):
            return ffn_half(y, mod(m, s, 0), mod(m, s, 1), mod(m, s, 2), norm_gain[li, s], ffn_in[li, w_idx],
                            ffn_out[li, w_idx])

        yp, ys = ffn(yp, mp, 0, 0), ffn(ys, ms, 0, 0)
        if li % 2 == 0:
            e = li // 2
            w_p, gain_cols = _even_weights(even_w_in[e], moba_qk_gain[e], nsa_qk_gain[e])
            w_o = even_w_out[e].astype(BF16)
            cmp_w = _compress_weights(nsa_cmp_pe[e], nsa_cmp_w1[e], nsa_cmp_w2[e])
            cmp_gain = jnp.tile(nsa_qk_gain[e, 1], NSA_KV).reshape(1, NSA_KW)
            project = lambda y, m, outs: mixer_project(y, mod(m, 1, 0), mod(m, 1, 1), norm_gain[li, 1], w_p,
                                                       gain_cols, EVEN_MODES, outs)
            pp = project(yp, mp, EVEN_OUTS_PROMPT)
            gl, mq_h, mk_h, mv_h, nq_h, sk_h, sv_h, wk_h, wv_h = pp[8:]
            o_m = moba_prompt(mq_h, mk_h, mv_h, bias_m)
            ck, cv = nsa_compress_prompt(pp[2], pp[3], *cmp_w, cmp_gain)
            o_n = nsa_prompt(nq_h, gl, ck, cv, sk_h, sv_h, wk_h, wv_h, bias_n, cbias)
            pieces_p = [o_m, o_n]
            wb = min(WINDOW, t)
            new_p = list(pp[:6]) + [pp[6][:, t - wb:], pp[7][:, t - wb:]]
            ps = [a.reshape(bs, 1, a.shape[-1]) for a in project(ys, ms, EVEN_OUTS_DECODE)]
            mk_s, mv_s, ckr_s, cvr_s, sk_s, sv_s, wk_s, wv_s, gl_s, mq_s, nq_s = ps
            o_ms = moba_decode(page_table, _block_queries(mq_s[:, 0], MOBA_HEADS, 1, MOBA_HP), mk_s, mv_s,
                               dec_bias_m, bnew_m, pool_mk, pool_mv, e)
            gates_s = jnp.pad(gl_s[:, 0, :3 * NSA_HEADS].reshape(bs, NSA_HEADS, 3),
                              ((0, 0), (0, NSA_HP - NSA_HEADS), (0, LANES - 3)))
            o_ns = nsa_decode(page_table, _block_queries(nq_s[:, 0], NSA_HEADS, NSA_GROUP, NSA_HP), gates_s,
                              sk_s, sv_s, wk_s, wv_s, win_k_all, win_v_all, cmp_w, cmp_gain,
                              dec_cbias, dec_bias_n, dec_wbias, bnew_n, pool_ck, pool_cv, pool_sk, pool_sv, e)
            pieces_s = [_decode_heads(o_ms, MOBA_HEADS), _decode_heads(o_ns, NSA_HEADS)]
            keep = min(WINDOW, PAST_LEN + 1)
            lanes_e = slice(e * NSA_KW, (e + 1) * NSA_KW)
            new_s = [mk_s, mv_s, ckr_s, cvr_s, sk_s, sv_s,
                     jnp.concatenate([win_k_all[:, :, lanes_e], wk_s], axis=1)[:, -keep:],
                     jnp.concatenate([win_v_all[:, :, lanes_e], wv_s], axis=1)[:, -keep:]]
            for acc, r in zip(rows_ep, new_p):
                acc.append(r)
            for acc, r in zip(rows_es, new_s):
                acc.append(r)
        else:
            o = li // 2
            w_p, gain_cols = _odd_weights(odd_w_in[o], dsa_qk_gain[o])
            w_o = odd_w_out[o].astype(BF16)
            project = lambda y, m, outs: mixer_project(y, mod(m, 1, 0), mod(m, 1, 1), norm_gain[li, 1], w_p,
                                                       gain_cols, ODD_MODES, outs)
            pp = project(yp, mp, ODD_OUTS_PROMPT)
            wi, q_h, k_h, v_h, qi_h = pp[3:]
            keep_mask = dsa_index_prompt(qi_h, wi, pp[2])
            pieces_p = [dsa_attn_prompt(q_h, k_h, v_h, bias_d, keep_mask)]
            ps = [a.reshape(bs, 1, a.shape[-1]) for a in project(ys, ms, ODD_OUTS_DECODE)]
            k_s, v_s, ki_s, wi_s, q_s, qi_s = ps
            idx_lanes = lambda a: jnp.pad(a, ((0, 0), (0, 0), (o * IDX_DIM, LANES - (o + 1) * IDX_DIM)))
            wi_rows = jnp.broadcast_to(wi_s[:, 0, :IDX_HEADS, None], (bs, IDX_HEADS, LANES))
            score = dsa_score_decode(page_table, idx_lanes(qi_s[:, 0].reshape(bs, IDX_HEADS, IDX_DIM)), wi_rows,
                                     idx_lanes(ki_s[:, :, :IDX_DIM]), pool_di)
            keep_s = dsa_topk_decode(score.reshape(bs, KEYS_PAD)).reshape(bs, 1, KEYS_PAD)
            o_ds = dsa_attn_decode(page_table, _block_queries(q_s[:, 0], DSA_HEADS, DSA_GROUP, DSA_HEADS), k_s, v_s,
                                   keep_s, dec_bias_d, new_bias, pool_dk, pool_dv, o)
            pieces_s = [_decode_heads(o_ds, DSA_HEADS)]
            for acc, r in zip(rows_op, (pp[0], pp[1], pp[2][..., :IDX_DIM])):
                acc.append(r)
            for acc, r in zip(rows_os, (k_s, v_s, ki_s[..., :IDX_DIM])):
                acc.append(r)
        yp = mixer_merge(yp, pieces_p, mod(mp, 1, 2), w_o)
        ys = mixer_merge(ys, pieces_s, mod(ms, 1, 2), w_o)
        yp, ys = ffn(yp, mp, 2, 1), ffn(ys, ms, 2, 1)

    def stack_layers(rows, n_heads):
        return jnp.stack([r.reshape(r.shape[0], r.shape[1], n_heads, HEAD_DIM) for r in rows], axis=2)

    even_heads = (MOBA_HEADS, MOBA_HEADS) + (NSA_KV,) * 6
    moba_k_p, moba_v_p, cmp_k_p, cmp_v_p, slc_k_p, slc_v_p, win_k_p, win_v_p = [
        stack_layers(r, n) for r, n in zip(rows_ep, even_heads)]
    moba_k_s, moba_v_s, cmp_k_s, cmp_v_s, slc_k_s, slc_v_s, win_k_s, win_v_s = [
        stack_layers(r, n) for r, n in zip(rows_es, even_heads)]
    dsa_k_p, dsa_v_p = stack_layers(rows_op[0], DSA_KV), stack_layers(rows_op[1], DSA_KV)
    dsa_k_s, dsa_v_s = stack_layers(rows_os[0], DSA_KV), stack_layers(rows_os[1], DSA_KV)
    dsa_idx_k_p, dsa_idx_k_s = jnp.stack(rows_op[2], axis=2), jnp.stack(rows_os[2], axis=2)
    return (yp, ys.reshape(bs, 1, D_MODEL), moba_k_p, moba_k_s, moba_v_p, moba_v_s, cmp_k_p, cmp_k_s, cmp_v_p,
            cmp_v_s, slc_k_p, slc_k_s, slc_v_p, slc_v_s, win_k_p, win_k_s, win_v_p, win_v_s,
            dsa_k_p, dsa_k_s, dsa_v_p, dsa_v_s, dsa_idx_k_p, dsa_idx_k_s)
```

```python
import functools
import math

import jax
import jax.numpy as jnp
from jax import lax
from jax.experimental import pallas as pl
from jax.experimental.pallas import tpu as pltpu

D_MODEL = 1024
DEPTH = 4
PAST_LEN = 2048
PAGE_SIZE = 128
N_PAGES = PAST_LEN // PAGE_SIZE
HEAD_DIM = 64
MOBA_HEADS = 4
NSA_HEADS = 12
NSA_KV = 2
NSA_GROUP = 6
DSA_HEADS = 16
DSA_KV = 4
DSA_GROUP = 4
IDX_HEADS = 8
IDX_DIM = 64
MOBA_BLOCK = 256
MOBA_TOPK = 3
CMP_LEN = 32
CMP_STRIDE = 16
CMP_HIDDEN = 128
SEL_BLOCK = 64
N_SEL = 8
WINDOW = 512
DSA_TOPK = 256
N_BUCKETS = 32
MAX_DISTANCE = 128
D_FF = 2816
MOBA_W = MOBA_HEADS * HEAD_DIM
NSA_W = NSA_HEADS * HEAD_DIM
NSA_KW = NSA_KV * HEAD_DIM
DSA_W = DSA_HEADS * HEAD_DIM
DSA_KW = DSA_KV * HEAD_DIM
RMS_EPS = 1e-6
NEG_INF = -1e30
TINY = 1e-30
SEL_FORCE = 1e4
QK_SCALE = HEAD_DIM ** -0.5

LANES = 128
VMEM_LIMIT = 56 * 1024 * 1024
BF16 = jnp.bfloat16
F32 = jnp.float32
HIGHEST = lax.Precision.HIGHEST
NT_DIMS = (((1,), (1,)), ((), ()))
M_INIT = -3e38
INT_MIN = -2 ** 31


def _cparams(*sem):
    return pltpu.CompilerParams(dimension_semantics=sem, vmem_limit_bytes=VMEM_LIMIT)


def _t5_bucket(dist):
    n = jnp.maximum(dist, 0)
    exact = N_BUCKETS // 2
    nf = jnp.maximum(n, 1).astype(F32)
    large = exact + (jnp.log(nf / exact) / math.log(MAX_DISTANCE / exact) * (N_BUCKETS - exact)).astype(jnp.int32)
    return jnp.where(n < exact, n, jnp.minimum(large, N_BUCKETS - 1))


def _rel_bias(dist, table):
    onehot = jax.nn.one_hot(_t5_bucket(dist), N_BUCKETS, dtype=F32)
    return jnp.einsum("...k,kh->h...", onehot, table, precision=HIGHEST)


def _adaln_kernel(c_ref, w_ref, b_ref, o_ref):
    c = c_ref[...]
    s = (c * jax.nn.sigmoid(c)).astype(BF16)
    o_ref[0] = jnp.dot(s, w_ref[0].astype(BF16), preferred_element_type=F32) + b_ref[0]


def adaln_all(c_all, ada_w, ada_b):
    r = c_all.shape[0]
    n_out = ada_w.shape[2]
    tn = 1024
    return pl.pallas_call(
        _adaln_kernel,
        grid=(DEPTH, n_out // tn),
        in_specs=[
            pl.BlockSpec((r, D_MODEL), lambda l, j: (0, 0)),
            pl.BlockSpec((1, D_MODEL, tn), lambda l, j: (l, 0, j)),
            pl.BlockSpec((1, 1, tn), lambda l, j: (l, 0, j)),
        ],
        out_specs=pl.BlockSpec((1, r, tn), lambda l, j: (l, 0, j)),
        out_shape=jax.ShapeDtypeStruct((DEPTH, r, n_out), F32),
        compiler_params=_cparams("parallel", "parallel"),
    )(c_all, ada_w, ada_b.reshape(DEPTH, 1, n_out))


def _modnorm(x, g, scale, shift):
    y = x * lax.rsqrt(jnp.mean(x * x, axis=-1, keepdims=True) + RMS_EPS)
    return (y * g) * (1.0 + scale) + shift


def _mod_spec(tm_rows, t_mod):
    if t_mod == 1:
        return pl.BlockSpec((1, 1, D_MODEL), lambda b, i, *_: (b, 0, 0))
    return pl.BlockSpec((1, tm_rows, D_MODEL), lambda b, i, *_: (b, i, 0))


def _seg_ones():
    return jnp.kron(jnp.eye(LANES // HEAD_DIM, dtype=F32), jnp.ones((HEAD_DIM, HEAD_DIM), F32))


def _head_rms(z, seg_ref, gain):
    ss = jnp.dot(z * z, seg_ref[...], preferred_element_type=F32, precision=HIGHEST)
    return z * lax.rsqrt(ss * (1.0 / HEAD_DIM) + RMS_EPS) * gain


def _ffn_kernel(x_ref, sh_ref, sc_ref, gt_ref, g_ref, wa_ref, wg_ref, wo_ref, o_ref, xn_ref, acc_ref):
    j = pl.program_id(2)

    @pl.when(j == 0)
    def _():
        xn_ref[...] = _modnorm(x_ref[0], g_ref[...], sc_ref[0], sh_ref[0]).astype(BF16)
        acc_ref[...] = jnp.zeros_like(acc_ref)

    xn = xn_ref[...]
    a = jnp.dot(xn, wa_ref[...], preferred_element_type=F32)
    g = jnp.dot(xn, wg_ref[...], preferred_element_type=F32)
    h = ((g * jax.nn.sigmoid(g)) * a).astype(BF16)
    acc_ref[...] += jnp.dot(h, wo_ref[...], preferred_element_type=F32)

    @pl.when(j == pl.num_programs(2) - 1)
    def _():
        o_ref[0] = x_ref[0] + (0.5 * gt_ref[0]) * acc_ref[...]


def ffn_half(x, shift, scale, gate, g, w_in, w_out):
    b, t, _ = x.shape
    tm = min(t, 512)
    fc = D_FF // 2
    nf = D_FF // fc
    mspec = _mod_spec(tm, shift.shape[1])
    return pl.pallas_call(
        _ffn_kernel,
        grid=(b, t // tm, nf),
        in_specs=[
            pl.BlockSpec((1, tm, D_MODEL), lambda bb, i, j: (bb, i, 0)),
            mspec, mspec, mspec,
            pl.BlockSpec((1, D_MODEL), lambda bb, i, j: (0, 0)),
            pl.BlockSpec((D_MODEL, fc), lambda bb, i, j: (0, j)),
            pl.BlockSpec((D_MODEL, fc), lambda bb, i, j: (0, j + nf)),
            pl.BlockSpec((fc, D_MODEL), lambda bb, i, j: (j, 0)),
        ],
        out_specs=pl.BlockSpec((1, tm, D_MODEL), lambda bb, i, j: (bb, i, 0)),
        out_shape=jax.ShapeDtypeStruct(x.shape, F32),
        scratch_shapes=[pltpu.VMEM((tm, D_MODEL), BF16), pltpu.VMEM((tm, D_MODEL), F32)],
        compiler_params=_cparams("parallel", "parallel", "arbitrary"),
    )(x, shift, scale, gate, g.reshape(1, D_MODEL), w_in, w_in, w_out)


def _proj_kernel(modes, outs, x_ref, sh_ref, sc_ref, g_ref, w_ref, gain_ref, seg_ref, *o_refs):
    h = _modnorm(x_ref[0], g_ref[...], sc_ref[0], sh_ref[0])
    z = jnp.dot(h.astype(BF16), w_ref[...], preferred_element_type=F32)
    chunks = []
    for c, mode in enumerate(modes):
        zc = z[:, c * LANES:(c + 1) * LANES]
        if mode == "norm":
            zc = _head_rms(zc, seg_ref, gain_ref[:, c * LANES:(c + 1) * LANES])
        elif mode == "sigmoid":
            zc = jax.nn.sigmoid(zc)
        chunks.append(zc)
    for o_ref, out in zip(o_refs, outs):
        if out[0] == "cols":
            _, start, width = out
            for c in range(width // LANES):
                o_ref[0, :, c * LANES:(c + 1) * LANES] = chunks[start // LANES + c]
        else:
            _, start, n_heads, dtype, scale = out
            for hh in range(n_heads):
                lo = start + hh * HEAD_DIM
                piece = chunks[lo // LANES][:, lo % LANES:lo % LANES + HEAD_DIM]
                o_ref[0, hh] = (piece * scale).astype(dtype)


def mixer_project(x, shift, scale, g, w, gain_cols, modes, outs):
    b, t, _ = x.shape
    tm = min(t, 512)
    wp = w.shape[1]
    mspec = _mod_spec(tm, shift.shape[1])
    out_shapes, out_specs = [], []
    for out in outs:
        if out[0] == "cols":
            out_shapes.append(jax.ShapeDtypeStruct((b, t, out[2]), F32))
            out_specs.append(pl.BlockSpec((1, tm, out[2]), lambda bb, i: (bb, i, 0)))
        else:
            out_shapes.append(jax.ShapeDtypeStruct((b, out[2], t, HEAD_DIM), out[3]))
            out_specs.append(pl.BlockSpec((1, out[2], tm, HEAD_DIM), lambda bb, i: (bb, 0, i, 0)))
    return pl.pallas_call(
        functools.partial(_proj_kernel, modes, outs),
        grid=(b, t // tm),
        in_specs=[
            pl.BlockSpec((1, tm, D_MODEL), lambda bb, i: (bb, i, 0)),
            mspec, mspec,
            pl.BlockSpec((1, D_MODEL), lambda bb, i: (0, 0)),
            pl.BlockSpec((D_MODEL, wp), lambda bb, i: (0, 0)),
            pl.BlockSpec((1, wp), lambda bb, i: (0, 0)),
            pl.BlockSpec((LANES, LANES), lambda bb, i: (0, 0)),
        ],
        out_specs=out_specs,
        out_shape=out_shapes,
        compiler_params=_cparams("parallel", "parallel"),
    )(x, shift, scale, g.reshape(1, D_MODEL), w, gain_cols, _seg_ones())


def _merge_kernel(n_pieces, y_ref, gt_ref, w_ref, *refs):
    out_ref = refs[n_pieces]
    heads = [refs[k][0, hh] for k in range(n_pieces) for hh in range(refs[k].shape[1])]
    o = jnp.concatenate(heads, axis=1).astype(BF16)
    out_ref[0] = y_ref[0] + gt_ref[0] * jnp.dot(o, w_ref[...], preferred_element_type=F32)


def mixer_merge(y, pieces, gate, w_out):
    b, t, _ = y.shape
    tm = min(t, 512)
    piece_specs = [pl.BlockSpec((1, p.shape[1], tm, HEAD_DIM), lambda bb, i: (bb, 0, i, 0)) for p in pieces]
    return pl.pallas_call(
        functools.partial(_merge_kernel, len(pieces)),
        grid=(b, t // tm),
        in_specs=[
            pl.BlockSpec((1, tm, D_MODEL), lambda bb, i: (bb, i, 0)),
            _mod_spec(tm, gate.shape[1]),
            pl.BlockSpec((D_MODEL, D_MODEL), lambda bb, i: (0, 0)),
        ] + piece_specs,
        out_specs=pl.BlockSpec((1, tm, D_MODEL), lambda bb, i: (bb, i, 0)),
        out_shape=jax.ShapeDtypeStruct(y.shape, F32),
        compiler_params=_cparams("parallel", "parallel"),
    )(y, gate, w_out, *pieces)


TQ = 128
CHUNK_TILES = 4
CK = CHUNK_TILES * TQ


def _bias_tiles(table, t):
    d = jnp.arange(t)[:, None] - jnp.arange(t)[None, :]
    diag = jnp.where(d >= 0, _rel_bias(d, table), NEG_INF)
    adj = _rel_bias(t + d, table)
    far = _rel_bias(2 * t + d, table)
    masked = jnp.full_like(far, NEG_INF)
    edge = jnp.where(d < 0, far, NEG_INF)
    return jnp.stack([diag, adj, far, masked, edge], axis=1)


def _tile_kind(d):
    return jnp.where(d < 0, 3, jnp.minimum(d, 2))


def _loop(n, body, init, static):
    if static:
        for c in range(n):
            init = body(c, init)
        return init
    return lax.fori_loop(0, n, body, init)


def _attend(qb, n_chunks, k_of, v_of, add_of, s_ref, static=False):
    rows = qb.shape[0]

    def cols(c, w):
        return pl.ds(c * w if static else pl.multiple_of(c * w, w), w)

    def score(c, mx):
        s = lax.dot_general(qb, k_of(c), NT_DIMS, preferred_element_type=F32) + add_of(c)
        w = s.shape[1]
        s_ref[:, cols(c, w)] = s
        for part in range(w // LANES):
            mx = jnp.maximum(mx, s[:, part * LANES:(part + 1) * LANES])
        return mx

    mx = _loop(n_chunks, score, jnp.full((rows, LANES), M_INIT, F32), static)
    m = jnp.max(mx, axis=1, keepdims=True)

    def accumulate(c, acc):
        v = v_of(c)
        w = v.shape[0]
        p = jnp.exp(s_ref[:, cols(c, w)] - m).astype(BF16)
        return acc + jnp.dot(p, jnp.concatenate([v, jnp.ones_like(v)], axis=1), preferred_element_type=F32)

    acc = _loop(n_chunks, accumulate, jnp.zeros((rows, 2 * HEAD_DIM), F32), static)
    return acc[:, :HEAD_DIM] / jnp.maximum(acc[:, HEAD_DIM:HEAD_DIM + 1], TINY)


def _top_k_rows(score, allowed, k, n_cand):
    lane = lax.broadcasted_iota(jnp.int32, score.shape, 1)
    s = jnp.where(allowed, score, -jnp.inf)
    ahead = jnp.zeros(score.shape, F32)
    for j in range(n_cand):
        col = s[:, j:j + 1]
        ahead = ahead + jnp.where(col > s, 1.0, jnp.where(col == s, jnp.where(lane > j, 1.0, 0.0), 0.0))
    return jnp.logical_and(allowed, ahead < float(k))


def _count(mask):
    return jnp.sum(jnp.where(mask, 1.0, 0.0), axis=1, keepdims=True)


def _top_k_mask(score, valid, pos, k):
    bits = pltpu.bitcast(score, jnp.int32)
    key = jnp.where(bits < 0, bits ^ jnp.int32(0x7FFFFFFF), bits)
    key = jnp.where(valid, key, jnp.int32(INT_MIN))
    kf = float(k)

    thr = jnp.full((score.shape[0], 1), INT_MIN, jnp.int32)
    cand = thr ^ jnp.int32(INT_MIN)
    thr = jnp.where(_count(key >= cand) >= kf, cand, thr)

    def step(b, thr):
        cand = thr | lax.shift_left(jnp.int32(1), 30 - b)
        return jnp.where(_count(key >= cand) >= kf, cand, thr)

    thr = lax.fori_loop(0, 31, step, thr)
    above = key > thr
    tie = jnp.logical_and(key == thr, valid)
    need = kf - _count(above)

    n_bits = max(int(score.shape[1] - 1).bit_length(), 1)
    all_pos = jnp.full((score.shape[0], 1), 2 ** n_bits - 1, jnp.int32)

    def tie_search():
        def tie_step(b, cut):
            cand = cut | lax.shift_left(jnp.int32(1), n_bits - 1 - b)
            return jnp.where(_count(jnp.logical_and(tie, pos < cand)) < need, cand, cut)

        return lax.fori_loop(0, n_bits, tie_step, jnp.zeros_like(all_pos))

    surplus = jnp.max(jnp.where(_count(tie) > need, 1.0, 0.0))
    cut = lax.cond(surplus > 0.0, tie_search, lambda: all_pos)
    keep = jnp.logical_or(above, jnp.logical_and(tie, pos <= cut))
    return jnp.logical_and(keep, valid)


def _moba_kernel(q_ref, k_ref, v_ref, bias_ref, o_ref, km_ref, s_ref):
    i = pl.program_id(2)
    nb = km_ref.shape[0]
    t = MOBA_BLOCK

    @pl.when(i == 0)
    def _():
        km_ref[...] = jnp.mean(k_ref[0, 0].reshape(nb, t, HEAD_DIM), axis=1)

    q = q_ref[0, 0]
    gate = lax.dot_general(q.astype(BF16), km_ref[...].astype(BF16), NT_DIMS, preferred_element_type=F32)
    blk = lax.broadcasted_iota(jnp.int32, gate.shape, 1)
    chosen = jnp.logical_or(_top_k_rows(gate, blk < i, MOBA_TOPK, nb), blk == i)
    blk_add = jnp.where(chosen, 0.0, NEG_INF)

    def rows_of(j):
        return pl.ds(pl.multiple_of(j * t, t), t)

    def add_of(j):
        col = jnp.sum(jnp.where(blk == j, blk_add, 0.0), axis=1, keepdims=True)
        return bias_ref[0, jnp.minimum(i - j, 2)] + col

    o_ref[0, 0] = _attend((q * QK_SCALE).astype(BF16), i + 1, lambda j: k_ref[0, 0, rows_of(j), :].astype(BF16),
                          lambda j: v_ref[0, 0, rows_of(j), :], add_of, s_ref)


def moba_prompt(q, k, v, bias):
    b, h, t, _ = q.shape
    tb = MOBA_BLOCK
    nb = t // tb
    full = pl.BlockSpec((1, 1, t, HEAD_DIM), lambda bb, hh, i: (bb, hh, 0, 0))
    return pl.pallas_call(
        _moba_kernel,
        grid=(b, h, nb),
        in_specs=[
            pl.BlockSpec((1, 1, tb, HEAD_DIM), lambda bb, hh, i: (bb, hh, i, 0)),
            full, full,
            pl.BlockSpec((1, 3, tb, tb), lambda bb, hh, i: (hh, 0, 0, 0)),
        ],
        out_specs=pl.BlockSpec((1, 1, tb, HEAD_DIM), lambda bb, hh, i: (bb, hh, i, 0)),
        out_shape=jax.ShapeDtypeStruct(q.shape, F32),
        scratch_shapes=[pltpu.VMEM((nb, HEAD_DIM), F32), pltpu.VMEM((tb, t), F32)],
        compiler_params=_cparams("parallel", "parallel", "arbitrary"),
    )(q, k, v, bias)


N_CHUNK = PAST_LEN // CMP_STRIDE
CHUNK_W = CMP_STRIDE * NSA_KW
N_CMP = (PAST_LEN - CMP_LEN) // CMP_STRIDE + 1
HALF_W = NSA_KV * CMP_HIDDEN


def _compress_weights(pe, w1, w2):
    eye = jnp.eye(NSA_KV, dtype=F32)
    pe_x = jnp.broadcast_to(pe.reshape(2, 2, CMP_STRIDE, 1, HEAD_DIM),
                            (2, 2, CMP_STRIDE, NSA_KV, HEAD_DIM)).reshape(2, 2, CHUNK_W)
    w1_h = w1.reshape(2, 2, CMP_STRIDE, HEAD_DIM, CMP_HIDDEN)
    w1_x = jnp.einsum("thrdj,gq->trgdhqj", w1_h, eye).reshape(2, CHUNK_W, 2 * HALF_W)
    w2_x = jnp.einsum("tjd,gq->tgjqd", w2, eye).reshape(2, HALF_W, NSA_KW)
    return pe_x, w1_x.astype(BF16), w2_x.astype(BF16)


def _compress_tail(ha, hb, w2):
    hid = jax.nn.gelu(ha + pltpu.roll(hb, N_CHUNK - 1, 0))
    return jnp.dot(hid.astype(BF16), w2, preferred_element_type=F32)


def _compress_one(r, pe_ref, w1_ref, w2_ref, t):
    ha = jnp.dot((r + pe_ref[t, 0:1]).astype(BF16), w1_ref[t, :, :HALF_W], preferred_element_type=F32)
    hb = jnp.dot((r + pe_ref[t, 1:2]).astype(BF16), w1_ref[t, :, HALF_W:], preferred_element_type=F32)
    return _compress_tail(ha, hb, w2_ref[t])


def _compress_kernel(rk_ref, rv_ref, pe_ref, w1_ref, w2_ref, gain_ref, seg_ref, ck_ref, cv_ref):
    ck_ref[0] = _head_rms(_compress_one(rk_ref[0], pe_ref, w1_ref, w2_ref, 0), seg_ref, gain_ref[...])
    cv_ref[0] = _compress_one(rv_ref[0], pe_ref, w1_ref, w2_ref, 1)


def nsa_compress_prompt(ck_raw, cv_raw, pe_x, w1_x, w2_x, gain_row):
    b = ck_raw.shape[0]
    rows = pl.BlockSpec((1, N_CHUNK, CHUNK_W), lambda bb: (bb, 0, 0))
    out = pl.BlockSpec((1, N_CHUNK, NSA_KW), lambda bb: (bb, 0, 0))
    const = lambda *shape: pl.BlockSpec(shape, lambda bb: (0,) * len(shape))
    return pl.pallas_call(
        _compress_kernel,
        grid=(b,),
        in_specs=[rows, rows, const(2, 2, CHUNK_W), const(2, CHUNK_W, 2 * HALF_W),
                  const(2, HALF_W, NSA_KW), const(1, NSA_KW), const(LANES, LANES)],
        out_specs=[out, out],
        out_shape=[jax.ShapeDtypeStruct((b, N_CHUNK, NSA_KW), F32)] * 2,
        compiler_params=_cparams("parallel"),
    )(ck_raw.reshape(b, N_CHUNK, CHUNK_W), cv_raw.reshape(b, N_CHUNK, CHUNK_W), pe_x, w1_x, w2_x, gain_row,
      _seg_ones())


def _overlap_matrix():
    c = jnp.arange(LANES)[:, None]
    n = jnp.arange(LANES)[None, :]
    return jnp.logical_and(c * CMP_STRIDE < n * SEL_BLOCK + SEL_BLOCK,
                           c * CMP_STRIDE + CMP_LEN > n * SEL_BLOCK).astype(F32)


def _block_expand(n_keys):
    return (jnp.arange(n_keys)[None, :] // SEL_BLOCK == jnp.arange(LANES)[:, None]).astype(BF16)


def _nsa_kernel(q_ref, gt_ref, ck_ref, cv_ref, sk_ref, sv_ref, wk_ref, wv_ref, bias_ref, cb_ref, ov_ref, ex_ref,
                o_ref, madd_ref, s_ref):
    g = pl.program_id(1)
    i = pl.program_id(2)
    hg = NSA_GROUP
    rows = hg * TQ
    nq = madd_ref.shape[0]
    row = lax.broadcasted_iota(jnp.int32, (TQ, LANES), 0)
    lane = lax.broadcasted_iota(jnp.int32, (TQ, LANES), 1)
    qpos = i * TQ + row
    qf = q_ref[0].reshape(rows, HEAD_DIM)

    def group_half(x):
        return jnp.where(g == 0, x[:, :HEAD_DIM], x[:, HEAD_DIM:])

    cmask = jnp.logical_and(qpos - (lane * CMP_STRIDE + CMP_LEN - 1) >= 0, lane < N_CMP)
    cmask_all = jnp.concatenate([cmask] * hg, axis=0)
    lc = lax.dot_general(qf.astype(BF16), group_half(ck_ref[0]).astype(BF16), NT_DIMS, preferred_element_type=F32)
    lc = jnp.where(cmask_all, lc * QK_SCALE + cb_ref[...].reshape(rows, LANES), NEG_INF)
    e = jnp.where(cmask_all, jnp.exp(lc - jnp.max(lc, axis=1, keepdims=True)), 0.0)
    pc = e / jnp.maximum(jnp.sum(e, axis=1, keepdims=True), TINY)
    o_cmp = jnp.dot(pc.astype(BF16), group_half(cv_ref[0]).astype(BF16), preferred_element_type=F32)
    psum = jnp.sum(pc.reshape(hg, TQ, LANES), axis=0)

    imp = jnp.dot(psum.astype(BF16), ov_ref[...], preferred_element_type=F32)
    cur = qpos // SEL_BLOCK
    forced = jnp.logical_or(lane == 0, jnp.logical_or(lane == cur, lane == cur - 1))
    picked = _top_k_rows(jnp.where(forced, SEL_FORCE, imp), lane <= cur, N_SEL, nq * TQ // SEL_BLOCK)
    pick_b = jnp.where(picked, 1.0, 0.0).astype(BF16)

    key_add = jnp.where(jnp.dot(pick_b, ex_ref[...], preferred_element_type=F32) > 0.5, 0.0, NEG_INF)
    for j in range(nq):
        madd_ref[j] = key_add[:, j * TQ:(j + 1) * TQ]

    qb = (qf * QK_SCALE).astype(BF16)

    def bias_rows(kind):
        return bias_ref[:, kind].reshape(rows, TQ)

    def key_tile(ref, j):
        return ref[0, 0, pl.ds(pl.multiple_of(j * TQ, TQ), TQ), :]

    def keys_of(c):
        return pl.ds(pl.multiple_of(c * CK, CK), CK)

    def slc_add(c):
        tiles = [CHUNK_TILES * c + k for k in range(CHUNK_TILES)]
        mask = jnp.concatenate([madd_ref[j] for j in tiles], axis=1)
        add = jnp.concatenate([bias_rows(_tile_kind(i - j)) for j in tiles], axis=1)
        return add + jnp.concatenate([mask] * hg, axis=0)

    o_slc = _attend(qb, i // CHUNK_TILES + 1, lambda c: sk_ref[0, 0, keys_of(c), :],
                    lambda c: sv_ref[0, 0, keys_of(c), :], slc_add, s_ref)

    n_win = WINDOW // TQ
    win_chunk = (n_win + 2) // 2

    def win_tiles(w):
        first = i - 2 * win_chunk + 1 + w * win_chunk
        return [first + k for k in range(win_chunk)]

    def win_keys(ref):
        return lambda w: jnp.concatenate([key_tile(ref, jnp.maximum(j, 0)) for j in win_tiles(w)], axis=0)

    def win_add(w):
        kinds = []
        for j in win_tiles(w):
            d = i - j
            kind = jnp.where(d == n_win, 4, jnp.where(d > n_win, 3, jnp.minimum(d, 2)))
            kinds.append(jnp.where(j < 0, 3, kind))
        return jnp.concatenate([bias_rows(kind) for kind in kinds], axis=1)

    o_win = _attend(qb, 2, win_keys(wk_ref), win_keys(wv_ref), win_add, s_ref, static=True)

    gt = gt_ref[0]

    def gate_col(branch):
        cols = []
        for h in range(hg):
            c0 = 3 * h + branch
            cols.append(jnp.where(g == 0, gt[:, c0:c0 + 1], gt[:, 3 * hg + c0:3 * hg + c0 + 1]))
        return jnp.concatenate(cols, axis=0)

    o = gate_col(0) * o_cmp + gate_col(1) * o_slc + gate_col(2) * o_win
    o_ref[0] = o.reshape(hg, TQ, HEAD_DIM)


def nsa_prompt(q, gates, ck, cv, sk, sv, wk, wv, bias, cbias):
    b, _, t, _ = q.shape
    hg = NSA_GROUP
    nq = t // TQ
    kv = pl.BlockSpec((1, 1, t, HEAD_DIM), lambda bb, gg, i: (bb, gg, 0, 0))
    cmp_spec = pl.BlockSpec((1, LANES, NSA_KW), lambda bb, gg, i: (bb, 0, 0))
    return pl.pallas_call(
        _nsa_kernel,
        grid=(b, NSA_KV, nq),
        in_specs=[
            pl.BlockSpec((1, hg, TQ, HEAD_DIM), lambda bb, gg, i: (bb, gg, i, 0)),
            pl.BlockSpec((1, TQ, LANES), lambda bb, gg, i: (bb, i, 0)),
            cmp_spec, cmp_spec, kv, kv, kv, kv,
            pl.BlockSpec((hg, 5, TQ, TQ), lambda bb, gg, i: (gg, 0, 0, 0)),
            pl.BlockSpec((hg, TQ, LANES), lambda bb, gg, i: (gg, i, 0)),
            pl.BlockSpec((LANES, LANES), lambda bb, gg, i: (0, 0)),
            pl.BlockSpec((LANES, t), lambda bb, gg, i: (0, 0)),
        ],
        out_specs=pl.BlockSpec((1, hg, TQ, HEAD_DIM), lambda bb, gg, i: (bb, gg, i, 0)),
        out_shape=jax.ShapeDtypeStruct(q.shape, F32),
        scratch_shapes=[pltpu.VMEM((nq, TQ, TQ), F32), pltpu.VMEM((hg * TQ, t), F32)],
        compiler_params=_cparams("parallel", "parallel", "arbitrary"),
    )(q, gates, ck, cv, sk, sv, wk, wv, bias, cbias, _overlap_matrix().astype(BF16), _block_expand(t))


def _dsa_index_kernel(qi_ref, wi_ref, ki_ref, o_ref):
    i = pl.program_id(1)
    t = ki_ref.shape[1]
    nq = t // TQ
    wi = wi_ref[0]

    def select(width):
        ki = ki_ref[0, :width, :IDX_DIM].astype(BF16)
        score = jnp.zeros((TQ, width), F32)
        for h in range(IDX_HEADS):
            s = lax.dot_general(qi_ref[0, h].astype(BF16), ki, NT_DIMS, preferred_element_type=F32)
            score = score + wi[:, h:h + 1] * jnp.maximum(s * IDX_DIM ** -0.5, 0.0)
        score = score * IDX_HEADS ** -0.5
        kpos = lax.broadcasted_iota(jnp.int32, (TQ, width), 1)
        qpos = i * TQ + lax.broadcasted_iota(jnp.int32, (TQ, width), 0)
        keep = _top_k_mask(score, kpos <= qpos, kpos, min(DSA_TOPK, t // 4))
        add = jnp.where(keep, 0.0, NEG_INF).astype(BF16)
        for j in range(nq):
            lo = j * TQ
            o_ref[0, 0, j] = add[:, lo:lo + TQ] if lo < width else jnp.full((TQ, TQ), NEG_INF, BF16)

    widths = [w for w in (t // 4, t // 2, t) if w % TQ == 0 and w >= TQ]
    lo_tile = 0
    for w in widths:
        hi_tile = w // TQ
        pl.when(jnp.logical_and(i >= lo_tile, i < hi_tile))(functools.partial(select, w))
        lo_tile = hi_tile


def dsa_index_prompt(qi, wi, ki):
    b, ih, t, _ = qi.shape
    nq = t // TQ
    return pl.pallas_call(
        _dsa_index_kernel,
        grid=(b, nq),
        in_specs=[
            pl.BlockSpec((1, ih, TQ, IDX_DIM), lambda bb, i: (bb, 0, i, 0)),
            pl.BlockSpec((1, TQ, LANES), lambda bb, i: (bb, i, 0)),
            pl.BlockSpec((1, t, LANES), lambda bb, i: (bb, 0, 0)),
        ],
        out_specs=pl.BlockSpec((1, 1, nq, TQ, TQ), lambda bb, i: (bb, i, 0, 0, 0)),
        out_shape=jax.ShapeDtypeStruct((b, nq, nq, TQ, TQ), BF16),
        compiler_params=_cparams("parallel", "parallel"),
    )(qi, wi, ki)


def _dsa_attn_kernel(q_ref, k_ref, v_ref, bias_ref, m_ref, o_ref, s_ref):
    i = pl.program_id(2)
    hg = DSA_GROUP
    rows = hg * TQ

    def keys_of(c):
        return pl.ds(pl.multiple_of(c * CK, CK), CK)

    def add_of(c):
        tiles = [CHUNK_TILES * c + k for k in range(CHUNK_TILES)]
        mask = jnp.concatenate([m_ref[0, 0, j] for j in tiles], axis=1).astype(F32)
        add = jnp.concatenate([bias_ref[:, _tile_kind(i - j)].reshape(rows, TQ) for j in tiles], axis=1)
        return add + jnp.concatenate([mask] * hg, axis=0)

    o = _attend(q_ref[0].reshape(rows, HEAD_DIM), i // CHUNK_TILES + 1, lambda c: k_ref[0, 0, keys_of(c), :],
                lambda c: v_ref[0, 0, keys_of(c), :], add_of, s_ref)
    o_ref[0] = o.reshape(hg, TQ, HEAD_DIM)


def dsa_attn_prompt(q, k, v, bias, mask):
    b, _, t, _ = q.shape
    hg = DSA_GROUP
    nq = t // TQ
    kv = pl.BlockSpec((1, 1, t, HEAD_DIM), lambda bb, gg, i: (bb, gg, 0, 0))
    return pl.pallas_call(
        _dsa_attn_kernel,
        grid=(b, DSA_KV, nq),
        in_specs=[
            pl.BlockSpec((1, hg, TQ, HEAD_DIM), lambda bb, gg, i: (bb, gg, i, 0)),
            kv, kv,
            pl.BlockSpec((hg, 5, TQ, TQ), lambda bb, gg, i: (gg, 0, 0, 0)),
            pl.BlockSpec((1, 1, nq, TQ, TQ), lambda bb, gg, i: (bb, i, 0, 0, 0)),
        ],
        out_specs=pl.BlockSpec((1, hg, TQ, HEAD_DIM), lambda bb, gg, i: (bb, gg, i, 0)),
        out_shape=jax.ShapeDtypeStruct(q.shape, F32),
        scratch_shapes=[pltpu.VMEM((hg * TQ, t), F32)],
        compiler_params=_cparams("parallel", "parallel", "arbitrary"),
    )(q, k, v, bias, mask)


REQ = 4


def _row_spec(*shape):
    return pl.BlockSpec((REQ,) + shape, lambda b, pt: (b,) + (0,) * len(shape))


def _const_spec(*shape):
    return pl.BlockSpec(shape, lambda b, pt: (0,) * len(shape))


def _decode_call(kernel, n_req, page_table, in_specs, args, out_shapes, out_specs):
    return pl.pallas_call(
        kernel,
        grid_spec=pltpu.PrefetchScalarGridSpec(num_scalar_prefetch=1, grid=(n_req // REQ,), in_specs=in_specs,
                                               out_specs=out_specs),
        out_shape=out_shapes,
        compiler_params=_cparams("parallel"),
    )(page_table, *args)


def _page_specs(lane_block, width):
    return [pl.BlockSpec((1, PAGE_SIZE, width),
                         functools.partial(lambda r, p, b, pt: (pt[b * REQ + r, p], 0, lane_block), r, p))
            for r in range(REQ) for p in range(N_PAGES)]


def _request_pages(refs, n_tensors, r):
    per = REQ * N_PAGES
    return [refs[t * per + r * N_PAGES:t * per + (r + 1) * N_PAGES] for t in range(n_tensors)]


def _paged_attention(qblk, k_pages, v_pages, bias_ref, mask_of_page, k_new, v_new, new_add):
    qb = (qblk * QK_SCALE).astype(BF16)
    s_new = jnp.sum(qblk * k_new, axis=1, keepdims=True) * QK_SCALE + new_add
    s_pages = []
    for p in range(N_PAGES):
        s = lax.dot_general(qb, k_pages[p][0].astype(BF16), NT_DIMS, preferred_element_type=F32)
        s_pages.append(s + bias_ref[:, p * PAGE_SIZE:(p + 1) * PAGE_SIZE] + mask_of_page(p))
    m = s_new
    for s in s_pages:
        m = jnp.maximum(m, jnp.max(s, axis=1, keepdims=True))
    e_new = jnp.exp(s_new - m)
    l = e_new
    acc = e_new * v_new
    for p in range(N_PAGES):
        e = jnp.exp(s_pages[p] - m)
        l = l + jnp.sum(e, axis=1, keepdims=True)
        acc = acc + jnp.dot(e.astype(BF16), v_pages[p][0].astype(BF16), preferred_element_type=F32)
    return acc / jnp.maximum(l, TINY)


def _group_lanes(o_full, heads_per_group):
    hp, w = o_full.shape
    grp = lax.broadcasted_iota(jnp.int32, (hp, HEAD_DIM), 0) // heads_per_group
    out = o_full[:, :HEAD_DIM]
    for gg in range(1, w // HEAD_DIM):
        out = jnp.where(grp == gg, o_full[:, gg * HEAD_DIM:(gg + 1) * HEAD_DIM], out)
    return out


MOBA_HP = 8


def _moba_decode_kernel(pt_ref, q_ref, kn_ref, vn_ref, bias_ref, bnew_ref, *refs):
    o_ref = refs[2 * REQ * N_PAGES]
    for r in range(REQ):
        k_pages, v_pages = _request_pages(refs, 2, r)
        o_ref[r] = _moba_decode_one(q_ref[r], kn_ref[r], vn_ref[r], bias_ref, bnew_ref, k_pages, v_pages)


def _moba_decode_one(qblk, k_new, v_new, bias_ref, bnew_ref, k_pages, v_pages):
    pages_per_block = MOBA_BLOCK // PAGE_SIZE
    n_blk = N_PAGES // pages_per_block
    means = []
    for blk in range(n_blk):
        tot = jnp.sum(k_pages[blk * pages_per_block][0], axis=0, keepdims=True)
        for p in range(blk * pages_per_block + 1, (blk + 1) * pages_per_block):
            tot = tot + jnp.sum(k_pages[p][0], axis=0, keepdims=True)
        means.append(tot * (1.0 / MOBA_BLOCK))
    k_mean = jnp.concatenate(means, axis=0)
    gate = lax.dot_general(qblk, k_mean, NT_DIMS, preferred_element_type=F32, precision=HIGHEST)
    chosen = _top_k_rows(gate, jnp.ones(gate.shape, jnp.bool_), MOBA_TOPK, n_blk)
    blk_add = jnp.where(chosen, 0.0, NEG_INF)

    def mask_of_page(p):
        b0 = p // pages_per_block
        return blk_add[:, b0:b0 + 1]

    o_full = _paged_attention(qblk, k_pages, v_pages, bias_ref, mask_of_page, k_new, v_new, bnew_ref[:, 0:1])
    return _group_lanes(o_full, 1)


def moba_decode(page_table, qblk, k_new, v_new, bias, bnew, cache_k, cache_v, layer):
    n_req = qblk.shape[0]
    in_specs = ([_row_spec(MOBA_HP, MOBA_W), _row_spec(1, MOBA_W), _row_spec(1, MOBA_W),
                 _const_spec(MOBA_HP, PAST_LEN), _const_spec(MOBA_HP, LANES)]
                + _page_specs(layer, MOBA_W) * 2)
    args = [qblk, k_new, v_new, bias, bnew] + [cache_k] * (REQ * N_PAGES) + [cache_v] * (REQ * N_PAGES)
    return _decode_call(_moba_decode_kernel, n_req, page_table, in_specs, args,
                        jax.ShapeDtypeStruct((n_req, MOBA_HP, HEAD_DIM), F32), _row_spec(MOBA_HP, HEAD_DIM))


NSA_HP = 16
GRP_ROWS = 8


def _chunk_rows(pages):
    per_page = PAGE_SIZE // CMP_STRIDE
    return jnp.concatenate(
        [jnp.concatenate([pg[0, pl.ds(r, per_page, stride=CMP_STRIDE), :] for pg in pages], axis=0)
         for r in range(CMP_STRIDE)], axis=1)


def _nsa_decode_kernel(pt_ref, q_ref, gt_ref, skn_ref, svn_ref, wkn_ref, wvn_ref, wkb_ref, wvb_ref,
                       pe_ref, w1_ref, w2_ref, gain_ref, seg_ref, cb_ref, sb_ref, wb_ref, bnew_ref,
                       ov_ref, ex_ref, g2_ref, p16_ref, *refs):
    o_ref = refs[4 * REQ * N_PAGES]
    consts = (pe_ref, w1_ref, w2_ref, gain_ref, seg_ref, cb_ref, sb_ref, wb_ref, bnew_ref, ov_ref, ex_ref, g2_ref,
              p16_ref)
    for r in range(REQ):
        o_ref[r] = _nsa_decode_one(q_ref[r], gt_ref[r], skn_ref[r], svn_ref[r], wkn_ref[r], wvn_ref[r], wkb_ref[r],
                                   wvb_ref[r], consts, _request_pages(refs, 4, r))


def _nsa_decode_one(qblk, gt, sk_new, sv_new, wk_new, wv_new, wk_buf, wv_buf, consts, pages):
    pe_ref, w1_ref, w2_ref, gain_ref, seg_ref, cb_ref, sb_ref, wb_ref, bnew_ref, ov_ref, ex_ref, g2_ref, p16_ref = consts
    ck_pages, cv_pages, sk_pages, sv_pages = pages
    new_add = bnew_ref[:, 0:1]
    lane = lax.broadcasted_iota(jnp.int32, (GRP_ROWS, LANES), 1)

    ck = _head_rms(_compress_one(_chunk_rows(ck_pages), pe_ref, w1_ref, w2_ref, 0), seg_ref, gain_ref[...])
    cv = _compress_one(_chunk_rows(cv_pages), pe_ref, w1_ref, w2_ref, 1)
    cvalid = lax.broadcasted_iota(jnp.int32, (NSA_HP, LANES), 1) < N_CMP
    lc = lax.dot_general(qblk, ck, NT_DIMS, preferred_element_type=F32, precision=HIGHEST)
    lc = jnp.where(cvalid, lc * QK_SCALE + cb_ref[...], NEG_INF)
    e = jnp.where(cvalid, jnp.exp(lc - jnp.max(lc, axis=1, keepdims=True)), 0.0)
    pc = e / jnp.maximum(jnp.sum(e, axis=1, keepdims=True), TINY)
    o_cmp = jnp.dot(pc.astype(BF16), cv.astype(BF16), preferred_element_type=F32)

    psum = jnp.dot(g2_ref[...], pc, preferred_element_type=F32, precision=HIGHEST)
    imp = jnp.dot(psum, ov_ref[...], preferred_element_type=F32, precision=HIGHEST)
    cur = PAST_LEN // SEL_BLOCK
    forced = jnp.logical_or(lane == 0, jnp.logical_or(lane == cur, lane == cur - 1))
    picked = _top_k_rows(jnp.where(forced, SEL_FORCE, imp), lane <= cur, N_SEL, cur + 1)
    pick_h =jnp.dot(p16_ref[...], jnp.where(picked, 1.0, 0.0).astype(BF16), preferred_element_type=F32)
    key_hit = jnp.dot(pick_h.astype(BF16), ex_ref[...], preferred_element_type=F32)
    key_add = jnp.where(key_hit > 0.5, 0.0, NEG_INF)

    o_slc = _paged_attention(qblk, sk_pages, sv_pages, sb_ref, lambda p: key_add[:, p * PAGE_SIZE:(p + 1) * PAGE_SIZE],
                             sk_new, sv_new, new_add)

    qb = (qblk * QK_SCALE).astype(BF16)
    sw = lax.dot_general(qb, wk_buf.astype(BF16), NT_DIMS, preferred_element_type=F32) + wb_ref[...]
    s_new = jnp.sum(qblk * wk_new, axis=1, keepdims=True) * QK_SCALE + new_add
    m = jnp.maximum(s_new, jnp.max(sw, axis=1, keepdims=True))
    ew = jnp.exp(sw - m)
    e_new = jnp.exp(s_new - m)
    l = e_new + jnp.sum(ew, axis=1, keepdims=True)
    o_win = e_new * wv_new + jnp.dot(ew.astype(BF16), wv_buf.astype(BF16), preferred_element_type=F32)
    o_win = o_win / jnp.maximum(l, TINY)

    o_full = gt[:, 0:1] * o_cmp + gt[:, 1:2] * o_slc + gt[:, 2:3] * o_win
    return _group_lanes(o_full, NSA_GROUP)


def nsa_decode(page_table, qblk, gates, sk_new, sv_new, wk_new, wv_new, win_k, win_v, cmp_w, gain_row,
               cbias, sbias, wbias, bnew, cache_ck, cache_cv, cache_sk, cache_sv, layer):
    n_req = qblk.shape[0]
    pe_x, w1_x, w2_x = cmp_w
    hh = jnp.arange(NSA_HP)
    g2 = jnp.logical_and(hh[None, :] // NSA_GROUP == jnp.arange(GRP_ROWS)[:, None], hh[None, :] < NSA_HEADS)
    p16 = (hh[:, None] // NSA_GROUP == jnp.arange(GRP_ROWS)[None, :]).astype(BF16)
    win_spec = pl.BlockSpec((REQ, WINDOW, NSA_KW), lambda b, pt: (b, 0, layer))
    in_specs = ([_row_spec(NSA_HP, NSA_KW), _row_spec(NSA_HP, LANES)] + [_row_spec(1, NSA_KW)] * 4
                + [win_spec, win_spec,
                   _const_spec(2, 2, CHUNK_W), _const_spec(2, CHUNK_W, 2 * HALF_W), _const_spec(2, HALF_W, NSA_KW),
                   _const_spec(1, NSA_KW), _const_spec(LANES, LANES),
                   _const_spec(NSA_HP, LANES), _const_spec(NSA_HP, PAST_LEN), _const_spec(NSA_HP, WINDOW),
                   _const_spec(NSA_HP, LANES), _const_spec(LANES, LANES), _const_spec(LANES, PAST_LEN),
                   _const_spec(GRP_ROWS, NSA_HP), _const_spec(NSA_HP, GRP_ROWS)]
                + _page_specs(layer, NSA_KW) * 4)
    args = ([qblk, gates, sk_new, sv_new, wk_new, wv_new, win_k, win_v, pe_x, w1_x, w2_x, gain_row, _seg_ones(),
             cbias, sbias, wbias, bnew, _overlap_matrix(), _block_expand(PAST_LEN), g2.astype(F32), p16]
            + [c for c in (cache_ck, cache_cv, cache_sk, cache_sv) for _ in range(REQ * N_PAGES)])
    return _decode_call(_nsa_decode_kernel, n_req, page_table, in_specs, args,
                        jax.ShapeDtypeStruct((n_req, NSA_HP, HEAD_DIM), F32), _row_spec(NSA_HP, HEAD_DIM))


KEYS_PAD = PAST_LEN + LANES


def _dsa_score_kernel(pt_ref, qi_ref, wi_ref, kin_ref, *refs):
    o_ref = refs[REQ * N_PAGES]
    lane = lax.broadcasted_iota(jnp.int32, (1, LANES), 1)
    for r in range(REQ):
        pages, = _request_pages(refs, 1, r)
        qi = qi_ref[r]
        wi = wi_ref[r][:, 0:1]
        parts = []
        for p in range(N_PAGES):
            s = lax.dot_general(qi, pages[p][0], NT_DIMS, preferred_element_type=F32, precision=HIGHEST)
            parts.append(jnp.sum(wi * jnp.maximum(s * IDX_DIM ** -0.5, 0.0), axis=0, keepdims=True))
        s_new = jnp.sum(qi * kin_ref[r], axis=1, keepdims=True) * IDX_DIM ** -0.5
        new = jnp.sum(wi * jnp.maximum(s_new, 0.0), axis=0, keepdims=True)
        parts.append(jnp.where(lane == 0, new, 0.0))
        o_ref[r] = jnp.concatenate(parts, axis=1) * IDX_HEADS ** -0.5


def dsa_score_decode(page_table, qi, wi, ki_new, cache_idx):
    n_req = qi.shape[0]
    in_specs = ([_row_spec(IDX_HEADS, LANES), _row_spec(IDX_HEADS, LANES), _row_spec(1, LANES)]
                + _page_specs(0, LANES))
    args = [qi, wi, ki_new] + [cache_idx] * (REQ * N_PAGES)
    return _decode_call(_dsa_score_kernel, n_req, page_table, in_specs, args,
                        jax.ShapeDtypeStruct((n_req, 1, KEYS_PAD), F32), _row_spec(1, KEYS_PAD))


def _topk_rows_kernel(k, s_ref, o_ref):
    score = s_ref[...]
    pos = lax.broadcasted_iota(jnp.int32, score.shape, 1)
    keep = _top_k_mask(score, pos <= PAST_LEN, pos, k)
    o_ref[...] = jnp.where(keep, 0.0, NEG_INF)


def dsa_topk_decode(score):
    k = min(DSA_TOPK, (PAST_LEN + 1) // 4)
    return pl.pallas_call(
        functools.partial(_topk_rows_kernel, k),
        out_shape=jax.ShapeDtypeStruct(score.shape, F32),
        compiler_params=pltpu.CompilerParams(vmem_limit_bytes=VMEM_LIMIT),
    )(score)


def _dsa_decode_kernel(pt_ref, q_ref, kn_ref, vn_ref, m_ref, bias_ref, bnew_ref, *refs):
    o_ref = refs[2 * REQ * N_PAGES]
    for r in range(REQ):
        k_pages, v_pages = _request_pages(refs, 2, r)
        mask = m_ref[r]
        new_add = bnew_ref[:, 0:1] + mask[:, PAST_LEN:PAST_LEN + 1]
        o_full = _paged_attention(q_ref[r], k_pages, v_pages, bias_ref,
                                  functools.partial(lambda m, p: m[:, p * PAGE_SIZE:(p + 1) * PAGE_SIZE], mask),
                                  kn_ref[r], vn_ref[r], new_add)
        o_ref[r] = _group_lanes(o_full, DSA_GROUP)


def dsa_attn_decode(page_table, qblk, k_new, v_new, mask, bias, bnew, cache_k, cache_v, layer):
    n_req = qblk.shape[0]
    in_specs = ([_row_spec(DSA_HEADS, DSA_KW), _row_spec(1, DSA_KW), _row_spec(1, DSA_KW), _row_spec(1, KEYS_PAD),
                 _const_spec(DSA_HEADS, PAST_LEN), _const_spec(DSA_HEADS, LANES)]
                + _page_specs(layer, DSA_KW) * 2)
    args = [qblk, k_new, v_new, mask, bias, bnew] + [cache_k] * (REQ * N_PAGES) + [cache_v] * (REQ * N_PAGES)
    return _decode_call(_dsa_decode_kernel, n_req, page_table, in_specs, args,
                        jax.ShapeDtypeStruct((n_req, DSA_HEADS, HEAD_DIM), F32), _row_spec(DSA_HEADS, HEAD_DIM))


EVEN_WP = 2432
EVEN_MODES = ("norm",) * 4 + ("raw",) * 2 + ("norm",) * 6 + ("raw", "raw", "norm", "raw", "norm", "raw", "sigmoid")
EVEN_CACHE_OUTS = (("cols", 256, 256), ("cols", 512, 256), ("cols", 1536, 128), ("cols", 1664, 128),
                   ("cols", 1792, 128), ("cols", 1920, 128), ("cols", 2048, 128), ("cols", 2176, 128))
EVEN_OUTS_PROMPT = EVEN_CACHE_OUTS + (
    ("cols", 2304, 128), ("heads", 0, 4, F32, 1.0), ("heads", 256, 4, F32, 1.0), ("heads", 512, 4, BF16, 1.0),
    ("heads", 768, 12, F32, 1.0), ("heads", 1792, 2, BF16, 1.0), ("heads", 1920, 2, BF16, 1.0),
    ("heads", 2048, 2, BF16, 1.0), ("heads", 2176, 2, BF16, 1.0))
EVEN_OUTS_DECODE = EVEN_CACHE_OUTS + (("cols", 2304, 128), ("cols", 0, 256), ("cols", 768, 768))
ODD_WP = 2304
ODD_MODES = ("norm",) * 10 + ("raw",) * 8
ODD_CACHE_OUTS = (("cols", 1024, 256), ("cols", 1280, 256), ("cols", 2048, 128))
ODD_OUTS_PROMPT = ODD_CACHE_OUTS + (
    ("cols", 2176, 128), ("heads", 0, 16, BF16, QK_SCALE), ("heads", 1024, 4, BF16, 1.0),
    ("heads", 1280, 4, BF16, 1.0), ("heads", 1536, 8, F32, 1.0))
ODD_OUTS_DECODE = ODD_CACHE_OUTS + (("cols", 2176, 128), ("cols", 0, 1024), ("cols", 1536, 512))


def _even_weights(w_in, moba_g, nsa_g):
    w = jnp.pad(w_in, ((0, 0), (0, EVEN_WP - w_in.shape[1]))).astype(BF16)
    z = lambda n: jnp.zeros((n,), F32)
    gain = jnp.concatenate([jnp.tile(moba_g[0], 4), jnp.tile(moba_g[1], 4), z(256), jnp.tile(nsa_g[0], 12), z(256),
                            jnp.tile(nsa_g[2], 2), z(128), jnp.tile(nsa_g[3], 2), z(256)])
    return w, gain.reshape(1, EVEN_WP)


def _odd_weights(w_in, qk_g):
    zc = lambda n: jnp.zeros((D_MODEL, n), F32)
    w = jnp.concatenate([w_in[:, :2112], zc(64), w_in[:, 2112:], zc(ODD_WP - 2176 - IDX_HEADS)], axis=1).astype(BF16)
    gain = jnp.concatenate([jnp.tile(qk_g[0], 16), jnp.tile(qk_g[1], 4), jnp.zeros((ODD_WP - 1280,), F32)])
    return w, gain.reshape(1, ODD_WP)


def _block_queries(q, n_heads, heads_per_group, rows):
    b = q.shape[0]
    n_groups = n_heads // heads_per_group
    qh = q.reshape(b, n_heads, 1, HEAD_DIM)
    grp = (jnp.arange(n_heads)[:, None] // heads_per_group == jnp.arange(n_groups)[None, :]).astype(F32)
    blk = (qh * grp[None, :, :, None]).reshape(b, n_heads, n_groups * HEAD_DIM)
    return jnp.pad(blk, ((0, 0), (0, rows - n_heads), (0, 0)))


def _decode_heads(o, n_heads):
    return o[:, :n_heads].transpose(1, 0, 2)[None]


def kernel(x_prompt, x_sample, cache_moba_k, cache_moba_v, cache_nsa_cmp_k, cache_nsa_cmp_v, cache_nsa_slc_k,
           cache_nsa_slc_v, state_nsa_win_k, state_nsa_win_v, cache_dsa_k, cache_dsa_v, cache_dsa_idx_k, page_table,
           c_prompt, c_sample, bias_table, norm_gain, ada_w, ada_b, ffn_w_in, ffn_w_out, even_w_in, even_w_out,
           moba_qk_gain, nsa_qk_gain, nsa_cmp_pe, nsa_cmp_w1, nsa_cmp_w2, odd_w_in, odd_w_out, dsa_qk_gain):
    bp, t, _ = x_prompt.shape
    bs = x_sample.shape[0]

    tab_m, tab_n = bias_table[:, :MOBA_HEADS], bias_table[:, MOBA_HEADS:]
    bias_m = _bias_tiles(tab_m, MOBA_BLOCK)[:, :3]
    bias_d = _bias_tiles(bias_table, TQ)
    bias_n = bias_d[MOBA_HEADS:]
    cmp_end = jnp.arange(LANES) * CMP_STRIDE + CMP_LEN - 1
    cbias = _rel_bias(jnp.arange(t)[:, None] - cmp_end[None, :], tab_n)
    pad_rows = lambda a, rows: jnp.pad(a, ((0, rows - a.shape[0]), (0, 0)))
    dec_bias_d = _rel_bias(PAST_LEN - jnp.arange(PAST_LEN), bias_table)
    dec_bias_m = pad_rows(dec_bias_d[:MOBA_HEADS], MOBA_HP)
    dec_bias_n = pad_rows(dec_bias_d[MOBA_HEADS:], NSA_HP)
    dec_cbias = pad_rows(_rel_bias(PAST_LEN - cmp_end, tab_n), NSA_HP)
    win_dist = WINDOW - jnp.arange(WINDOW)
    dec_wbias = pad_rows(jnp.where(win_dist < WINDOW, _rel_bias(win_dist, tab_n), NEG_INF), NSA_HP)
    new_bias = jnp.broadcast_to(bias_table[0][:, None], (DSA_HEADS, LANES))
    bnew_m = pad_rows(new_bias[:MOBA_HEADS], MOBA_HP)
    bnew_n = pad_rows(new_bias[MOBA_HEADS:], NSA_HP)

    flat = lambda c: c.reshape(c.shape[0], c.shape[1], -1)
    pool_mk, pool_mv = flat(cache_moba_k), flat(cache_moba_v)
    pool_ck, pool_cv = flat(cache_nsa_cmp_k), flat(cache_nsa_cmp_v)
    pool_sk, pool_sv = flat(cache_nsa_slc_k), flat(cache_nsa_slc_v)
    pool_dk, pool_dv, pool_di = flat(cache_dsa_k), flat(cache_dsa_v), flat(cache_dsa_idx_k)
    win_k_all, win_v_all = flat(state_nsa_win_k), flat(state_nsa_win_v)

    mods = adaln_all(jnp.concatenate([c_prompt, c_sample], axis=0), ada_w, ada_b)
    mods = mods.reshape(DEPTH, bp + bs, 3, 3, D_MODEL)
    ffn_in = ffn_w_in.astype(BF16)
    ffn_out = ffn_w_out.astype(BF16)

    yp = x_prompt
    ys = x_sample.reshape(1, bs, D_MODEL)
    rows_ep = [[] for _ in range(8)]
    rows_es = [[] for _ in range(8)]
    rows_op = [[] for _ in range(3)]
    rows_os = [[] for _ in range(3)]
    for li in range(DEPTH):
        mp = mods[li, :bp][:, None]
        ms = mods[li, bp:][None]
        mod = lambda m, s, k: m[:, :, s, k]

        def ffn(y, m, s, w_idx):
            return ffn_half(y, mod(m, s, 0), mod(m, s, 1), mod(m, s, 2), norm_gain[li, s], ffn_in[li, w_idx],
                            ffn_out[li, w_idx])

        yp, ys = ffn(yp, mp, 0, 0), ffn(ys, ms, 0, 0)
        if li % 2 == 0:
            e = li // 2
            w_p, gain_cols = _even_weights(even_w_in[e], moba_qk_gain[e], nsa_qk_gain[e])
            w_o = even_w_out[e].astype(BF16)
            cmp_w = _compress_weights(nsa_cmp_pe[e], nsa_cmp_w1[e], nsa_cmp_w2[e])
            cmp_gain = jnp.tile(nsa_qk_gain[e, 1], NSA_KV).reshape(1, NSA_KW)
            project = lambda y, m, outs: mixer_project(y, mod(m, 1, 0), mod(m, 1, 1), norm_gain[li, 1], w_p,
                                                       gain_cols, EVEN_MODES, outs)
            pp = project(yp, mp, EVEN_OUTS_PROMPT)
            gl, mq_h, mk_h, mv_h, nq_h, sk_h, sv_h, wk_h, wv_h = pp[8:]
            o_m = moba_prompt(mq_h, mk_h, mv_h, bias_m)
            ck, cv = nsa_compress_prompt(pp[2], pp[3], *cmp_w, cmp_gain)
            o_n = nsa_prompt(nq_h, gl, ck, cv, sk_h, sv_h, wk_h, wv_h, bias_n, cbias)
            pieces_p = [o_m, o_n]
            wb = min(WINDOW, t)
            new_p = list(pp[:6]) + [pp[6][:, t - wb:], pp[7][:, t - wb:]]
            ps = [a.reshape(bs, 1, a.shape[-1]) for a in project(ys, ms, EVEN_OUTS_DECODE)]
            mk_s, mv_s, ckr_s, cvr_s, sk_s, sv_s, wk_s, wv_s, gl_s, mq_s, nq_s = ps
            o_ms = moba_decode(page_table, _block_queries(mq_s[:, 0], MOBA_HEADS, 1, MOBA_HP), mk_s, mv_s,
                               dec_bias_m, bnew_m, pool_mk, pool_mv, e)
            gates_s = jnp.pad(gl_s[:, 0, :3 * NSA_HEADS].reshape(bs, NSA_HEADS, 3),
                              ((0, 0), (0, NSA_HP - NSA_HEADS), (0, LANES - 3)))
            o_ns = nsa_decode(page_table, _block_queries(nq_s[:, 0], NSA_HEADS, NSA_GROUP, NSA_HP), gates_s,
                              sk_s, sv_s, wk_s, wv_s, win_k_all, win_v_all, cmp_w, cmp_gain,
                              dec_cbias, dec_bias_n, dec_wbias, bnew_n, pool_ck, pool_cv, pool_sk, pool_sv, e)
            pieces_s = [_decode_heads(o_ms, MOBA_HEADS), _decode_heads(o_ns, NSA_HEADS)]
            keep = min(WINDOW, PAST_LEN + 1)
            lanes_e = slice(e * NSA_KW, (e + 1) * NSA_KW)
            new_s = [mk_s, mv_s, ckr_s, cvr_s, sk_s, sv_s,
                     jnp.concatenate([win_k_all[:, :, lanes_e], wk_s], axis=1)[:, -keep:],
                     jnp.concatenate([win_v_all[:, :, lanes_e], wv_s], axis=1)[:, -keep:]]
            for acc, r in zip(rows_ep, new_p):
                acc.append(r)
            for acc, r in zip(rows_es, new_s):
                acc.append(r)
        else:
            o = li // 2
            w_p, gain_cols = _odd_weights(odd_w_in[o], dsa_qk_gain[o])
            w_o = odd_w_out[o].astype(BF16)
            project = lambda y, m, outs: mixer_project(y, mod(m, 1, 0), mod(m, 1, 1), norm_gain[li, 1], w_p,
                                                       gain_cols, ODD_MODES, outs)
            pp = project(yp, mp, ODD_OUTS_PROMPT)
            wi, q_h, k_h, v_h, qi_h = pp[3:]
            keep_mask = dsa_index_prompt(qi_h, wi, pp[2])
            pieces_p = [dsa_attn_prompt(q_h, k_h, v_h, bias_d, keep_mask)]
            ps = [a.reshape(bs, 1, a.shape[-1]) for a in project(ys, ms, ODD_OUTS_DECODE)]
            k_s, v_s, ki_s, wi_s, q_s, qi_s = ps
            idx_lanes = lambda a: jnp.pad(a, ((0, 0), (0, 0), (o * IDX_DIM, LANES - (o + 1) * IDX_DIM)))
            wi_rows = jnp.broadcast_to(wi_s[:, 0, :IDX_HEADS, None], (bs, IDX_HEADS, LANES))
            score = dsa_score_decode(page_table, idx_lanes(qi_s[:, 0].reshape(bs, IDX_HEADS, IDX_DIM)), wi_rows,
                                     idx_lanes(ki_s[:, :, :IDX_DIM]), pool_di)
            keep_s = dsa_topk_decode(score.reshape(bs, KEYS_PAD)).reshape(bs, 1, KEYS_PAD)
            o_ds = dsa_attn_decode(page_table, _block_queries(q_s[:, 0], DSA_HEADS, DSA_GROUP, DSA_HEADS), k_s, v_s,
                                   keep_s, dec_bias_d, new_bias, pool_dk, pool_dv, o)
            pieces_s = [_decode_heads(o_ds, DSA_HEADS)]
            for acc, r in zip(rows_op, (pp[0], pp[1], pp[2][..., :IDX_DIM])):
                acc.append(r)
            for acc, r in zip(rows_os, (k_s, v_s, ki_s[..., :IDX_DIM])):
                acc.append(r)
        yp = mixer_merge(yp, pieces_p, mod(mp, 1, 2), w_o)
        ys = mixer_merge(ys, pieces_s, mod(ms, 1, 2), w_o)
        yp, ys = ffn(yp, mp, 2, 1), ffn(ys, ms, 2, 1)

    def stack_layers(rows, n_heads):
        return jnp.stack([r.reshape(r.shape[0], r.shape[1], n_heads, HEAD_DIM) for r in rows], axis=2)

    even_heads = (MOBA_HEADS, MOBA_HEADS) + (NSA_KV,) * 6
    moba_k_p, moba_v_p, cmp_k_p, cmp_v_p, slc_k_p, slc_v_p, win_k_p, win_v_p = [
        stack_layers(r, n) for r, n in zip(rows_ep, even_heads)]
    moba_k_s, moba_v_s, cmp_k_s, cmp_v_s, slc_k_s, slc_v_s, win_k_s, win_v_s = [
        stack_layers(r, n) for r, n in zip(rows_es, even_heads)]
    dsa_k_p, dsa_v_p = stack_layers(rows_op[0], DSA_KV), stack_layers(rows_op[1], DSA_KV)
    dsa_k_s, dsa_v_s = stack_layers(rows_os[0], DSA_KV), stack_layers(rows_os[1], DSA_KV)
    dsa_idx_k_p, dsa_idx_k_s = jnp.stack(rows_op[2], axis=2), jnp.stack(rows_os[2], axis=2)
    return (yp, ys.reshape(bs, 1, D_MODEL), moba_k_p, moba_k_s, moba_v_p, moba_v_s, cmp_k_p, cmp_k_s, cmp_v_p,
            cmp_v_s, slc_k_p, slc_k_s, slc_v_p, slc_v_s, win_k_p, win_k_s, win_v_p, win_v_s,
            dsa_k_p, dsa_k_s, dsa_v_p, dsa_v_s, dsa_idx_k_p, dsa_idx_k_s)
```
